```python
import numpy as np
import jax
import jax.numpy as jnp
from jax import lax

D_MODEL = 1024
BATCH = 32
SEQ = 256
DEPTH = 4
DEC_BATCH = 8
DEC_SEQ = 4096
PAST_LEN = 512

GRID_W = 64
HEAD_DIM = 64
BRANCH_W = D_MODEL // 4
POOL_WINDOWS = (2, 4, 8, 16)
POOL_GROUP_W = BRANCH_W // len(POOL_WINDOWS)
NAT_HEADS = BRANCH_W // HEAD_DIM
NAT_ROWS = 8
NAT_COLS = 16
CONV_WIDTH = 31
GQA_HEADS = BRANCH_W // HEAD_DIM
GQA_KV_HEADS = GQA_HEADS // 2
ROPE_THETA = 10000.0
Q_BLOCK = 128
LN_EPS = 1e-5
RMS_EPS = 1e-6
NEG_INF = -1e30
DEEPNORM_ALPHA = (2 * DEPTH) ** 0.25
DEEPNORM_BETA = (8 * DEPTH) ** -0.25
IN_SPLITS = (BRANCH_W, BRANCH_W,
             BRANCH_W, BRANCH_W, BRANCH_W, BRANCH_W,
             BRANCH_W, BRANCH_W, BRANCH_W,
             GQA_HEADS * HEAD_DIM, GQA_KV_HEADS * HEAD_DIM, GQA_KV_HEADS * HEAD_DIM, BRANCH_W)
IN_WIDTH = sum(IN_SPLITS)
SPLIT_POINTS = tuple(np.cumsum(IN_SPLITS)[:-1].tolist())

kernel_name = 'hybrid_diffusion_prefix_step'


def layer_norm(x, g, b):
    xf = x.astype(jnp.float32)
    mu = jnp.mean(xf, axis=-1, keepdims=True)
    var = jnp.mean(jnp.square(xf - mu), axis=-1, keepdims=True)
    return ((xf - mu) * lax.rsqrt(var + LN_EPS)).astype(x.dtype) * g + b


def rms_norm(x, g):
    xf = x.astype(jnp.float32)
    return (xf * lax.rsqrt(jnp.mean(xf * xf, axis=-1, keepdims=True) + RMS_EPS)).astype(x.dtype) * g


def axial_rope(x):
    L = x.shape[1]
    half = HEAD_DIM // 2
    nf = half // 2
    t = jnp.arange(L)
    inv = ROPE_THETA ** (-jnp.arange(nf, dtype=jnp.float32) * 2.0 / half)

    def rot(xa, pos):
        ang = pos.astype(jnp.float32)[:, None] * inv[None, :]
        cos = jnp.cos(ang)[None, :, None, :].astype(x.dtype)
        sin = jnp.sin(ang)[None, :, None, :].astype(x.dtype)
        x1, x2 = xa[..., :nf], xa[..., nf:]
        return jnp.concatenate([x1 * cos - x2 * sin, x2 * cos + x1 * sin], axis=-1)

    return jnp.concatenate([rot(x[..., :half], t // GRID_W), rot(x[..., half:], t % GRID_W)], axis=-1)


def blocked_attention(q, k, v):
    B, Lq, H, dh = q.shape
    KV = k.shape[2]
    G = H // KV
    qb = jnp.moveaxis(q.reshape(B, Lq // Q_BLOCK, Q_BLOCK, KV, G, dh), 1, 0)

    def one_block(qi):
        s = jnp.einsum('bqkgd,bskd->bkgqs', qi, k).astype(jnp.float32) * (dh ** -0.5)
        p = jax.nn.softmax(s, axis=-1).astype(v.dtype)
        return jnp.einsum('bkgqs,bskd->bqkgd', p, v)

    o = lax.map(one_block, qb)
    return jnp.moveaxis(o, 0, 1).reshape(B, Lq, H * dh)


def nat_attention(q, k, v, k_ctx, v_ctx, bias_tab):
    B, L, H, dh = q.shape
    rows = L // GRID_W
    wr = min(NAT_ROWS, rows)
    r = np.arange(rows)
    row_start = np.clip(r - wr // 2, 0, rows - wr)
    row_idx = row_start[:, None] + np.arange(wr)[None, :]
    col = np.arange(GRID_W)
    col_start = np.clip(col - NAT_COLS // 2, 0, GRID_W - NAT_COLS)
    col_in = (col[None, :] >= col_start[:, None]) & (col[None, :] < col_start[:, None] + NAT_COLS)
    dc = np.clip(col[None, :] - col[:, None], -(NAT_COLS - 1), NAT_COLS - 1) + NAT_COLS - 1
    dr = row_idx - r[:, None] + NAT_ROWS - 1
    scale = dh ** -0.5

    qg = q.reshape(B, rows, GRID_W, H, dh)
    kg = jnp.take(k.reshape(B, rows, GRID_W, H, dh), row_idx, axis=1)
    vg = jnp.take(v.reshape(B, rows, GRID_W, H, dh), row_idx, axis=1)
    s_loc = jnp.einsum('brchd,brjwhd->brhcjw', qg, kg).astype(jnp.float32) * scale
    bias = bias_tab[:, dr[:, :, None, None], dc[None, None, :, :]]
    bias = jnp.transpose(bias, (1, 0, 3, 2, 4)).astype(jnp.float32)
    s_loc = jnp.where(col_in[:, None, :], s_loc + bias[None], NEG_INF)
    s_ctx = jnp.einsum('brchd,bshd->brhcs', qg, k_ctx).astype(jnp.float32) * scale
    n_loc = wr * GRID_W
    s_all = jnp.concatenate([s_loc.reshape(B, rows, H, GRID_W, n_loc), s_ctx], axis=-1)
    p = jax.nn.softmax(s_all, axis=-1).astype(v.dtype)
    p_loc = p[..., :n_loc].reshape(B, rows, H, GRID_W, wr, GRID_W)
    p_ctx = p[..., n_loc:]
    o = jnp.einsum('brhcjw,brjwhd->brchd', p_loc, vg) + jnp.einsum('brhcs,bshd->brchd', p_ctx, v_ctx)
    return o.reshape(B, L, H * dh)


def pool_mix(x, w, scale):
    B, L, C = x.shape
    xf = x.astype(jnp.float32)
    cs = jnp.concatenate([jnp.zeros((B, 1, C), jnp.float32), jnp.cumsum(xf, axis=1)], axis=1)
    t = np.arange(L)
    outs = []
    for g, win in enumerate(POOL_WINDOWS):
        lo = np.clip(t - win // 2, 0, L)
        hi = np.clip(t - win // 2 + win, 0, L)
        sl = slice(g * POOL_GROUP_W, (g + 1) * POOL_GROUP_W)
        cg = cs[..., sl]
        mean = (cg[:, hi] - cg[:, lo]) / jnp.asarray(hi - lo, jnp.float32)[None, :, None]
        outs.append(mean - xf[..., sl])
    y = jnp.stack(outs, axis=2).astype(x.dtype)
    y = jnp.einsum('blgc,gcd->blgd', y, w)
    return y.reshape(B, L, C) * scale


def conv_mix(val, glu, w_dw, b_dw, g, b, w_pw):
    z = val * jax.nn.sigmoid(glu)
    z = lax.conv_general_dilated(z, w_dw[:, None, :], window_strides=(1,),
                                 padding=[(CONV_WIDTH // 2, CONV_WIDTH // 2)],
                                 dimension_numbers=('NWC', 'WIO', 'NWC'),
                                 feature_group_count=z.shape[-1]) + b_dw
    z = jax.nn.silu(layer_norm(z, g, b))
    return z @ w_pw


def trunk_layer(x, cond, p, ctx=None):
    B, L, _ = x.shape
    mod = jax.nn.silu(cond) @ p['w_mod'] + p['b_mod']
    shift, scale, gate = jnp.split(mod, 3, axis=-1)
    h = x * (1 + scale) + shift
    u = h @ p['w_in'] + p['b_in']
    (a_x, a_g, nq, nk, nv, n_g, c_v, c_glu, c_g, gq, gk, gv, g_g) = jnp.split(u, SPLIT_POINTS, axis=-1)

    a_out = pool_mix(a_x, p['pool_w'], p['pool_scale'])
    c_out = conv_mix(c_v, c_glu, p['conv_w'], p['conv_b'], p['conv_ln_g'], p['conv_ln_b'], p['conv_pw'])

    nq = nq.reshape(B, L, NAT_HEADS, HEAD_DIM)
    nk = nk.reshape(B, L, NAT_HEADS, HEAD_DIM)
    nv = nv.reshape(B, L, NAT_HEADS, HEAD_DIM)
    gq = rms_norm(gq.reshape(B, L, GQA_HEADS, HEAD_DIM), p['q_norm'])
    gk = rms_norm(gk.reshape(B, L, GQA_KV_HEADS, HEAD_DIM), p['k_norm'])
    gv = gv.reshape(B, L, GQA_KV_HEADS, HEAD_DIM)

    if ctx is None:
        n_out = blocked_attention(nq, nk, nv)
        g_out = blocked_attention(gq, gk, gv)
        new = (nk, nv, gk, gv)
    else:
        ck_n, cv_n, ck_g, cv_g = ctx
        n_out = nat_attention(nq, nk, nv, ck_n, cv_n, p['nat_bias'])
        g_out = blocked_attention(axial_rope(gq),
                                  jnp.concatenate([axial_rope(gk), ck_g], axis=1),
                                  jnp.concatenate([gv, cv_g], axis=1))
        new = None

    mixed = jnp.concatenate([a_out * jax.nn.silu(a_g), n_out * jax.nn.silu(n_g),
                             c_out * jax.nn.silu(c_g), g_out * jax.nn.silu(g_g)], axis=-1)
    out = mixed @ p['w_out'] + p['b_out']
    x = layer_norm(DEEPNORM_ALPHA * x + gate * out, p['ln_g'], p['ln_b'])
    return x, new


def setup_inputs(seed: int = 0) -> dict:
    key = jax.random.key(seed)
    ks = jax.random.split(key, 26)
    D = D_MODEL

    def nrm(k, shape, s):
        return jax.random.normal(k, shape, jnp.float32) * s

    return {
        'x_prompt': nrm(ks[0], (BATCH, SEQ, D), 1.0),
        'x_sample': nrm(ks[1], (DEC_BATCH, DEC_SEQ, D), 1.0),
        'c': nrm(ks[2], (DEC_BATCH, D), 1.0),
        'cache_nat_k': nrm(ks[3], (DEC_BATCH, DEPTH, PAST_LEN, NAT_HEADS, HEAD_DIM), 1.0),
        'cache_nat_v': nrm(ks[4], (DEC_BATCH, DEPTH, PAST_LEN, NAT_HEADS, HEAD_DIM), 1.0),
        'cache_gqa_k': nrm(ks[5], (DEC_BATCH, DEPTH, PAST_LEN, GQA_KV_HEADS, HEAD_DIM), 1.0),
        'cache_gqa_v': nrm(ks[6], (DEC_BATCH, DEPTH, PAST_LEN, GQA_KV_HEADS, HEAD_DIM), 1.0),
        'c_ctx': nrm(ks[7], (D,), 1.0),
        'w_mod': nrm(ks[8], (DEPTH, D, 3 * D), 0.5 * D ** -0.5),
        'b_mod': nrm(ks[9], (DEPTH, 3 * D), 0.01),
        'w_in': nrm(ks[10], (DEPTH, D, IN_WIDTH), D ** -0.5),
        'b_in': nrm(ks[11], (DEPTH, IN_WIDTH), 0.01),
        'pool_w': nrm(ks[12], (DEPTH, len(POOL_WINDOWS), POOL_GROUP_W, POOL_GROUP_W), POOL_GROUP_W ** -0.5),
        'pool_scale': 1.0 + nrm(ks[13], (DEPTH, BRANCH_W), 0.1),
        'nat_bias': nrm(ks[14], (DEPTH, NAT_HEADS, 2 * NAT_ROWS - 1, 2 * NAT_COLS - 1), 0.1),
        'q_norm': 1.0 + nrm(ks[15], (DEPTH, HEAD_DIM), 0.05),
        'k_norm': 1.0 + nrm(ks[16], (DEPTH, HEAD_DIM), 0.05),
        'conv_w': nrm(ks[17], (DEPTH, CONV_WIDTH, BRANCH_W), CONV_WIDTH ** -0.5),
        'conv_b': nrm(ks[18], (DEPTH, BRANCH_W), 0.01),
        'conv_ln_g': 1.0 + nrm(ks[19], (DEPTH, BRANCH_W), 0.05),
        'conv_ln_b': nrm(ks[20], (DEPTH, BRANCH_W), 0.01),
        'conv_pw': nrm(ks[21], (DEPTH, BRANCH_W, BRANCH_W), BRANCH_W ** -0.5 * DEEPNORM_BETA),
        'w_out': nrm(ks[22], (DEPTH, D, D), D ** -0.5 * DEEPNORM_BETA),
        'b_out': nrm(ks[23], (DEPTH, D), 0.01),
        'ln_g': 1.0 + nrm(ks[24], (DEPTH, D), 0.05),
        'ln_b': nrm(ks[25], (DEPTH, D), 0.01),
    }


def reference(x_prompt, x_sample, c, cache_nat_k, cache_nat_v, cache_gqa_k, cache_gqa_v, c_ctx,
              w_mod, b_mod, w_in, b_in, pool_w, pool_scale, nat_bias, q_norm, k_norm,
              conv_w, conv_b, conv_ln_g, conv_ln_b, conv_pw, w_out, b_out, ln_g, ln_b):
    cond_ctx = c_ctx[None, None, :]
    cond_lat = c[:, None, :]
    y_prompt = x_prompt
    y_sample = x_sample
    nat_k, nat_v, gqa_k, gqa_v = [], [], [], []
    for l in range(DEPTH):
        p = {'w_mod': w_mod[l], 'b_mod': b_mod[l], 'w_in': w_in[l], 'b_in': b_in[l],
             'pool_w': pool_w[l], 'pool_scale': pool_scale[l], 'nat_bias': nat_bias[l],
             'q_norm': q_norm[l], 'k_norm': k_norm[l], 'conv_w': conv_w[l], 'conv_b': conv_b[l],
             'conv_ln_g': conv_ln_g[l], 'conv_ln_b': conv_ln_b[l], 'conv_pw': conv_pw[l],
             'w_out': w_out[l], 'b_out': b_out[l], 'ln_g': ln_g[l], 'ln_b': ln_b[l]}
        y_prompt, (nk, nv, gk, gv) = trunk_layer(y_prompt, cond_ctx, p)
        nat_k.append(nk)
        nat_v.append(nv)
        gqa_k.append(gk)
        gqa_v.append(gv)
        ctx = (cache_nat_k[:, l], cache_nat_v[:, l], cache_gqa_k[:, l], cache_gqa_v[:, l])
        y_sample, _ = trunk_layer(y_sample, cond_lat, p, ctx)
    new_nat_k = jnp.stack(nat_k, axis=1)
    new_nat_v = jnp.stack(nat_v, axis=1)
    new_gqa_k = jnp.stack(gqa_k, axis=1)
    new_gqa_v = jnp.stack(gqa_v, axis=1)
    return (y_prompt, y_sample, new_nat_k, new_nat_v, new_gqa_k, new_gqa_v)
```

```python
import functools

import numpy as np
import jax
import jax.numpy as jnp
from jax.experimental import pallas as pl
from jax.experimental.pallas import tpu as pltpu

F32 = jnp.float32
BF16 = jnp.bfloat16

D_MODEL = 1024
DEPTH = 4
GRID_W = 64
HEAD_DIM = 64
BRANCH_W = D_MODEL // 4
POOL_WINDOWS = (2, 4, 8, 16)
POOL_GROUP_W = BRANCH_W // len(POOL_WINDOWS)
NAT_HEADS = BRANCH_W // HEAD_DIM
NAT_ROWS = 8
NAT_COLS = 16
CONV_WIDTH = 31
GQA_HEADS = BRANCH_W // HEAD_DIM
GQA_KV_HEADS = GQA_HEADS // 2
KV_W = GQA_KV_HEADS * HEAD_DIM
ROPE_THETA = 10000.0
LN_EPS = 1e-5
RMS_EPS = 1e-6
NEG_INF = -1e30
DEEPNORM_ALPHA = (2 * DEPTH) ** 0.25
SM_SCALE = HEAD_DIM ** -0.5

C_AX, C_AG = 0, 256
C_NQ, C_NK, C_NV, C_NG = 512, 768, 1024, 1280
C_CV, C_CGLU, C_CG = 1536, 1792, 2048
C_GQ, C_GK, C_GV, C_GG = 2304, 2560, 2688, 2816
IN_WIDTH = 3072

HALO = 16
VMEM_LIMIT = 56 * 1024 * 1024
MOD_ROWS = 16


def _cparams(n_axes):
    return pltpu.CompilerParams(dimension_semantics=("arbitrary",) * n_axes, vmem_limit_bytes=VMEM_LIMIT)


def _silu(x):
    return x * jax.nn.sigmoid(x)


def _dot(a, b):
    return jnp.dot(a, b, preferred_element_type=F32)


def _dot_t(a, b):
    return jax.lax.dot_general(a, b, (((1,), (1,)), ((), ())), preferred_element_type=F32)


def _mod_kernel(cond_ref, w_ref, b_ref, o_ref):
    a = _silu(cond_ref[...]).astype(BF16)
    o_ref[...] = _dot(a, w_ref[...].astype(BF16)) + b_ref[...]


def _modulation(cond, w_mod, b_mod):
    nj = 3 * D_MODEL // 1024
    return pl.pallas_call(
        _mod_kernel,
        grid=(DEPTH, nj),
        in_specs=[
            pl.BlockSpec((MOD_ROWS, D_MODEL), lambda l, j: (0, 0)),
            pl.BlockSpec((None, D_MODEL, 1024), lambda l, j: (l, 0, j)),
            pl.BlockSpec((None, 1, 1024), lambda l, j: (l, 0, j)),
        ],
        out_specs=pl.BlockSpec((None, MOD_ROWS, 1024), lambda l, j: (l, 0, j)),
        out_shape=jax.ShapeDtypeStruct((DEPTH, MOD_ROWS, 3 * D_MODEL), F32),
        compiler_params=_cparams(2),
        name="modulation",
    )(cond, w_mod, b_mod.reshape(DEPTH, 1, 3 * D_MODEL))


def _head_rms(x, seg, gain):
    ss = x * x
    hi = ss.astype(BF16)
    lo = (ss - hi.astype(F32)).astype(BF16)
    tot = _dot(hi, seg) + _dot(lo, seg)
    return x * jax.lax.rsqrt(tot * (1.0 / HEAD_DIM) + RMS_EPS) * gain


def _rope(x, cos, sin_a, sin_b):
    n = x.shape[-1]
    return x * cos + pltpu.roll(x, n - 16, 1) * sin_a + pltpu.roll(x, 16, 1) * sin_b


def _inproj_kernel(*refs, rope, emit_kv):
    x_ref, sh_ref, sc_ref, w_ref, b_ref, qn_ref, kn_ref, seg_ref = refs[:8]
    refs = refs[8:]
    if rope:
        cos_ref, sa_ref, sb_ref = refs[:3]
        refs = refs[3:]
    ax_ref, z_ref, gates_ref, nat_ref, gqa_ref = refs[:5]
    refs = refs[5:]

    h = (x_ref[...] * (1.0 + sc_ref[...]) + sh_ref[...]).astype(BF16)

    def proj(c0, n):
        return _dot(h, w_ref[:, c0:c0 + n]) + b_ref[:, c0:c0 + n]

    ax_ref[...] = proj(C_AX, BRANCH_W)
    cv = proj(C_CV, BRANCH_W)
    z_ref[...] = cv * jax.nn.sigmoid(proj(C_CGLU, BRANCH_W))
    for i, c0 in enumerate((C_AG, C_NG, C_CG, C_GG)):
        gates_ref[:, i * BRANCH_W:(i + 1) * BRANCH_W] = _silu(proj(c0, BRANCH_W)).astype(BF16)

    nat_ref[:, 0:256] = (proj(C_NQ, BRANCH_W) * SM_SCALE).astype(BF16)
    nk = proj(C_NK, BRANCH_W)
    nv = proj(C_NV, BRANCH_W)
    nat_ref[:, 256:512] = nk.astype(BF16)
    nat_ref[:, 512:768] = nv.astype(BF16)

    gq = _head_rms(proj(C_GQ, BRANCH_W), seg_ref[...], qn_ref[...])
    gk = _head_rms(proj(C_GK, KV_W), seg_ref[0:KV_W, 0:KV_W], kn_ref[...])
    gv = proj(C_GV, KV_W)
    if emit_kv:
        nk_ref, nv_ref, gk_ref, gv_ref = refs
        nk_ref[...] = nk
        nv_ref[...] = nv
        gk_ref[...] = gk
        gv_ref[...] = gv
    if rope:
        cos, sa, sb = cos_ref[...], sa_ref[...], sb_ref[...]
        gk = _rope(gk, cos, sa, sb)
        gq = _rope(gq, jnp.concatenate([cos, cos], axis=1), jnp.concatenate([sa, sa], axis=1),
                   jnp.concatenate([sb, sb], axis=1))
    gqa_ref[:, 0:256] = (gq * SM_SCALE).astype(BF16)
    gqa_ref[:, 256:384] = gk.astype(BF16)
    gqa_ref[:, 384:512] = gv.astype(BF16)


def _inproj(x, mod, mod_row0, layer, w_in, b_in, qn, kn, seg, rope_tabs, emit_kv, tm):
    B, L, D = x.shape
    nt = L // tm
    rope = rope_tabs is not None
    tok = lambda w: pl.BlockSpec((None, tm, w), lambda i, b: (b, i, 0))
    const = lambda shape: pl.BlockSpec(shape, lambda i, b: (0,) * len(shape))
    in_specs = [
        tok(D),
        pl.BlockSpec((None, None, 1, D), lambda i, b: (layer, mod_row0 + b, 0, 0)),
        pl.BlockSpec((None, None, 1, D), lambda i, b: (layer, mod_row0 + b, 0, 1)),
        const((D, IN_WIDTH)), const((1, IN_WIDTH)), const((1, BRANCH_W)), const((1, KV_W)),
        const((BRANCH_W, BRANCH_W)),
    ]
    args = [x, mod, mod, w_in, b_in, qn, kn, seg]
    if rope:
        in_specs += [pl.BlockSpec((tm, KV_W), lambda i, b: (i, 0))] * 3
        args += list(rope_tabs)
    out_specs = [tok(BRANCH_W), tok(BRANCH_W), tok(D), tok(3 * BRANCH_W), tok(BRANCH_W + 2 * KV_W)]
    out_shape = [jax.ShapeDtypeStruct((B, L, BRANCH_W), F32), jax.ShapeDtypeStruct((B, L, BRANCH_W), F32),
                 jax.ShapeDtypeStruct((B, L, D), BF16), jax.ShapeDtypeStruct((B, L, 3 * BRANCH_W), BF16),
                 jax.ShapeDtypeStruct((B, L, BRANCH_W + 2 * KV_W), BF16)]
    if emit_kv:
        out_specs += [tok(BRANCH_W), tok(BRANCH_W), tok(KV_W), tok(KV_W)]
        out_shape += [jax.ShapeDtypeStruct((B, L, w), F32) for w in (BRANCH_W, BRANCH_W, KV_W, KV_W)]
    return pl.pallas_call(
        functools.partial(_inproj_kernel, rope=rope, emit_kv=emit_kv),
        grid=(nt, B),
        in_specs=in_specs,
        out_specs=out_specs,
        out_shape=out_shape,
        compiler_params=_cparams(2),
        name="inproj_lat" if rope else "inproj_ctx",
    )(*args)


LOCAL_TM = 256
LOCAL_RC = 64


def _local_kernel(ax_ref, axp_ref, axn_ref, z_ref, zp_ref, zn_ref, pw_ref, ps_ref, cw_ref, cb_ref, lg_ref, lb_ref,
                  cpw_ref, a_out_ref, c_out_ref, abuf, zbuf, ybuf, cbuf, *, seq_len):
    tm = LOCAL_TM
    i = pl.program_id(1)
    first = i == 0
    last = i == pl.num_programs(1) - 1
    zeros = jnp.zeros((HALO, BRANCH_W), F32)
    for buf, cur, prv, nxt in ((abuf, ax_ref, axp_ref, axn_ref), (zbuf, z_ref, zp_ref, zn_ref)):
        buf[0:HALO, :] = jnp.where(first, zeros, prv[...])
        buf[HALO:HALO + tm, :] = cur[...]
        buf[HALO + tm:, :] = jnp.where(last, zeros, nxt[...])

    lane = jax.lax.broadcasted_iota(jnp.int32, (LOCAL_RC, 128), 1)
    low = lane < POOL_GROUP_W
    for r0 in range(0, tm, LOCAL_RC):
        t = i * tm + r0 + jax.lax.broadcasted_iota(jnp.int32, (LOCAL_RC, 128), 0)

        def sh(off, c0):
            return abuf[HALO + r0 + off:HALO + r0 + off + LOCAL_RC, c0:c0 + 128]

        def centred(total, half, x):
            cnt = jnp.minimum(t + half, seq_len) - jnp.maximum(t - half, 0)
            return total / cnt.astype(F32) - x

        x0 = sh(0, 0)
        s2 = x0 + sh(-1, 0)
        s4 = s2 + sh(-2, 0) + sh(1, 0)
        ybuf[r0:r0 + LOCAL_RC, 0:128] = centred(jnp.where(low, s2, s4), jnp.where(low, 1, 2), x0)
        x1 = sh(0, 128)
        s8 = x1
        for off in (-4, -3, -2, -1, 1, 2, 3):
            s8 = s8 + sh(off, 128)
        s16 = s8
        for off in (-8, -7, -6, -5, 4, 5, 6, 7):
            s16 = s16 + sh(off, 128)
        ybuf[r0:r0 + LOCAL_RC, 128:256] = centred(jnp.where(low, s8, s16), jnp.where(low, 4, 8), x1)

        acc = jnp.zeros((LOCAL_RC, BRANCH_W), F32) + cb_ref[...]
        for j in range(CONV_WIDTH):
            off = j - CONV_WIDTH // 2
            acc = acc + zbuf[HALO + r0 + off:HALO + r0 + off + LOCAL_RC, :] * cw_ref[j:j + 1, :]
        cbuf[r0:r0 + LOCAL_RC, :] = acc

    a = _dot(ybuf[...].astype(BF16), pw_ref[...]) * ps_ref[...]
    a_out_ref[...] = a.astype(BF16)

    cz = cbuf[...]
    mu = jnp.mean(cz, axis=-1, keepdims=True)
    d = cz - mu
    var = jnp.mean(d * d, axis=-1, keepdims=True)
    zn = _silu(d * jax.lax.rsqrt(var + LN_EPS) * lg_ref[...] + lb_ref[...])
    c_out_ref[...] = _dot(zn.astype(BF16), cpw_ref[...]).astype(BF16)


def _local_mix(ax, z, pool_bd, pool_scale, conv_w, conv_b, ln_g, ln_b, conv_pw):
    B, L, C = ax.shape
    tm = LOCAL_TM
    nt = L // tm
    hb = tm // HALO
    nhb = L // HALO
    tok = pl.BlockSpec((None, tm, C), lambda b, i: (b, i, 0))
    prv = pl.BlockSpec((None, HALO, C), lambda b, i: (b, jnp.maximum(i * hb - 1, 0), 0))
    nxt = pl.BlockSpec((None, HALO, C), lambda b, i: (b, jnp.minimum((i + 1) * hb, nhb - 1), 0))
    const = lambda shape: pl.BlockSpec(shape, lambda b, i: (0,) * len(shape))
    return pl.pallas_call(
        functools.partial(_local_kernel, seq_len=L),
        grid=(B, nt),
        in_specs=[tok, prv, nxt, tok, prv, nxt, const((C, C)), const((1, C)), const((CONV_WIDTH, C)), const((1, C)),
                  const((1, C)), const((1, C)), const((C, C))],
        out_specs=[tok, tok],
        out_shape=[jax.ShapeDtypeStruct((B, L, C), BF16)] * 2,
        scratch_shapes=[pltpu.VMEM((tm + 2 * HALO, C), F32), pltpu.VMEM((tm + 2 * HALO, C), F32),
                        pltpu.VMEM((tm, C), F32), pltpu.VMEM((tm, C), F32)],
        compiler_params=_cparams(2),
        name="local_mix",
    )(ax, ax, ax, z, z, z, pool_bd, pool_scale, conv_w, conv_b, ln_g, ln_b, conv_pw)


def _head_mask(shape, h):
    lane = jax.lax.broadcasted_iota(jnp.int32, shape, 1)
    return (lane >= h * HEAD_DIM) & (lane < (h + 1) * HEAD_DIM)


def _softmax_pv(s, v):
    m = jnp.max(s, axis=-1, keepdims=True)
    p = jnp.exp(s - m)
    l = jnp.sum(p, axis=-1, keepdims=True)
    return _dot(p.astype(BF16), v) / l


def _attn_ctx_kernel(nat_ref, gqa_ref, n_out_ref, g_out_ref):
    q, k, v = nat_ref[:, 0:256], nat_ref[:, 256:512], nat_ref[:, 512:768]
    acc = jnp.zeros(q.shape, F32)
    for h in range(NAT_HEADS):
        msk = _head_mask(q.shape, h)
        o = _softmax_pv(_dot_t(jnp.where(msk, q, jnp.zeros_like(q)), k), v)
        acc = acc + jnp.where(msk, o, 0.0)
    n_out_ref[...] = acc.astype(BF16)

    k, v = gqa_ref[:, 256:384], gqa_ref[:, 384:512]
    for side in range(2):
        q = gqa_ref[:, side * 128:(side + 1) * 128]
        acc = jnp.zeros(q.shape, F32)
        for kv in range(GQA_KV_HEADS):
            msk = _head_mask(q.shape, kv)
            o = _softmax_pv(_dot_t(jnp.where(msk, q, jnp.zeros_like(q)), k), v)
            acc = acc + jnp.where(msk, o, 0.0)
        g_out_ref[:, side * 128:(side + 1) * 128] = acc.astype(BF16)


def _attn_ctx(nat, gqa):
    B, L, _ = nat.shape
    return pl.pallas_call(
        _attn_ctx_kernel,
        grid=(B,),
        in_specs=[pl.BlockSpec((None, L, 3 * BRANCH_W), lambda b: (b, 0, 0)),
                  pl.BlockSpec((None, L, BRANCH_W + 2 * KV_W), lambda b: (b, 0, 0))],
        out_specs=[pl.BlockSpec((None, L, BRANCH_W), lambda b: (b, 0, 0))] * 2,
        out_shape=[jax.ShapeDtypeStruct((B, L, BRANCH_W), BF16)] * 2,
        compiler_params=_cparams(1),
        name="attn_ctx",
    )(nat, gqa)


NAT_RB = 8
NAT_NLOC = NAT_ROWS * GRID_W


def _nat_kernel(q_ref, k_ref, v_ref, kc_ref, vc_ref, bias_ref, o_ref, *, rows):
    i = pl.program_id(1)
    q = q_ref[...]
    kc = kc_ref[...].astype(BF16)
    vc = vc_ref[...].astype(BF16)
    acc = jnp.zeros(q.shape, F32)
    for h in range(NAT_HEADS):
        msk = _head_mask(q.shape, h)
        qh = jnp.where(msk, q, jnp.zeros_like(q))
        s_ctx = _dot_t(qh, kc)
        m_ctx = jnp.max(s_ctx, axis=-1, keepdims=True)
        ms, ls, os_ = [], [], []
        for rr in range(NAT_RB):
            r = i * NAT_RB + rr
            row_start = jnp.clip(r - NAT_ROWS // 2, 0, rows - NAT_ROWS)
            off = row_start - r + NAT_ROWS - 1
            base = pl.multiple_of(row_start * GRID_W, GRID_W)
            kw = k_ref[pl.ds(base, NAT_NLOC), :]
            vw = v_ref[pl.ds(base, NAT_NLOC), :]
            bias = jnp.concatenate([bias_ref[off + 2 * jj, h] for jj in range(NAT_ROWS // 2)], axis=1)
            s = _dot_t(qh[rr * GRID_W:(rr + 1) * GRID_W], kw) + bias
            m = jnp.maximum(jnp.max(s, axis=-1, keepdims=True), m_ctx[rr * GRID_W:(rr + 1) * GRID_W])
            p = jnp.exp(s - m)
            ms.append(m)
            ls.append(jnp.sum(p, axis=-1, keepdims=True))
            os_.append(_dot(p.astype(BF16), vw))
        m = jnp.concatenate(ms, axis=0)
        p_ctx = jnp.exp(s_ctx - m)
        l = jnp.concatenate(ls, axis=0) + jnp.sum(p_ctx, axis=-1, keepdims=True)
        o = (jnp.concatenate(os_, axis=0) + _dot(p_ctx.astype(BF16), vc)) / l
        acc = acc + jnp.where(msk, o, 0.0)
    o_ref[...] = acc.astype(BF16)


def _nat_lat(nat, cache_k, cache_v, layer, bias_tiles):
    B, L, _ = nat.shape
    rows = L // GRID_W
    lc = cache_k.shape[2]
    tq = NAT_RB * GRID_W
    seq = lambda col: pl.BlockSpec((None, L, BRANCH_W), lambda b, i: (b, 0, col))
    ctx = pl.BlockSpec((None, None, lc, BRANCH_W), lambda b, i: (b, layer, 0, 0))
    return pl.pallas_call(
        functools.partial(_nat_kernel, rows=rows),
        grid=(B, rows // NAT_RB),
        in_specs=[pl.BlockSpec((None, tq, BRANCH_W), lambda b, i: (b, i, 0)), seq(1), seq(2), ctx, ctx,
                  pl.BlockSpec(bias_tiles.shape, lambda b, i: (0, 0, 0, 0))],
        out_specs=pl.BlockSpec((None, tq, BRANCH_W), lambda b, i: (b, i, 0)),
        out_shape=jax.ShapeDtypeStruct((B, L, BRANCH_W), BF16),
        compiler_params=_cparams(2),
        name="nat_lat",
    )(nat, nat, nat, cache_k, cache_v, bias_tiles)


GQA_TQ = 256
GQA_KC = 512


def _gqa_kernel(q_ref, k_ref, v_ref, kc_ref, vc_ref, o_ref):
    n_lat = k_ref.shape[0] // GQA_KC
    kc_ctx = kc_ref[...].astype(BF16)
    vc_ctx = vc_ref[...].astype(BF16)

    def update(state, qh, kc, vc):
        m, l, acc = state
        s = _dot_t(qh, kc)
        m_new = jnp.maximum(m, jnp.max(s, axis=-1, keepdims=True))
        a = jnp.exp(m - m_new)
        p = jnp.exp(s - m_new)
        return (m_new, a * l + jnp.sum(p, axis=-1, keepdims=True), a * acc + _dot(p.astype(BF16), vc))

    for side in range(2):
        q = q_ref[:, side * 128:(side + 1) * 128]
        out = jnp.zeros(q.shape, F32)
        for kv in range(GQA_KV_HEADS):
            msk = _head_mask(q.shape, kv)
            qh = jnp.where(msk, q, jnp.zeros_like(q))

            def body(c, state):
                start = pl.multiple_of(c * GQA_KC, GQA_KC)
                return update(state, qh, k_ref[pl.ds(start, GQA_KC), :], v_ref[pl.ds(start, GQA_KC), :])

            init = (jnp.full((GQA_TQ, 1), NEG_INF, F32), jnp.zeros((GQA_TQ, 1), F32), jnp.zeros(q.shape, F32))
            state = jax.lax.fori_loop(0, n_lat, body, init)
            for c in range(kc_ctx.shape[0] // GQA_KC):
                state = update(state, qh, kc_ctx[c * GQA_KC:(c + 1) * GQA_KC], vc_ctx[c * GQA_KC:(c + 1) * GQA_KC])
            _, l, acc = state
            out = out + jnp.where(msk, acc / l, 0.0)
        o_ref[:, side * 128:(side + 1) * 128] = out.astype(BF16)


def _gqa_lat(gqa, cache_k, cache_v, layer):
    B, L, _ = gqa.shape
    lc = cache_k.shape[2]
    ctx = pl.BlockSpec((None, None, lc, KV_W), lambda b, i: (b, layer, 0, 0))
    return pl.pallas_call(
        _gqa_kernel,
        grid=(B, L // GQA_TQ),
        in_specs=[pl.BlockSpec((None, GQA_TQ, BRANCH_W), lambda b, i: (b, i, 0)),
                  pl.BlockSpec((None, L, KV_W), lambda b, i: (b, 0, 2)),
                  pl.BlockSpec((None, L, KV_W), lambda b, i: (b, 0, 3)), ctx, ctx],
        out_specs=pl.BlockSpec((None, GQA_TQ, BRANCH_W), lambda b, i: (b, i, 0)),
        out_shape=jax.ShapeDtypeStruct((B, L, BRANCH_W), BF16),
        compiler_params=_cparams(2),
        name="gqa_lat",
    )(gqa, gqa, gqa, cache_k, cache_v)


def _outproj_kernel(a_ref, n_ref, c_ref, g_ref, gates_ref, x_ref, gm_ref, w_ref, b_ref, lg_ref, lb_ref, o_ref):
    out = b_ref[...]
    for i, m_ref in enumerate((a_ref, n_ref, c_ref, g_ref)):
        sl = slice(i * BRANCH_W, (i + 1) * BRANCH_W)
        mixed = (m_ref[...].astype(F32) * gates_ref[:, sl].astype(F32)).astype(BF16)
        out = out + _dot(mixed, w_ref[sl, :])
    y = DEEPNORM_ALPHA * x_ref[...] + gm_ref[...] * out
    mu = jnp.mean(y, axis=-1, keepdims=True)
    d = y - mu
    var = jnp.mean(d * d, axis=-1, keepdims=True)
    o_ref[...] = d * jax.lax.rsqrt(var + LN_EPS) * lg_ref[...] + lb_ref[...]


def _outproj(a, n, c, g, gates, x, mod, mod_row0, layer, w_out, b_out, ln_g, ln_b, tm):
    B, L, D = x.shape
    tok = lambda w: pl.BlockSpec((None, tm, w), lambda i, b: (b, i, 0))
    const = lambda shape: pl.BlockSpec(shape, lambda i, b: (0,) * len(shape))
    return pl.pallas_call(
        _outproj_kernel,
        grid=(L // tm, B),
        in_specs=[tok(BRANCH_W)] * 4 + [
            tok(D), tok(D),
            pl.BlockSpec((None, None, 1, D), lambda i, b: (layer, mod_row0 + b, 0, 2)),
            const((D, D)), const((1, D)), const((1, D)), const((1, D))],
        out_specs=tok(D),
        out_shape=jax.ShapeDtypeStruct((B, L, D), F32),
        compiler_params=_cparams(2),
        name="outproj",
    )(a, n, c, g, gates, x, mod, w_out, b_out, ln_g, ln_b)


_GQA_HEAD_ORDER = (0, 2, 1, 3)


def _gqa_perm():
    return np.concatenate([np.arange(HEAD_DIM) + HEAD_DIM * h for h in _GQA_HEAD_ORDER])


def _rope_tables(seq_len):
    half = HEAD_DIM // 2
    nf = half // 2
    t = jnp.arange(seq_len)
    inv = ROPE_THETA ** (-jnp.arange(nf, dtype=F32) * 2.0 / half)

    def tabs(pos):
        ang = pos.astype(F32)[:, None] * inv[None, :]
        return jnp.cos(ang), jnp.sin(ang)

    cr, sr = tabs(t // GRID_W)
    cc, sc = tabs(t % GRID_W)
    zero = jnp.zeros_like(sr)
    cos = jnp.concatenate([cr, cr, cc, cc], axis=1)
    sin_a = jnp.concatenate([-sr, zero, -sc, zero], axis=1)
    sin_b = jnp.concatenate([zero, sr, zero, sc], axis=1)
    return tuple(jnp.tile(a, (1, 2)) for a in (cos, sin_a, sin_b))


def _nat_bias_tiles(bias_tab):
    col = np.arange(GRID_W)
    col_start = np.clip(col - NAT_COLS // 2, 0, GRID_W - NAT_COLS)
    col_in = (col[None, :] >= col_start[:, None]) & (col[None, :] < col_start[:, None] + NAT_COLS)
    dc = np.clip(col[None, :] - col[:, None], -(NAT_COLS - 1), NAT_COLS - 1) + NAT_COLS - 1
    full = jnp.where(col_in[None, None], bias_tab[:, :, dc], NEG_INF)
    pair = jnp.concatenate([full[:, :-1], full[:, 1:]], axis=-1)
    return jnp.transpose(pair, (1, 0, 2, 3))


def kernel(x_prompt, x_sample, c, cache_nat_k, cache_nat_v, cache_gqa_k, cache_gqa_v, c_ctx, w_mod, b_mod, w_in, b_in,
           pool_w, pool_scale, nat_bias, q_norm, k_norm, conv_w, conv_b, conv_ln_g, conv_ln_b, conv_pw, w_out, b_out,
           ln_g, ln_b):
    nb, seq, D = x_prompt.shape
    db, dseq, _ = x_sample.shape
    lc = cache_nat_k.shape[2]

    cond = jnp.zeros((MOD_ROWS, D), F32).at[0].set(c_ctx).at[1:1 + db].set(c)
    mod = _modulation(cond, w_mod, b_mod).reshape(DEPTH, MOD_ROWS, 1, 3 * D)

    perm = _gqa_perm()
    cols = np.arange(IN_WIDTH)
    cols[C_GQ:C_GQ + BRANCH_W] = C_GQ + perm
    cols[C_GG:C_GG + BRANCH_W] = C_GG + perm
    w_in_b = w_in[:, :, cols].astype(BF16)
    b_in_p = b_in[:, cols].reshape(DEPTH, 1, IN_WIDTH)
    rows_out = np.arange(D)
    rows_out[3 * BRANCH_W:] = 3 * BRANCH_W + perm
    w_out_b = w_out[:, rows_out, :].astype(BF16)

    seg = jnp.asarray(np.kron(np.eye(GQA_HEADS), np.ones((HEAD_DIM, HEAD_DIM))), BF16)
    rope_tabs = _rope_tables(dseq)
    ck_n = cache_nat_k.reshape(db, DEPTH, lc, BRANCH_W)
    cv_n = cache_nat_v.reshape(db, DEPTH, lc, BRANCH_W)
    ck_g = cache_gqa_k.reshape(db, DEPTH, lc, KV_W)
    cv_g = cache_gqa_v.reshape(db, DEPTH, lc, KV_W)

    y_p = x_prompt.reshape(1, nb * seq, D)
    y_s = x_sample
    new_kv = []
    for l in range(DEPTH):
        qn = jnp.tile(q_norm[l], GQA_HEADS)[None]
        kn = jnp.tile(k_norm[l], GQA_KV_HEADS)[None]
        pool_bd = jax.scipy.linalg.block_diag(*[pool_w[l, g] for g in range(len(POOL_WINDOWS))]).astype(BF16)
        local_w = (pool_bd, pool_scale[l][None], conv_w[l], conv_b[l][None], conv_ln_g[l][None], conv_ln_b[l][None],
                   conv_pw[l].astype(BF16))
        out_w = (w_out_b[l], b_out[l][None], ln_g[l][None], ln_b[l][None])

        ax, z, gates, nat, gqa, nk, nv, gk, gv = _inproj(y_p, mod, 0, l, w_in_b[l], b_in_p[l], qn, kn, seg, None,
                                                          True, 512)
        new_kv.append((nk, nv, gk, gv))
        a_out, c_out = _local_mix(ax.reshape(nb, seq, BRANCH_W), z.reshape(nb, seq, BRANCH_W), *local_w)
        n_out, g_out = _attn_ctx(nat.reshape(nb, seq, -1), gqa.reshape(nb, seq, -1))
        flat = lambda a: a.reshape(1, nb * seq, BRANCH_W)
        y_p = _outproj(flat(a_out), flat(n_out), flat(c_out), flat(g_out), gates, y_p, mod, 0, l, *out_w, 512)

        ax, z, gates, nat, gqa = _inproj(y_s, mod, 1, l, w_in_b[l], b_in_p[l], qn, kn, seg, rope_tabs, False, 512)
        a_out, c_out = _local_mix(ax, z, *local_w)
        n_out = _nat_lat(nat, ck_n, cv_n, l, _nat_bias_tiles(nat_bias[l]))
        g_out = _gqa_lat(gqa, ck_g, cv_g, l)
        y_s = _outproj(a_out, n_out, c_out, g_out, gates, y_s, mod, 1, l, *out_w, 512)

    def stack(j, heads):
        return jnp.stack([kv[j].reshape(nb, seq, heads, HEAD_DIM) for kv in new_kv], axis=1)

    return (y_p.reshape(nb, seq, D), y_s, stack(0, NAT_HEADS), stack(1, NAT_HEADS), stack(2, GQA_KV_HEADS),
            stack(3, GQA_KV_HEADS))
```

```python
import functools

import numpy as np
import jax
import jax.numpy as jnp
from jax.experimental import pallas as pl
from jax.experimental.pallas import tpu as pltpu

F32 = jnp.float32
BF16 = jnp.bfloat16

D_MODEL = 1024
DEPTH = 4
GRID_W = 64
HEAD_DIM = 64
BRANCH_W = D_MODEL // 4
POOL_WINDOWS = (2, 4, 8, 16)
POOL_GROUP_W = BRANCH_W // len(POOL_WINDOWS)
NAT_HEADS = BRANCH_W // HEAD_DIM
NAT_ROWS = 8
NAT_COLS = 16
CONV_WIDTH = 31
GQA_HEADS = BRANCH_W // HEAD_DIM
GQA_KV_HEADS = GQA_HEADS // 2
KV_W = GQA_KV_HEADS * HEAD_DIM
ROPE_THETA = 10000.0
LN_EPS = 1e-5
RMS_EPS = 1e-6
NEG_INF = -1e30
DEEPNORM_ALPHA = (2 * DEPTH) ** 0.25
LOG2E = 1.4426950408889634
Q_SCALE = HEAD_DIM ** -0.5 * LOG2E
VT_ROWS = 80

C_AX, C_AG = 0, 256
C_NQ, C_NK, C_NV, C_NG = 512, 768, 1024, 1280
C_CV, C_CGLU, C_CG = 1536, 1792, 2048
C_GQ, C_GK, C_GV, C_GG = 2304, 2560, 2688, 2816
IN_WIDTH = 3072

HALO = 16
VMEM_LIMIT = 56 * 1024 * 1024
MOD_ROWS = 16


def _cparams(n_axes):
    return pltpu.CompilerParams(dimension_semantics=("arbitrary",) * n_axes, vmem_limit_bytes=VMEM_LIMIT)


def _silu(x):
    return x * jax.nn.sigmoid(x)


def _dot(a, b):
    return jnp.dot(a, b, preferred_element_type=F32)


def _dot_t(a, b):
    return jax.lax.dot_general(a, b, (((1,), (1,)), ((), ())), preferred_element_type=F32)


def _mod_kernel(cond_ref, w_ref, b_ref, o_ref):
    a = _silu(cond_ref[...]).astype(BF16)
    o_ref[...] = _dot(a, w_ref[...].astype(BF16)) + b_ref[...]


def _modulation(cond, w_mod, b_mod):
    nj = 3 * D_MODEL // 1024
    return pl.pallas_call(
        _mod_kernel,
        grid=(DEPTH, nj),
        in_specs=[
            pl.BlockSpec((MOD_ROWS, D_MODEL), lambda l, j: (0, 0)),
            pl.BlockSpec((None, D_MODEL, 1024), lambda l, j: (l, 0, j)),
            pl.BlockSpec((None, 1, 1024), lambda l, j: (l, 0, j)),
        ],
        out_specs=pl.BlockSpec((None, MOD_ROWS, 1024), lambda l, j: (l, 0, j)),
        out_shape=jax.ShapeDtypeStruct((DEPTH, MOD_ROWS, 3 * D_MODEL), F32),
        compiler_params=_cparams(2),
        name="modulation",
    )(cond, w_mod, b_mod.reshape(DEPTH, 1, 3 * D_MODEL))


def _head_rms(x, seg, gain):
    ss = x * x
    hi = ss.astype(BF16)
    lo = (ss - hi.astype(F32)).astype(BF16)
    tot = _dot(hi, seg) + _dot(lo, seg)
    return x * jax.lax.rsqrt(tot * (1.0 / HEAD_DIM) + RMS_EPS) * gain


def _rope(x, cos, sin_a, sin_b):
    n = x.shape[-1]
    return x * cos + pltpu.roll(x, n - 16, 1) * sin_a + pltpu.roll(x, 16, 1) * sin_b


def _inproj_kernel(*refs, latent):
    x_ref, sh_ref, sc_ref, w_ref, b_ref, qn_ref, kn_ref, seg_ref = refs[:8]
    refs = refs[8:]
    if latent:
        cos_ref, sa_ref, sb_ref = refs[:3]
        refs = refs[3:]
    ax_ref, z_ref, gates_ref, nat_ref, gqa_ref = refs[:5]
    refs = refs[5:]

    h = (x_ref[...] * (1.0 + sc_ref[...]) + sh_ref[...]).astype(BF16)

    def proj(c0, n):
        return _dot(h, w_ref[:, c0:c0 + n]) + b_ref[:, c0:c0 + n]

    ax_ref[...] = proj(C_AX, BRANCH_W)
    cv = proj(C_CV, BRANCH_W)
    z_ref[...] = cv * jax.nn.sigmoid(proj(C_CGLU, BRANCH_W))
    for i, c0 in enumerate((C_AG, C_NG, C_CG, C_GG)):
        gates_ref[:, i * BRANCH_W:(i + 1) * BRANCH_W] = _silu(proj(c0, BRANCH_W)).astype(BF16)

    nat_ref[:, 0:256] = (proj(C_NQ, BRANCH_W) * Q_SCALE).astype(BF16)
    nk = proj(C_NK, BRANCH_W)
    nv = proj(C_NV, BRANCH_W)
    nat_ref[:, 256:512] = nk.astype(BF16)
    nat_ref[:, 512:768] = nv.astype(BF16)

    gq = _head_rms(proj(C_GQ, BRANCH_W), seg_ref[...], qn_ref[...])
    gk = _head_rms(proj(C_GK, KV_W), seg_ref[0:KV_W, 0:KV_W], kn_ref[...])
    gv = proj(C_GV, KV_W)
    if not latent:
        nk_ref, nv_ref, gk_ref, gv_ref = refs
        nk_ref[...] = nk
        nv_ref[...] = nv
        gk_ref[...] = gk
        gv_ref[...] = gv
        gqa_ref[:, 256:384] = gk.astype(BF16)
        gqa_ref[:, 384:512] = gv.astype(BF16)
    else:
        cos, sa, sb = cos_ref[...], sa_ref[...], sb_ref[...]
        gk = _rope(gk, cos, sa, sb)
        gq = _rope(gq, jnp.concatenate([cos, cos], axis=1), jnp.concatenate([sa, sa], axis=1),
                   jnp.concatenate([sb, sb], axis=1))
        k_ref, vt_ref = refs
        k_ref[...] = gk.astype(BF16)
        gvt = gv.T.astype(BF16)
        ones = jnp.ones((VT_ROWS - HEAD_DIM, gvt.shape[1]), BF16)
        for kv in range(GQA_KV_HEADS):
            vt_ref[kv * VT_ROWS:kv * VT_ROWS + HEAD_DIM, :] = gvt[kv * HEAD_DIM:(kv + 1) * HEAD_DIM]
            vt_ref[kv * VT_ROWS + HEAD_DIM:(kv + 1) * VT_ROWS, :] = ones
    gqa_ref[:, 0:256] = (gq * Q_SCALE).astype(BF16)


def _inproj(x, mod, mod_row0, layer, w_in, b_in, qn, kn, seg, rope_tabs, tm):
    B, L, D = x.shape
    nt = L // tm
    latent = rope_tabs is not None
    tok = lambda w: pl.BlockSpec((None, tm, w), lambda i, b: (b, i, 0))
    const = lambda shape: pl.BlockSpec(shape, lambda i, b: (0,) * len(shape))
    in_specs = [
        tok(D),
        pl.BlockSpec((None, None, 1, D), lambda i, b: (layer, mod_row0 + b, 0, 0)),
        pl.BlockSpec((None, None, 1, D), lambda i, b: (layer, mod_row0 + b, 0, 1)),
        const((D, IN_WIDTH)), const((1, IN_WIDTH)), const((1, BRANCH_W)), const((1, KV_W)),
        const((BRANCH_W, BRANCH_W)),
    ]
    args = [x, mod, mod, w_in, b_in, qn, kn, seg]
    if latent:
        in_specs += [pl.BlockSpec((tm, KV_W), lambda i, b: (i, 0))] * 3
        args += list(rope_tabs)
    gqa_w = BRANCH_W if latent else BRANCH_W + 2 * KV_W
    out_specs = [tok(BRANCH_W), tok(BRANCH_W), tok(D), tok(3 * BRANCH_W), tok(gqa_w)]
    out_shape = [jax.ShapeDtypeStruct((B, L, BRANCH_W), F32), jax.ShapeDtypeStruct((B, L, BRANCH_W), F32),
                 jax.ShapeDtypeStruct((B, L, D), BF16), jax.ShapeDtypeStruct((B, L, 3 * BRANCH_W), BF16),
                 jax.ShapeDtypeStruct((B, L, gqa_w), BF16)]
    if latent:
        out_specs += [tok(KV_W), pl.BlockSpec((None, GQA_KV_HEADS * VT_ROWS, tm), lambda i, b: (b, 0, i))]
        out_shape += [jax.ShapeDtypeStruct((B, L, KV_W), BF16),
                      jax.ShapeDtypeStruct((B, GQA_KV_HEADS * VT_ROWS, L), BF16)]
    else:
        out_specs += [tok(BRANCH_W), tok(BRANCH_W), tok(KV_W), tok(KV_W)]
        out_shape += [jax.ShapeDtypeStruct((B, L, w), F32) for w in (BRANCH_W, BRANCH_W, KV_W, KV_W)]
    return pl.pallas_call(
        functools.partial(_inproj_kernel, latent=latent),
        grid=(nt, B),
        in_specs=in_specs,
        out_specs=out_specs,
        out_shape=out_shape,
        compiler_params=_cparams(2),
        name="inproj_lat" if latent else "inproj_ctx",
    )(*args)


LOCAL_TM = 256
LOCAL_RC = 64


def _local_kernel(ax_ref, axp_ref, axn_ref, z_ref, zp_ref, zn_ref, pw_ref, ps_ref, cw_ref, cb_ref, lg_ref, lb_ref,
                  cpw_ref, a_out_ref, c_out_ref, abuf, zbuf, ybuf, cbuf, *, seq_len):
    tm = LOCAL_TM
    i = pl.program_id(1)
    first = i == 0
    last = i == pl.num_programs(1) - 1
    zeros = jnp.zeros((HALO, BRANCH_W), F32)
    for buf, cur, prv, nxt in ((abuf, ax_ref, axp_ref, axn_ref), (zbuf, z_ref, zp_ref, zn_ref)):
        buf[0:HALO, :] = jnp.where(first, zeros, prv[...])
        buf[HALO:HALO + tm, :] = cur[...]
        buf[HALO + tm:, :] = jnp.where(last, zeros, nxt[...])

    lane = jax.lax.broadcasted_iota(jnp.int32, (LOCAL_RC, 128), 1)
    low = lane < POOL_GROUP_W
    for r0 in range(0, tm, LOCAL_RC):
        t = i * tm + r0 + jax.lax.broadcasted_iota(jnp.int32, (LOCAL_RC, 128), 0)

        def sh(off, c0):
            return abuf[HALO + r0 + off:HALO + r0 + off + LOCAL_RC, c0:c0 + 128]

        def centred(total, half, x):
            cnt = jnp.minimum(t + half, seq_len) - jnp.maximum(t - half, 0)
            return total / cnt.astype(F32) - x

        x0 = sh(0, 0)
        s2 = x0 + sh(-1, 0)
        s4 = s2 + sh(-2, 0) + sh(1, 0)
        ybuf[r0:r0 + LOCAL_RC, 0:128] = centred(jnp.where(low, s2, s4), jnp.where(low, 1, 2), x0)
        x1 = sh(0, 128)
        s8 = x1
        for off in (-4, -3, -2, -1, 1, 2, 3):
            s8 = s8 + sh(off, 128)
        s16 = s8
        for off in (-8, -7, -6, -5, 4, 5, 6, 7):
            s16 = s16 + sh(off, 128)
        ybuf[r0:r0 + LOCAL_RC, 128:256] = centred(jnp.where(low, s8, s16), jnp.where(low, 4, 8), x1)

        acc = jnp.zeros((LOCAL_RC, BRANCH_W), F32) + cb_ref[...]
        for j in range(CONV_WIDTH):
            off = j - CONV_WIDTH // 2
            acc = acc + zbuf[HALO + r0 + off:HALO + r0 + off + LOCAL_RC, :] * cw_ref[j:j + 1, :]
        cbuf[r0:r0 + LOCAL_RC, :] = acc

    a = _dot(ybuf[...].astype(BF16), pw_ref[...]) * ps_ref[...]
    a_out_ref[...] = a.astype(BF16)

    cz = cbuf[...]
    mu = jnp.mean(cz, axis=-1, keepdims=True)
    d = cz - mu
    var = jnp.mean(d * d, axis=-1, keepdims=True)
    zn = _silu(d * jax.lax.rsqrt(var + LN_EPS) * lg_ref[...] + lb_ref[...])
    c_out_ref[...] = _dot(zn.astype(BF16), cpw_ref[...]).astype(BF16)


def _local_mix(ax, z, pool_bd, pool_scale, conv_w, conv_b, ln_g, ln_b, conv_pw):
    B, L, C = ax.shape
    tm = LOCAL_TM
    nt = L // tm
    hb = tm // HALO
    nhb = L // HALO
    tok = pl.BlockSpec((None, tm, C), lambda b, i: (b, i, 0))
    prv = pl.BlockSpec((None, HALO, C), lambda b, i: (b, jnp.maximum(i * hb - 1, 0), 0))
    nxt = pl.BlockSpec((None, HALO, C), lambda b, i: (b, jnp.minimum((i + 1) * hb, nhb - 1), 0))
    const = lambda shape: pl.BlockSpec(shape, lambda b, i: (0,) * len(shape))
    return pl.pallas_call(
        functools.partial(_local_kernel, seq_len=L),
        grid=(B, nt),
        in_specs=[tok, prv, nxt, tok, prv, nxt, const((C, C)), const((1, C)), const((CONV_WIDTH, C)), const((1, C)),
                  const((1, C)), const((1, C)), const((C, C))],
        out_specs=[tok, tok],
        out_shape=[jax.ShapeDtypeStruct((B, L, C), BF16)] * 2,
        scratch_shapes=[pltpu.VMEM((tm + 2 * HALO, C), F32), pltpu.VMEM((tm + 2 * HALO, C), F32),
                        pltpu.VMEM((tm, C), F32), pltpu.VMEM((tm, C), F32)],
        compiler_params=_cparams(2),
        name="local_mix",
    )(ax, ax, ax, z, z, z, pool_bd, pool_scale, conv_w, conv_b, ln_g, ln_b, conv_pw)


def _head_mask(shape, h):
    lane = jax.lax.broadcasted_iota(jnp.int32, shape, 1)
    return (lane >= h * HEAD_DIM) & (lane < (h + 1) * HEAD_DIM)


def _softmax_pv(s, v):
    m = jnp.max(s, axis=-1, keepdims=True)
    p = jnp.exp2(s - m)
    l = jnp.sum(p, axis=-1, keepdims=True)
    return _dot(p.astype(BF16), v) / l


def _attn_ctx_kernel(nat_ref, gqa_ref, n_out_ref, g_out_ref):
    q, k, v = nat_ref[:, 0:256], nat_ref[:, 256:512], nat_ref[:, 512:768]
    acc = jnp.zeros(q.shape, F32)
    for h in range(NAT_HEADS):
        msk = _head_mask(q.shape, h)
        o = _softmax_pv(_dot_t(jnp.where(msk, q, jnp.zeros_like(q)), k), v)
        acc = acc + jnp.where(msk, o, 0.0)
    n_out_ref[...] = acc.astype(BF16)

    k, v = gqa_ref[:, 256:384], gqa_ref[:, 384:512]
    for side in range(2):
        q = gqa_ref[:, side * 128:(side + 1) * 128]
        acc = jnp.zeros(q.shape, F32)
        for kv in range(GQA_KV_HEADS):
            msk = _head_mask(q.shape, kv)
            o = _softmax_pv(_dot_t(jnp.where(msk, q, jnp.zeros_like(q)), k), v)
            acc = acc + jnp.where(msk, o, 0.0)
        g_out_ref[:, side * 128:(side + 1) * 128] = acc.astype(BF16)


def _attn_ctx(nat, gqa):
    B, L, _ = nat.shape
    return pl.pallas_call(
        _attn_ctx_kernel,
        grid=(B,),
        in_specs=[pl.BlockSpec((None, L, 3 * BRANCH_W), lambda b: (b, 0, 0)),
                  pl.BlockSpec((None, L, BRANCH_W + 2 * KV_W), lambda b: (b, 0, 0))],
        out_specs=[pl.BlockSpec((None, L, BRANCH_W), lambda b: (b, 0, 0))] * 2,
        out_shape=[jax.ShapeDtypeStruct((B, L, BRANCH_W), BF16)] * 2,
        compiler_params=_cparams(1),
        name="attn_ctx",
    )(nat, gqa)


NAT_RB = 8
NAT_NLOC = NAT_ROWS * GRID_W


def _nat_kernel(q_ref, k_ref, v_ref, kc_ref, vc_ref, bias_ref, o_ref, *, rows):
    i = pl.program_id(1)
    kc = kc_ref[...]
    vc = vc_ref[...]
    nq = NAT_HEADS * GRID_W
    diag = (jax.lax.broadcasted_iota(jnp.int32, (nq, BRANCH_W), 0) // GRID_W
            == jax.lax.broadcasted_iota(jnp.int32, (nq, BRANCH_W), 1) // HEAD_DIM)
    for rr in range(NAT_RB):
        r = i * NAT_RB + rr
        row_start = jnp.clip(r - NAT_ROWS // 2, 0, rows - NAT_ROWS)
        off = row_start - r + NAT_ROWS - 1
        base = pl.multiple_of(row_start * GRID_W, GRID_W)
        kw = k_ref[pl.ds(base, NAT_NLOC), :]
        vw = v_ref[pl.ds(base, NAT_NLOC), :]
        q_r = q_ref[rr * GRID_W:(rr + 1) * GRID_W, :]
        qs = jnp.where(diag, jnp.concatenate([q_r] * NAT_HEADS, axis=0), jnp.zeros((nq, BRANCH_W), BF16))
        bias = jnp.concatenate([bias_ref[off + 2 * jj] for jj in range(NAT_ROWS // 2)], axis=1)
        s_loc = _dot_t(qs, kw) + bias
        s_ctx = _dot_t(qs, kc)
        m = jnp.maximum(jnp.max(s_loc, axis=-1, keepdims=True), jnp.max(s_ctx, axis=-1, keepdims=True))
        p_loc = jnp.exp2(s_loc - m)
        p_ctx = jnp.exp2(s_ctx - m)
        l = jnp.sum(p_loc, axis=-1, keepdims=True) + jnp.sum(p_ctx, axis=-1, keepdims=True)
        o = (_dot(p_loc.astype(BF16), vw) + _dot(p_ctx.astype(BF16), vc)) / l
        o = jnp.where(diag, o, 0.0)
        o_r = o[0:GRID_W]
        for h in range(1, NAT_HEADS):
            o_r = o_r + o[h * GRID_W:(h + 1) * GRID_W]
        o_ref[rr * GRID_W:(rr + 1) * GRID_W, :] = o_r.astype(BF16)


def _nat_lat(nat, ctx_k, ctx_v, layer, bias_tiles):
    B, L, _ = nat.shape
    rows = L // GRID_W
    lc = ctx_k.shape[2]
    tq = NAT_RB * GRID_W
    seq = lambda col: pl.BlockSpec((None, L, BRANCH_W), lambda b, i: (b, 0, col))
    ctx = pl.BlockSpec((None, None, lc, BRANCH_W), lambda b, i: (b, layer, 0, 0))
    return pl.pallas_call(
        functools.partial(_nat_kernel, rows=rows),
        grid=(B, rows // NAT_RB),
        in_specs=[pl.BlockSpec((None, tq, BRANCH_W), lambda b, i: (b, i, 0)), seq(1), seq(2), ctx, ctx,
                  pl.BlockSpec(bias_tiles.shape, lambda b, i: (0, 0, 0))],
        out_specs=pl.BlockSpec((None, tq, BRANCH_W), lambda b, i: (b, i, 0)),
        out_shape=jax.ShapeDtypeStruct((B, L, BRANCH_W), BF16),
        compiler_params=_cparams(2),
        name="nat_lat",
    )(nat, nat, nat, ctx_k, ctx_v, bias_tiles)


GQA_TQ = 256
GQA_KC = 512


def _gqa_kernel(q_ref, k_ref, vt_ref, o_ref, s_even, s_odd):
    n = k_ref.shape[0] // GQA_KC
    heads = [(side, kv) for side in range(2) for kv in range(GQA_KV_HEADS)]
    qh = []
    for side, kv in heads:
        q = q_ref[:, side * 128:(side + 1) * 128]
        qh.append(jnp.where(_head_mask(q.shape, kv), q, jnp.zeros_like(q)))

    def keys(c):
        return k_ref[pl.ds(pl.multiple_of(c * GQA_KC, GQA_KC), GQA_KC), :]

    def step(c_next, s_next, c, s_cur, state):
        kc = None if c_next is None else keys(c_next)
        vt = vt_ref[:, pl.ds(pl.multiple_of(c * GQA_KC, GQA_KC), GQA_KC)]
        new = []
        for h, ((side, kv), (m, acc)) in enumerate(zip(heads, state)):
            if kc is not None:
                s_next[h] = _dot_t(kc, qh[h])
            s = s_cur[h]
            m_new = jnp.maximum(m, jnp.max(s, axis=0, keepdims=True))
            p = jnp.exp2(s - m_new).astype(BF16)
            acc = jnp.exp2(m - m_new) * acc + _dot(vt[kv * VT_ROWS:(kv + 1) * VT_ROWS], p)
            new.append((m_new, acc))
        return tuple(new)

    kc0 = keys(0)
    for h in range(len(heads)):
        s_even[h] = _dot_t(kc0, qh[h])

    def body(j, state):
        state = step(2 * j + 1, s_odd, 2 * j, s_even, state)
        return step(2 * j + 2, s_even, 2 * j + 1, s_odd, state)

    init = tuple((jnp.full((1, GQA_TQ), NEG_INF, F32), jnp.zeros((VT_ROWS, GQA_TQ), F32)) for _ in heads)
    pairs = (n - 1) // 2
    state = jax.lax.fori_loop(0, pairs, body, init)
    if 2 * pairs + 1 < n:
        state = step(2 * pairs + 1, s_odd, 2 * pairs, s_even, state)
        state = step(None, None, 2 * pairs + 1, s_odd, state)
    else:
        state = step(None, None, 2 * pairs, s_even, state)
    outs = [acc[0:HEAD_DIM] / acc[HEAD_DIM:HEAD_DIM + 1] for _, acc in state]
    for side in range(2):
        o_t = jnp.concatenate(outs[side * GQA_KV_HEADS:(side + 1) * GQA_KV_HEADS], axis=0)
        o_ref[:, side * 128:(side + 1) * 128] = o_t.T.astype(BF16)


def _gqa_lat(q, k_all, vt_all):
    B, L, _ = q.shape
    lk = k_all.shape[1]
    nvt = GQA_KV_HEADS * VT_ROWS
    return pl.pallas_call(
        _gqa_kernel,
        grid=(B, L // GQA_TQ),
        in_specs=[pl.BlockSpec((None, GQA_TQ, BRANCH_W), lambda b, i: (b, i, 0)),
                  pl.BlockSpec((None, lk, KV_W), lambda b, i: (b, 0, 0)),
                  pl.BlockSpec((None, nvt, lk), lambda b, i: (b, 0, 0))],
        out_specs=pl.BlockSpec((None, GQA_TQ, BRANCH_W), lambda b, i: (b, i, 0)),
        out_shape=jax.ShapeDtypeStruct((B, L, BRANCH_W), BF16),
        scratch_shapes=[pltpu.VMEM((len(_GQA_HEAD_ORDER), GQA_KC, GQA_TQ), F32)] * 2,
        compiler_params=_cparams(2),
        name="gqa_lat",
    )(q, k_all, vt_all)


def _ctx_prep_kernel(nk_ref, nv_ref, gk_ref, gv_ref, nk_o, nv_o, gk_o, gvt_o):
    nk_o[...] = nk_ref[...].astype(BF16)
    nv_o[...] = nv_ref[...].astype(BF16)
    gk_o[...] = gk_ref[...].astype(BF16)
    gvt = gv_ref[...].T.astype(BF16)
    ones = jnp.ones((VT_ROWS - HEAD_DIM, gvt.shape[1]), BF16)
    for kv in range(GQA_KV_HEADS):
        gvt_o[kv * VT_ROWS:kv * VT_ROWS + HEAD_DIM, :] = gvt[kv * HEAD_DIM:(kv + 1) * HEAD_DIM]
        gvt_o[kv * VT_ROWS + HEAD_DIM:(kv + 1) * VT_ROWS, :] = ones


def _ctx_prep(ck_n, cv_n, ck_g, cv_g):
    B, depth, lc, _ = ck_n.shape
    nvt = GQA_KV_HEADS * VT_ROWS
    blk = lambda r, c: pl.BlockSpec((None, None, r, c), lambda b, l: (b, l, 0, 0))
    return pl.pallas_call(
        _ctx_prep_kernel,
        grid=(B, depth),
        in_specs=[blk(lc, BRANCH_W), blk(lc, BRANCH_W), blk(lc, KV_W), blk(lc, KV_W)],
        out_specs=[blk(lc, BRANCH_W), blk(lc, BRANCH_W), blk(lc, KV_W), blk(nvt, lc)],
        out_shape=[jax.ShapeDtypeStruct((B, depth, lc, BRANCH_W), BF16)] * 2
        + [jax.ShapeDtypeStruct((B, depth, lc, KV_W), BF16), jax.ShapeDtypeStruct((B, depth, nvt, lc), BF16)],
        compiler_params=_cparams(2),
        name="ctx_prep",
    )(ck_n, cv_n, ck_g, cv_g)


def _outproj_kernel(a_ref, n_ref, c_ref, g_ref, gates_ref, x_ref, gm_ref, w_ref, b_ref, lg_ref, lb_ref, o_ref):
    out = b_ref[...]
    for i, m_ref in enumerate((a_ref, n_ref, c_ref, g_ref)):
        sl = slice(i * BRANCH_W, (i + 1) * BRANCH_W)
        mixed = (m_ref[...].astype(F32) * gates_ref[:, sl].astype(F32)).astype(BF16)
        out = out + _dot(mixed, w_ref[sl, :])
    y = DEEPNORM_ALPHA * x_ref[...] + gm_ref[...] * out
    mu = jnp.mean(y, axis=-1, keepdims=True)
    d = y - mu
    var = jnp.mean(d * d, axis=-1, keepdims=True)
    o_ref[...] = d * jax.lax.rsqrt(var + LN_EPS) * lg_ref[...] + lb_ref[...]


def _outproj(a, n, c, g, gates, x, mod, mod_row0, layer, w_out, b_out, ln_g, ln_b, tm):
    B, L, D = x.shape
    tok = lambda w: pl.BlockSpec((None, tm, w), lambda i, b: (b, i, 0))
    const = lambda shape: pl.BlockSpec(shape, lambda i, b: (0,) * len(shape))
    return pl.pallas_call(
        _outproj_kernel,
        grid=(L // tm, B),
        in_specs=[tok(BRANCH_W)] * 4 + [
            tok(D), tok(D),
            pl.BlockSpec((None, None, 1, D), lambda i, b: (layer, mod_row0 + b, 0, 2)),
            const((D, D)), const((1, D)), const((1, D)), const((1, D))],
        out_specs=tok(D),
        out_shape=jax.ShapeDtypeStruct((B, L, D), F32),
        compiler_params=_cparams(2),
        name="outproj",
    )(a, n, c, g, gates, x, mod, w_out, b_out, ln_g, ln_b)


_GQA_HEAD_ORDER = (0, 2, 1, 3)


def _gqa_perm():
    return np.concatenate([np.arange(HEAD_DIM) + HEAD_DIM * h for h in _GQA_HEAD_ORDER])


def _rope_tables(seq_len):
    half = HEAD_DIM // 2
    nf = half // 2
    t = jnp.arange(seq_len)
    inv = ROPE_THETA ** (-jnp.arange(nf, dtype=F32) * 2.0 / half)

    def tabs(pos):
        ang = pos.astype(F32)[:, None] * inv[None, :]
        return jnp.cos(ang), jnp.sin(ang)

    cr, sr = tabs(t // GRID_W)
    cc, sc = tabs(t % GRID_W)
    zero = jnp.zeros_like(sr)
    cos = jnp.concatenate([cr, cr, cc, cc], axis=1)
    sin_a = jnp.concatenate([-sr, zero, -sc, zero], axis=1)
    sin_b = jnp.concatenate([zero, sr, zero, sc], axis=1)
    return tuple(jnp.tile(a, (1, 2)) for a in (cos, sin_a, sin_b))


def _nat_bias_tiles(bias_tab):
    col = np.arange(GRID_W)
    col_start = np.clip(col - NAT_COLS // 2, 0, GRID_W - NAT_COLS)
    col_in = (col[None, :] >= col_start[:, None]) & (col[None, :] < col_start[:, None] + NAT_COLS)
    dc = np.clip(col[None, :] - col[:, None], -(NAT_COLS - 1), NAT_COLS - 1) + NAT_COLS - 1
    full = jnp.where(col_in[None, None], bias_tab[:, :, dc] * LOG2E, NEG_INF)
    pair = jnp.concatenate([full[:, :-1], full[:, 1:]], axis=-1)
    return jnp.transpose(pair, (1, 0, 2, 3)).reshape(2 * NAT_ROWS - 2, NAT_HEADS * GRID_W, 2 * GRID_W)


def kernel(x_prompt, x_sample, c, cache_nat_k, cache_nat_v, cache_gqa_k, cache_gqa_v, c_ctx, w_mod, b_mod, w_in, b_in,
           pool_w, pool_scale, nat_bias, q_norm, k_norm, conv_w, conv_b, conv_ln_g, conv_ln_b, conv_pw, w_out, b_out,
           ln_g, ln_b):
    nb, seq, D = x_prompt.shape
    db, dseq, _ = x_sample.shape
    lc = cache_nat_k.shape[2]

    cond = jnp.zeros((MOD_ROWS, D), F32).at[0].set(c_ctx).at[1:1 + db].set(c)
    mod = _modulation(cond, w_mod, b_mod).reshape(DEPTH, MOD_ROWS, 1, 3 * D)

    perm = _gqa_perm()
    cols = np.arange(IN_WIDTH)
    cols[C_GQ:C_GQ + BRANCH_W] = C_GQ + perm
    cols[C_GG:C_GG + BRANCH_W] = C_GG + perm
    w_in_b = w_in[:, :, cols].astype(BF16)
    b_in_p = b_in[:, cols].reshape(DEPTH, 1, IN_WIDTH)
    rows_out = np.arange(D)
    rows_out[3 * BRANCH_W:] = 3 * BRANCH_W + perm
    w_out_b = w_out[:, rows_out, :].astype(BF16)

    seg = jnp.asarray(np.kron(np.eye(GQA_HEADS), np.ones((HEAD_DIM, HEAD_DIM))), BF16)
    rope_tabs = _rope_tables(dseq)
    ck_n, cv_n, ck_g, cvt_g = _ctx_prep(cache_nat_k.reshape(db, DEPTH, lc, BRANCH_W),
                                        cache_nat_v.reshape(db, DEPTH, lc, BRANCH_W),
                                        cache_gqa_k.reshape(db, DEPTH, lc, KV_W),
                                        cache_gqa_v.reshape(db, DEPTH, lc, KV_W))

    y_p = x_prompt.reshape(1, nb * seq, D)
    y_s = x_sample
    new_kv = []
    for l in range(DEPTH):
        qn = jnp.tile(q_norm[l], GQA_HEADS)[None]
        kn = jnp.tile(k_norm[l], GQA_KV_HEADS)[None]
        pool_bd = jax.scipy.linalg.block_diag(*[pool_w[l, g] for g in range(len(POOL_WINDOWS))]).astype(BF16)
        local_w = (pool_bd, pool_scale[l][None], conv_w[l], conv_b[l][None], conv_ln_g[l][None], conv_ln_b[l][None],
                   conv_pw[l].astype(BF16))
        out_w = (w_out_b[l], b_out[l][None], ln_g[l][None], ln_b[l][None])

        ax, z, gates, nat, gqa, nk, nv, gk, gv = _inproj(y_p, mod, 0, l, w_in_b[l], b_in_p[l], qn, kn, seg, None, 512)
        new_kv.append((nk, nv, gk, gv))
        a_out, c_out = _local_mix(ax.reshape(nb, seq, BRANCH_W), z.reshape(nb, seq, BRANCH_W), *local_w)
        n_out, g_out = _attn_ctx(nat.reshape(nb, seq, -1), gqa.reshape(nb, seq, -1))
        flat = lambda a: a.reshape(1, nb * seq, BRANCH_W)
        y_p = _outproj(flat(a_out), flat(n_out), flat(c_out), flat(g_out), gates, y_p, mod, 0, l, *out_w, 512)

        ax, z, gates, nat, gq, gk, gvt = _inproj(y_s, mod, 1, l, w_in_b[l], b_in_p[l], qn, kn, seg, rope_tabs, 512)
        a_out, c_out = _local_mix(ax, z, *local_w)
        n_out = _nat_lat(nat, ck_n, cv_n, l, _nat_bias_tiles(nat_bias[l]))
        g_out = _gqa_lat(gq, jnp.concatenate([gk, ck_g[:, l]], axis=1), jnp.concatenate([gvt, cvt_g[:, l]], axis=2))
        y_s = _outproj(a_out, n_out, c_out, g_out, gates, y_s, mod, 1, l, *out_w, 512)

    def stack(j, heads):
        return jnp.stack([kv[j].reshape(nb, seq, heads, HEAD_DIM) for kv in new_kv], axis=1)

    return (y_p.reshape(nb, seq, D), y_s, stack(0, NAT_HEADS), stack(1, NAT_HEADS), stack(2, GQA_KV_HEADS),
            stack(3, GQA_KV_HEADS))
```

```python
import functools

import numpy as np
import jax
import jax.numpy as jnp
from jax.experimental import pallas as pl
from jax.experimental.pallas import tpu as pltpu

F32 = jnp.float32
BF16 = jnp.bfloat16

D_MODEL = 1024
DEPTH = 4
GRID_W = 64
HEAD_DIM = 64
BRANCH_W = D_MODEL // 4
POOL_WINDOWS = (2, 4, 8, 16)
POOL_GROUP_W = BRANCH_W // len(POOL_WINDOWS)
NAT_HEADS = BRANCH_W // HEAD_DIM
NAT_ROWS = 8
NAT_COLS = 16
CONV_WIDTH = 31
GQA_HEADS = BRANCH_W // HEAD_DIM
GQA_KV_HEADS = GQA_HEADS // 2
KV_W = GQA_KV_HEADS * HEAD_DIM
ROPE_THETA = 10000.0
LN_EPS = 1e-5
RMS_EPS = 1e-6
NEG_INF = -1e30
DEEPNORM_ALPHA = (2 * DEPTH) ** 0.25
LOG2E = 1.4426950408889634
Q_SCALE = HEAD_DIM ** -0.5 * LOG2E
VT_ROWS = 80

C_AX, C_AG = 0, 256
C_NQ, C_NK, C_NV, C_NG = 512, 768, 1024, 1280
C_CV, C_CGLU, C_CG = 1536, 1792, 2048
C_GQ, C_GK, C_GV, C_GG = 2304, 2560, 2688, 2816
IN_WIDTH = 3072

HALO = 16
VMEM_LIMIT = 56 * 1024 * 1024
MOD_ROWS = 16


def _cparams(n_axes):
    return pltpu.CompilerParams(dimension_semantics=("arbitrary",) * n_axes, vmem_limit_bytes=VMEM_LIMIT)


def _silu(x):
    return x * jax.nn.sigmoid(x)


def _dot(a, b):
    return jnp.dot(a, b, preferred_element_type=F32)


def _dot_t(a, b):
    return jax.lax.dot_general(a, b, (((1,), (1,)), ((), ())), preferred_element_type=F32)


def _mod_kernel(cond_ref, w_ref, b_ref, o_ref):
    a = _silu(cond_ref[...]).astype(BF16)
    o_ref[...] = _dot(a, w_ref[...].astype(BF16)) + b_ref[...]


def _modulation(cond, w_mod, b_mod):
    nj = 3 * D_MODEL // 1024
    return pl.pallas_call(
        _mod_kernel,
        grid=(DEPTH, nj),
        in_specs=[
            pl.BlockSpec((MOD_ROWS, D_MODEL), lambda l, j: (0, 0)),
            pl.BlockSpec((None, D_MODEL, 1024), lambda l, j: (l, 0, j)),
            pl.BlockSpec((None, 1, 1024), lambda l, j: (l, 0, j)),
        ],
        out_specs=pl.BlockSpec((None, MOD_ROWS, 1024), lambda l, j: (l, 0, j)),
        out_shape=jax.ShapeDtypeStruct((DEPTH, MOD_ROWS, 3 * D_MODEL), F32),
        compiler_params=_cparams(2),
        name="modulation",
    )(cond, w_mod, b_mod.reshape(DEPTH, 1, 3 * D_MODEL))


def _head_rms(x, seg, gain):
    ss = x * x
    hi = ss.astype(BF16)
    lo = (ss - hi.astype(F32)).astype(BF16)
    tot = _dot(hi, seg) + _dot(lo, seg)
    return x * jax.lax.rsqrt(tot * (1.0 / HEAD_DIM) + RMS_EPS) * gain


def _rope(x, cos, sin_a, sin_b):
    n = x.shape[-1]
    return x * cos + pltpu.roll(x, n - 16, 1) * sin_a + pltpu.roll(x, 16, 1) * sin_b


def _inproj_kernel(*refs, latent):
    x_ref, sh_ref, sc_ref, w_ref, b_ref, qn_ref, kn_ref, seg_ref = refs[:8]
    refs = refs[8:]
    if latent:
        cos_ref, sa_ref, sb_ref = refs[:3]
        refs = refs[3:]
    ax_ref, z_ref, gates_ref, nat_ref, gqa_ref = refs[:5]
    refs = refs[5:]

    h = (x_ref[...] * (1.0 + sc_ref[...]) + sh_ref[...]).astype(BF16)

    def proj(c0, n):
        return _dot(h, w_ref[:, c0:c0 + n]) + b_ref[:, c0:c0 + n]

    ax_ref[...] = proj(C_AX, BRANCH_W)
    cv = proj(C_CV, BRANCH_W)
    z_ref[...] = cv * jax.nn.sigmoid(proj(C_CGLU, BRANCH_W))
    for i, c0 in enumerate((C_AG, C_NG, C_CG, C_GG)):
        gates_ref[:, i * BRANCH_W:(i + 1) * BRANCH_W] = _silu(proj(c0, BRANCH_W)).astype(BF16)

    nat_ref[:, 0:256] = (proj(C_NQ, BRANCH_W) * Q_SCALE).astype(BF16)
    nk = proj(C_NK, BRANCH_W)
    nv = proj(C_NV, BRANCH_W)
    nat_ref[:, 256:512] = nk.astype(BF16)
    nat_ref[:, 512:768] = nv.astype(BF16)

    gq = _head_rms(proj(C_GQ, BRANCH_W), seg_ref[...], qn_ref[...])
    gk = _head_rms(proj(C_GK, KV_W), seg_ref[0:KV_W, 0:KV_W], kn_ref[...])
    gv = proj(C_GV, KV_W)
    if not latent:
        nk_ref, nv_ref, gk_ref, gv_ref = refs
        nk_ref[...] = nk
        nv_ref[...] = nv
        gk_ref[...] = gk
        gv_ref[...] = gv
        gqa_ref[:, 256:384] = gk.astype(BF16)
        gqa_ref[:, 384:512] = gv.astype(BF16)
    else:
        cos, sa, sb = cos_ref[...], sa_ref[...], sb_ref[...]
        gk = _rope(gk, cos, sa, sb)
        gq = _rope(gq, jnp.concatenate([cos, cos], axis=1), jnp.concatenate([sa, sa], axis=1),
                   jnp.concatenate([sb, sb], axis=1))
        k_ref, vt_ref = refs
        k_ref[...] = gk.astype(BF16)
        gvt = gv.T.astype(BF16)
        ones = jnp.ones((VT_ROWS - HEAD_DIM, gvt.shape[1]), BF16)
        for kv in range(GQA_KV_HEADS):
            vt_ref[kv * VT_ROWS:kv * VT_ROWS + HEAD_DIM, :] = gvt[kv * HEAD_DIM:(kv + 1) * HEAD_DIM]
            vt_ref[kv * VT_ROWS + HEAD_DIM:(kv + 1) * VT_ROWS, :] = ones
    gqa_ref[:, 0:256] = (gq * Q_SCALE).astype(BF16)


def _inproj(x, mod, mod_row0, layer, w_in, b_in, qn, kn, seg, rope_tabs, tm):
    B, L, D = x.shape
    nt = L // tm
    latent = rope_tabs is not None
    tok = lambda w: pl.BlockSpec((None, tm, w), lambda i, b: (b, i, 0))
    const = lambda shape: pl.BlockSpec(shape, lambda i, b: (0,) * len(shape))
    in_specs = [
        tok(D),
        pl.BlockSpec((None, None, 1, D), lambda i, b: (layer, mod_row0 + b, 0, 0)),
        pl.BlockSpec((None, None, 1, D), lambda i, b: (layer, mod_row0 + b, 0, 1)),
        const((D, IN_WIDTH)), const((1, IN_WIDTH)), const((1, BRANCH_W)), const((1, KV_W)),
        const((BRANCH_W, BRANCH_W)),
    ]
    args = [x, mod, mod, w_in, b_in, qn, kn, seg]
    if latent:
        in_specs += [pl.BlockSpec((tm, KV_W), lambda i, b: (i, 0))] * 3
        args += list(rope_tabs)
    gqa_w = BRANCH_W if latent else BRANCH_W + 2 * KV_W
    out_specs = [tok(BRANCH_W), tok(BRANCH_W), tok(D), tok(3 * BRANCH_W), tok(gqa_w)]
    out_shape = [jax.ShapeDtypeStruct((B, L, BRANCH_W), F32), jax.ShapeDtypeStruct((B, L, BRANCH_W), F32),
                 jax.ShapeDtypeStruct((B, L, D), BF16), jax.ShapeDtypeStruct((B, L, 3 * BRANCH_W), BF16),
                 jax.ShapeDtypeStruct((B, L, gqa_w), BF16)]
    if latent:
        out_specs += [tok(KV_W), pl.BlockSpec((None, GQA_KV_HEADS * VT_ROWS, tm), lambda i, b: (b, 0, i))]
        out_shape += [jax.ShapeDtypeStruct((B, L, KV_W), BF16),
                      jax.ShapeDtypeStruct((B, GQA_KV_HEADS * VT_ROWS, L), BF16)]
    else:
        out_specs += [tok(BRANCH_W), tok(BRANCH_W), tok(KV_W), tok(KV_W)]
        out_shape += [jax.ShapeDtypeStruct((B, L, w), F32) for w in (BRANCH_W, BRANCH_W, KV_W, KV_W)]
    return pl.pallas_call(
        functools.partial(_inproj_kernel, latent=latent),
        grid=(nt, B),
        in_specs=in_specs,
        out_specs=out_specs,
        out_shape=out_shape,
        compiler_params=_cparams(2),
        name="inproj_lat" if latent else "inproj_ctx",
    )(*args)


LOCAL_TM = 256
LOCAL_RC = 64
LOCAL_UC = 88
assert (LOCAL_TM + 8) % LOCAL_UC == 0 and LOCAL_UC % 8 == 0


def _local_kernel(ax_ref, axp_ref, axn_ref, z_ref, zp_ref, zn_ref, pw_ref, ps_ref, cw_ref, cb_ref, lg_ref, lb_ref,
                  cpw_ref, a_out_ref, c_out_ref, abuf, zbuf, ybuf, cbuf, ubuf, *, seq_len):
    tm = LOCAL_TM
    i = pl.program_id(1)
    first = i == 0
    last = i == pl.num_programs(1) - 1
    zeros = jnp.zeros((HALO, BRANCH_W), F32)
    for buf, cur, prv, nxt in ((abuf, ax_ref, axp_ref, axn_ref), (zbuf, z_ref, zp_ref, zn_ref)):
        buf[0:HALO, :] = jnp.where(first, zeros, prv[...])
        buf[HALO:HALO + tm, :] = cur[...]
        buf[HALO + tm:, :] = jnp.where(last, zeros, nxt[...])

    half_w = CONV_WIDTH // 2
    for res in range(8):
        taps = [8 * a + res for a in range(-2, 2) if -half_w <= 8 * a + res <= half_w]
        for u0 in range(0, tm + 8, LOCAL_UC):
            part = None
            for d in taps:
                term = zbuf[HALO + u0 + d - res:HALO + u0 + d - res + LOCAL_UC, :] * cw_ref[d + half_w:d + half_w + 1, :]
                part = term if part is None else part + term
            ubuf[res, u0:u0 + LOCAL_UC, :] = part

    lane = jax.lax.broadcasted_iota(jnp.int32, (LOCAL_RC, 128), 1)
    low = lane < POOL_GROUP_W
    for r0 in range(0, tm, LOCAL_RC):
        t = i * tm + r0 + jax.lax.broadcasted_iota(jnp.int32, (LOCAL_RC, 128), 0)

        def sh(off, c0):
            return abuf[HALO + r0 + off:HALO + r0 + off + LOCAL_RC, c0:c0 + 128]

        def centred(total, half, x):
            cnt = jnp.minimum(t + half, seq_len) - jnp.maximum(t - half, 0)
            return total / cnt.astype(F32) - x

        x0 = sh(0, 0)
        s2 = x0 + sh(-1, 0)
        s4 = s2 + sh(-2, 0) + sh(1, 0)
        ybuf[r0:r0 + LOCAL_RC, 0:128] = centred(jnp.where(low, s2, s4), jnp.where(low, 1, 2), x0)
        x1 = sh(0, 128)
        s8 = x1
        for off in (-4, -3, -2, -1, 1, 2, 3):
            s8 = s8 + sh(off, 128)
        s16 = s8
        for off in (-8, -7, -6, -5, 4, 5, 6, 7):
            s16 = s16 + sh(off, 128)
        ybuf[r0:r0 + LOCAL_RC, 128:256] = centred(jnp.where(low, s8, s16), jnp.where(low, 4, 8), x1)

        acc = jnp.zeros((LOCAL_RC, BRANCH_W), F32) + cb_ref[...]
        for res in range(8):
            acc = acc + ubuf[res, r0 + res:r0 + res + LOCAL_RC, :]
        cbuf[r0:r0 + LOCAL_RC, :] = acc

    a = _dot(ybuf[...].astype(BF16), pw_ref[...]) * ps_ref[...]
    a_out_ref[...] = a.astype(BF16)

    cz = cbuf[...]
    mu = jnp.mean(cz, axis=-1, keepdims=True)
    d = cz - mu
    var = jnp.mean(d * d, axis=-1, keepdims=True)
    zn = _silu(d * jax.lax.rsqrt(var + LN_EPS) * lg_ref[...] + lb_ref[...])
    c_out_ref[...] = _dot(zn.astype(BF16), cpw_ref[...]).astype(BF16)


def _local_mix(ax, z, pool_bd, pool_scale, conv_w, conv_b, ln_g, ln_b, conv_pw):
    B, L, C = ax.shape
    tm = LOCAL_TM
    nt = L // tm
    hb = tm // HALO
    nhb = L // HALO
    tok = pl.BlockSpec((None, tm, C), lambda b, i: (b, i, 0))
    prv = pl.BlockSpec((None, HALO, C), lambda b, i: (b, jnp.maximum(i * hb - 1, 0), 0))
    nxt = pl.BlockSpec((None, HALO, C), lambda b, i: (b, jnp.minimum((i + 1) * hb, nhb - 1), 0))
    const = lambda shape: pl.BlockSpec(shape, lambda b, i: (0,) * len(shape))
    return pl.pallas_call(
        functools.partial(_local_kernel, seq_len=L),
        grid=(B, nt),
        in_specs=[tok, prv, nxt, tok, prv, nxt, const((C, C)), const((1, C)), const((CONV_WIDTH, C)), const((1, C)),
                  const((1, C)), const((1, C)), const((C, C))],
        out_specs=[tok, tok],
        out_shape=[jax.ShapeDtypeStruct((B, L, C), BF16)] * 2,
        scratch_shapes=[pltpu.VMEM((tm + 2 * HALO, C), F32), pltpu.VMEM((tm + 2 * HALO, C), F32),
                        pltpu.VMEM((tm, C), F32), pltpu.VMEM((tm, C), F32), pltpu.VMEM((8, tm + 8, C), F32)],
        compiler_params=_cparams(2),
        name="local_mix",
    )(ax, ax, ax, z, z, z, pool_bd, pool_scale, conv_w, conv_b, ln_g, ln_b, conv_pw)


def _head_mask(shape, h):
    lane = jax.lax.broadcasted_iota(jnp.int32, shape, 1)
    return (lane >= h * HEAD_DIM) & (lane < (h + 1) * HEAD_DIM)


def _softmax_pv(s, v):
    m = jnp.max(s, axis=-1, keepdims=True)
    p = jnp.exp2(s - m)
    l = jnp.sum(p, axis=-1, keepdims=True)
    return _dot(p.astype(BF16), v) / l


def _attn_ctx_kernel(nat_ref, gqa_ref, n_out_ref, g_out_ref):
    q, k, v = nat_ref[:, 0:256], nat_ref[:, 256:512], nat_ref[:, 512:768]
    acc = jnp.zeros(q.shape, F32)
    for h in range(NAT_HEADS):
        msk = _head_mask(q.shape, h)
        o = _softmax_pv(_dot_t(jnp.where(msk, q, jnp.zeros_like(q)), k), v)
        acc = acc + jnp.where(msk, o, 0.0)
    n_out_ref[...] = acc.astype(BF16)

    k, v = gqa_ref[:, 256:384], gqa_ref[:, 384:512]
    sides = []
    for side in range(2):
        q = gqa_ref[:, side * 128:(side + 1) * 128]
        acc = jnp.zeros(q.shape, F32)
        for kv in range(GQA_KV_HEADS):
            msk = _head_mask(q.shape, kv)
            o = _softmax_pv(_dot_t(jnp.where(msk, q, jnp.zeros_like(q)), k), v)
            acc = acc + jnp.where(msk, o, 0.0)
        sides.append(acc)
    low = _head_mask(sides[0].shape, 0)
    g_out_ref[:, 0:128] = jnp.where(low, sides[0], pltpu.roll(sides[1], HEAD_DIM, 1)).astype(BF16)
    g_out_ref[:, 128:256] = jnp.where(low, pltpu.roll(sides[0], HEAD_DIM, 1), sides[1]).astype(BF16)


def _attn_ctx(nat, gqa):
    B, L, _ = nat.shape
    return pl.pallas_call(
        _attn_ctx_kernel,
        grid=(B,),
        in_specs=[pl.BlockSpec((None, L, 3 * BRANCH_W), lambda b: (b, 0, 0)),
                  pl.BlockSpec((None, L, BRANCH_W + 2 * KV_W), lambda b: (b, 0, 0))],
        out_specs=[pl.BlockSpec((None, L, BRANCH_W), lambda b: (b, 0, 0))] * 2,
        out_shape=[jax.ShapeDtypeStruct((B, L, BRANCH_W), BF16)] * 2,
        compiler_params=_cparams(1),
        name="attn_ctx",
    )(nat, gqa)


NAT_RB = 8
NAT_NLOC = NAT_ROWS * GRID_W


def _nat_kernel(q_ref, k_ref, v_ref, kc_ref, vc_ref, bias_ref, o_ref, *, rows):
    i = pl.program_id(1)
    kc = kc_ref[...]
    vc = vc_ref[...]
    nq = NAT_HEADS * GRID_W
    diag = (jax.lax.broadcasted_iota(jnp.int32, (nq, BRANCH_W), 0) // GRID_W
            == jax.lax.broadcasted_iota(jnp.int32, (nq, BRANCH_W), 1) // HEAD_DIM)
    for rr in range(NAT_RB):
        r = i * NAT_RB + rr
        row_start = jnp.clip(r - NAT_ROWS // 2, 0, rows - NAT_ROWS)
        off = row_start - r + NAT_ROWS - 1
        base = pl.multiple_of(row_start * GRID_W, GRID_W)
        kw = k_ref[pl.ds(base, NAT_NLOC), :]
        vw = v_ref[pl.ds(base, NAT_NLOC), :]
        q_r = q_ref[rr * GRID_W:(rr + 1) * GRID_W, :]
        qs = jnp.where(diag, jnp.concatenate([q_r] * NAT_HEADS, axis=0), jnp.zeros((nq, BRANCH_W), BF16))
        bias = jnp.concatenate([bias_ref[off + 2 * jj] for jj in range(NAT_ROWS // 2)], axis=1)
        s_loc = _dot_t(qs, kw) + bias
        s_ctx = _dot_t(qs, kc)
        m = jnp.maximum(jnp.max(s_loc, axis=-1, keepdims=True), jnp.max(s_ctx, axis=-1, keepdims=True))
        p_loc = jnp.exp2(s_loc - m)
        p_ctx = jnp.exp2(s_ctx - m)
        l = jnp.sum(p_loc, axis=-1, keepdims=True) + jnp.sum(p_ctx, axis=-1, keepdims=True)
        o = (_dot(p_loc.astype(BF16), vw) + _dot(p_ctx.astype(BF16), vc)) / l
        o = jnp.where(diag, o, 0.0)
        o_r = o[0:GRID_W]
        for h in range(1, NAT_HEADS):
            o_r = o_r + o[h * GRID_W:(h + 1) * GRID_W]
        o_ref[rr * GRID_W:(rr + 1) * GRID_W, :] = o_r.astype(BF16)


def _nat_lat(nat, ctx_k, ctx_v, layer, bias_tiles):
    B, L, _ = nat.shape
    rows = L // GRID_W
    lc = ctx_k.shape[2]
    tq = NAT_RB * GRID_W
    seq = lambda col: pl.BlockSpec((None, L, BRANCH_W), lambda b, i: (b, 0, col))
    ctx = pl.BlockSpec((None, None, lc, BRANCH_W), lambda b, i: (b, layer, 0, 0))
    return pl.pallas_call(
        functools.partial(_nat_kernel, rows=rows),
        grid=(B, rows // NAT_RB),
        in_specs=[pl.BlockSpec((None, tq, BRANCH_W), lambda b, i: (b, i, 0)), seq(1), seq(2), ctx, ctx,
                  pl.BlockSpec(bias_tiles.shape, lambda b, i: (0, 0, 0))],
        out_specs=pl.BlockSpec((None, tq, BRANCH_W), lambda b, i: (b, i, 0)),
        out_shape=jax.ShapeDtypeStruct((B, L, BRANCH_W), BF16),
        compiler_params=_cparams(2),
        name="nat_lat",
    )(nat, nat, nat, ctx_k, ctx_v, bias_tiles)


GQA_TQ = 256
GQA_NCHUNK = 6


def _gqa_kernel(q_ref, k_ref, vt_ref, o_ref, s_a, s_b):
    kc_len = k_ref.shape[0] // GQA_NCHUNK
    nqt = q_ref.shape[0] // GQA_TQ
    heads = [(side, kv) for side in range(2) for kv in range(GQA_KV_HEADS)]
    bufs = (s_a, s_b)

    def masked_q(t):
        row = pl.multiple_of(t * GQA_TQ, GQA_TQ)
        out = []
        for side, kv in heads:
            q = q_ref[pl.ds(row, GQA_TQ), side * 128:(side + 1) * 128]
            out.append(jnp.where(_head_mask(q.shape, kv), q, jnp.zeros_like(q)))
        return out

    qh0 = masked_q(0)
    for h in range(len(heads)):
        s_a[h] = _dot_t(k_ref[0:kc_len, :], qh0[h])

    def qtile(t, carry):
        qh_cur = masked_q(t)
        qh_next = masked_q(jnp.minimum(t + 1, nqt - 1))
        state = [(jnp.full((1, GQA_TQ), NEG_INF, F32), jnp.zeros((VT_ROWS, GQA_TQ), F32)) for _ in heads]
        for c in range(GQA_NCHUNK):
            cur, nxt = bufs[c % 2], bufs[(c + 1) % 2]
            cn = (c + 1) % GQA_NCHUNK
            qn = qh_cur if c + 1 < GQA_NCHUNK else qh_next
            kn = k_ref[cn * kc_len:(cn + 1) * kc_len, :]
            vt = vt_ref[:, c * kc_len:(c + 1) * kc_len]
            for h, (side, kv) in enumerate(heads):
                nxt[h] = _dot_t(kn, qn[h])
                m, acc = state[h]
                s = cur[h]
                m_new = jnp.maximum(m, jnp.max(s, axis=0, keepdims=True))
                p = jnp.exp2(s - m_new).astype(BF16)
                acc = jnp.exp2(m - m_new) * acc + _dot(vt[kv * VT_ROWS:(kv + 1) * VT_ROWS], p)
                state[h] = (m_new, acc)
        outs = [acc[0:HEAD_DIM] / acc[HEAD_DIM:HEAD_DIM + 1] for _, acc in state]
        row = pl.multiple_of(t * GQA_TQ, GQA_TQ)
        for half in range(2):
            o_t = jnp.concatenate([outs[heads.index((side, half))] for side in range(2)], axis=0)
            o_ref[pl.ds(row, GQA_TQ), half * 128:(half + 1) * 128] = o_t.T.astype(BF16)
        return carry

    jax.lax.fori_loop(0, nqt, qtile, 0)


def _gqa_lat(q, k_all, vt_all):
    B, L, _ = q.shape
    lk = k_all.shape[1]
    nvt = GQA_KV_HEADS * VT_ROWS
    assert GQA_NCHUNK % 2 == 0 and lk % (GQA_NCHUNK * 128) == 0 and L % GQA_TQ == 0
    return pl.pallas_call(
        _gqa_kernel,
        grid=(B,),
        in_specs=[pl.BlockSpec((None, L, BRANCH_W), lambda b: (b, 0, 0)),
                  pl.BlockSpec((None, lk, KV_W), lambda b: (b, 0, 0)),
                  pl.BlockSpec((None, nvt, lk), lambda b: (b, 0, 0))],
        out_specs=pl.BlockSpec((None, L, BRANCH_W), lambda b: (b, 0, 0)),
        out_shape=jax.ShapeDtypeStruct((B, L, BRANCH_W), BF16),
        scratch_shapes=[pltpu.VMEM((GQA_HEADS, lk // GQA_NCHUNK, GQA_TQ), F32)] * 2,
        compiler_params=_cparams(1),
        name="gqa_lat",
    )(q, k_all, vt_all)


def _ctx_prep_kernel(nk_ref, nv_ref, gk_ref, gv_ref, nk_o, nv_o, gk_o, gvt_o):
    nk_o[...] = nk_ref[...].astype(BF16)
    nv_o[...] = nv_ref[...].astype(BF16)
    gk_o[...] = gk_ref[...].astype(BF16)
    gvt = gv_ref[...].T.astype(BF16)
    ones = jnp.ones((VT_ROWS - HEAD_DIM, gvt.shape[1]), BF16)
    for kv in range(GQA_KV_HEADS):
        gvt_o[kv * VT_ROWS:kv * VT_ROWS + HEAD_DIM, :] = gvt[kv * HEAD_DIM:(kv + 1) * HEAD_DIM]
        gvt_o[kv * VT_ROWS + HEAD_DIM:(kv + 1) * VT_ROWS, :] = ones


def _ctx_prep(ck_n, cv_n, ck_g, cv_g):
    B, depth, lc, _ = ck_n.shape
    nvt = GQA_KV_HEADS * VT_ROWS
    blk = lambda r, c: pl.BlockSpec((None, None, r, c), lambda b, l: (b, l, 0, 0))
    return pl.pallas_call(
        _ctx_prep_kernel,
        grid=(B, depth),
        in_specs=[blk(lc, BRANCH_W), blk(lc, BRANCH_W), blk(lc, KV_W), blk(lc, KV_W)],
        out_specs=[blk(lc, BRANCH_W), blk(lc, BRANCH_W), blk(lc, KV_W), blk(nvt, lc)],
        out_shape=[jax.ShapeDtypeStruct((B, depth, lc, BRANCH_W), BF16)] * 2
        + [jax.ShapeDtypeStruct((B, depth, lc, KV_W), BF16), jax.ShapeDtypeStruct((B, depth, nvt, lc), BF16)],
        compiler_params=_cparams(2),
        name="ctx_prep",
    )(ck_n, cv_n, ck_g, cv_g)


def _outproj_kernel(a_ref, n_ref, c_ref, g_ref, gates_ref, x_ref, gm_ref, w_ref, b_ref, lg_ref, lb_ref, o_ref):
    out = b_ref[...]
    for i, m_ref in enumerate((a_ref, n_ref, c_ref, g_ref)):
        sl = slice(i * BRANCH_W, (i + 1) * BRANCH_W)
        mixed = (m_ref[...].astype(F32) * gates_ref[:, sl].astype(F32)).astype(BF16)
        out = out + _dot(mixed, w_ref[sl, :])
    y = DEEPNORM_ALPHA * x_ref[...] + gm_ref[...] * out
    mu = jnp.mean(y, axis=-1, keepdims=True)
    d = y - mu
    var = jnp.mean(d * d, axis=-1, keepdims=True)
    o_ref[...] = d * jax.lax.rsqrt(var + LN_EPS) * lg_ref[...] + lb_ref[...]


def _outproj(a, n, c, g, gates, x, mod, mod_row0, layer, w_out, b_out, ln_g, ln_b, tm):
    B, L, D = x.shape
    tok = lambda w: pl.BlockSpec((None, tm, w), lambda i, b: (b, i, 0))
    const = lambda shape: pl.BlockSpec(shape, lambda i, b: (0,) * len(shape))
    return pl.pallas_call(
        _outproj_kernel,
        grid=(L // tm, B),
        in_specs=[tok(BRANCH_W)] * 4 + [
            tok(D), tok(D),
            pl.BlockSpec((None, None, 1, D), lambda i, b: (layer, mod_row0 + b, 0, 2)),
            const((D, D)), const((1, D)), const((1, D)), const((1, D))],
        out_specs=tok(D),
        out_shape=jax.ShapeDtypeStruct((B, L, D), F32),
        compiler_params=_cparams(2),
        name="outproj",
    )(a, n, c, g, gates, x, mod, w_out, b_out, ln_g, ln_b)


_GQA_HEAD_ORDER = (0, 2, 1, 3)


def _rope_tables(seq_len):
    half = HEAD_DIM // 2
    nf = half // 2
    t = jnp.arange(seq_len)
    inv = ROPE_THETA ** (-jnp.arange(nf, dtype=F32) * 2.0 / half)

    def tabs(pos):
        ang = pos.astype(F32)[:, None] * inv[None, :]
        return jnp.cos(ang), jnp.sin(ang)

    cr, sr = tabs(t // GRID_W)
    cc, sc = tabs(t % GRID_W)
    zero = jnp.zeros_like(sr)
    cos = jnp.concatenate([cr, cr, cc, cc], axis=1)
    sin_a = jnp.concatenate([-sr, zero, -sc, zero], axis=1)
    sin_b = jnp.concatenate([zero, sr, zero, sc], axis=1)
    return tuple(jnp.tile(a, (1, 2)) for a in (cos, sin_a, sin_b))


def _nat_bias_tiles(bias_tab):
    col = np.arange(GRID_W)
    col_start = np.clip(col - NAT_COLS // 2, 0, GRID_W - NAT_COLS)
    col_in = (col[None, :] >= col_start[:, None]) & (col[None, :] < col_start[:, None] + NAT_COLS)
    dc = np.clip(col[None, :] - col[:, None], -(NAT_COLS - 1), NAT_COLS - 1) + NAT_COLS - 1
    full = jnp.where(col_in[None, None], bias_tab[:, :, dc] * LOG2E, NEG_INF)
    pair = jnp.concatenate([full[:, :-1], full[:, 1:]], axis=-1)
    return jnp.transpose(pair, (1, 0, 2, 3)).reshape(2 * NAT_ROWS - 2, NAT_HEADS * GRID_W, 2 * GRID_W)


def kernel(x_prompt, x_sample, c, cache_nat_k, cache_nat_v, cache_gqa_k, cache_gqa_v, c_ctx, w_mod, b_mod, w_in, b_in,
           pool_w, pool_scale, nat_bias, q_norm, k_norm, conv_w, conv_b, conv_ln_g, conv_ln_b, conv_pw, w_out, b_out,
           ln_g, ln_b):
    nb, seq, D = x_prompt.shape
    db, dseq, _ = x_sample.shape
    lc = cache_nat_k.shape[2]

    cond = jnp.zeros((MOD_ROWS, D), F32).at[0].set(c_ctx).at[1:1 + db].set(c)
    mod = _modulation(cond, w_mod, b_mod).reshape(DEPTH, MOD_ROWS, 1, 3 * D)

    def reorder_q_heads(a):
        blocks = [a[..., C_GQ + h * HEAD_DIM:C_GQ + (h + 1) * HEAD_DIM] for h in _GQA_HEAD_ORDER]
        return jnp.concatenate([a[..., :C_GQ]] + blocks + [a[..., C_GQ + BRANCH_W:]], axis=-1)

    w_in_b = reorder_q_heads(w_in).astype(BF16)
    b_in_p = reorder_q_heads(b_in).reshape(DEPTH, 1, IN_WIDTH)
    w_out_b = w_out.astype(BF16)

    seg = jnp.asarray(np.kron(np.eye(GQA_HEADS), np.ones((HEAD_DIM, HEAD_DIM))), BF16)
    rope_tabs = _rope_tables(dseq)
    ck_n, cv_n, ck_g, cvt_g = _ctx_prep(cache_nat_k.reshape(db, DEPTH, lc, BRANCH_W),
                                        cache_nat_v.reshape(db, DEPTH, lc, BRANCH_W),
                                        cache_gqa_k.reshape(db, DEPTH, lc, KV_W),
                                        cache_gqa_v.reshape(db, DEPTH, lc, KV_W))

    y_p = x_prompt.reshape(1, nb * seq, D)
    y_s = x_sample
    new_kv = []
    for l in range(DEPTH):
        qn = jnp.tile(q_norm[l], GQA_HEADS)[None]
        kn = jnp.tile(k_norm[l], GQA_KV_HEADS)[None]
        pool_bd = jax.scipy.linalg.block_diag(*[pool_w[l, g] for g in range(len(POOL_WINDOWS))]).astype(BF16)
        local_w = (pool_bd, pool_scale[l][None], conv_w[l], conv_b[l][None], conv_ln_g[l][None], conv_ln_b[l][None],
                   conv_pw[l].astype(BF16))
        out_w = (w_out_b[l], b_out[l][None], ln_g[l][None], ln_b[l][None])

        ax, z, gates, nat, gqa, nk, nv, gk, gv = _inproj(y_p, mod, 0, l, w_in_b[l], b_in_p[l], qn, kn, seg, None, 512)
        new_kv.append((nk, nv, gk, gv))
        a_out, c_out = _local_mix(ax.reshape(nb, seq, BRANCH_W), z.reshape(nb, seq, BRANCH_W), *local_w)
        n_out, g_out = _attn_ctx(nat.reshape(nb, seq, -1), gqa.reshape(nb, seq, -1))
        flat = lambda a: a.reshape(1, nb * seq, BRANCH_W)
        y_p = _outproj(flat(a_out), flat(n_out), flat(c_out), flat(g_out), gates, y_p, mod, 0, l, *out_w, 512)

        ax, z, gates, nat, gq, gk, gvt = _inproj(y_s, mod, 1, l, w_in_b[l], b_in_p[l], qn, kn, seg, rope_tabs, 512)
        a_out, c_out = _local_mix(ax, z, *local_w)
        n_out = _nat_lat(nat, ck_n, cv_n, l, _nat_bias_tiles(nat_bias[l]))
        g_out = _gqa_lat(gq, jnp.concatenate([gk, ck_g[:, l]], axis=1), jnp.concatenate([gvt, cvt_g[:, l]], axis=2))
        y_s = _outproj(a_out, n_out, c_out, g_out, gates, y_s, mod, 1, l, *out_w, 512)

    def stack(j, heads):
        return jnp.stack([kv[j].reshape(nb, seq, heads, HEAD_DIM) for kv in new_kv], axis=1)

    return (y_p.reshape(nb, seq, D), y_s, stack(0, NAT_HEADS), stack(1, NAT_HEADS), stack(2, GQA_KV_HEADS),
            stack(3, GQA_KV_HEADS))
```

```python
import functools

import numpy as np
import jax
import jax.numpy as jnp
from jax.experimental import pallas as pl
from jax.experimental.pallas import tpu as pltpu

F32 = jnp.float32
BF16 = jnp.bfloat16

D_MODEL = 1024
DEPTH = 4
GRID_W = 64
HEAD_DIM = 64
BRANCH_W = D_MODEL // 4
POOL_WINDOWS = (2, 4, 8, 16)
POOL_GROUP_W = BRANCH_W // len(POOL_WINDOWS)
NAT_HEADS = BRANCH_W // HEAD_DIM
NAT_ROWS = 8
NAT_COLS = 16
CONV_WIDTH = 31
GQA_HEADS = BRANCH_W // HEAD_DIM
GQA_KV_HEADS = GQA_HEADS // 2
KV_W = GQA_KV_HEADS * HEAD_DIM
ROPE_THETA = 10000.0
LN_EPS = 1e-5
RMS_EPS = 1e-6
NEG_INF = -1e30
DEEPNORM_ALPHA = (2 * DEPTH) ** 0.25
LOG2E = 1.4426950408889634
Q_SCALE = HEAD_DIM ** -0.5 * LOG2E
VT_ROWS = 80

_IN_SEGMENTS = (
    [("gq%d" % h, 2304 + h * HEAD_DIM, HEAD_DIM) for h in (0, 2, 1, 3)]
    + [("gk", 2560, 128), ("gv", 2688, 128), ("ax", 0, 256), ("cv", 1536, 256), ("cglu", 1792, 256),
       ("ag", 256, 256), ("ng", 1280, 256), ("cg", 2048, 256), ("gg", 2816, 256),
       ("nq", 512, 256), ("nk", 768, 256), ("nv", 1024, 256)])
IN_WIDTH = sum(w for _, _, w in _IN_SEGMENTS)
_IN_OFFSET = dict(zip((n for n, _, _ in _IN_SEGMENTS), np.cumsum([0] + [w for _, _, w in _IN_SEGMENTS])[:-1].tolist()))
C_GQ, C_GK, C_GV = _IN_OFFSET["gq0"], _IN_OFFSET["gk"], _IN_OFFSET["gv"]
C_AX, C_CV, C_CGLU = _IN_OFFSET["ax"], _IN_OFFSET["cv"], _IN_OFFSET["cglu"]
C_AG, C_NG, C_CG, C_GG = _IN_OFFSET["ag"], _IN_OFFSET["ng"], _IN_OFFSET["cg"], _IN_OFFSET["gg"]
C_NQ, C_NK, C_NV = _IN_OFFSET["nq"], _IN_OFFSET["nk"], _IN_OFFSET["nv"]

HALO = 16
VMEM_LIMIT = 56 * 1024 * 1024
MOD_ROWS = 16


def _cparams(n_axes):
    return pltpu.CompilerParams(dimension_semantics=("arbitrary",) * n_axes, vmem_limit_bytes=VMEM_LIMIT)


def _layer_block(shape, layer):
    return pl.BlockSpec((None,) + tuple(shape), lambda *_: (layer,) + (0,) * len(shape))


def _silu(x):
    return x * jax.nn.sigmoid(x)


def _dot(a, b):
    return jnp.dot(a, b, preferred_element_type=F32)


def _dot_t(a, b):
    return jax.lax.dot_general(a, b, (((1,), (1,)), ((), ())), preferred_element_type=F32)


def _mod_kernel(cond_ref, w_ref, b_ref, o_ref):
    a = _silu(cond_ref[...]).astype(BF16)
    o_ref[...] = _dot(a, w_ref[...].astype(BF16)) + b_ref[...]


def _modulation(cond, w_mod, b_mod):
    nj = 3 * D_MODEL // 1024
    return pl.pallas_call(
        _mod_kernel,
        grid=(DEPTH, nj),
        in_specs=[
            pl.BlockSpec((MOD_ROWS, D_MODEL), lambda l, j: (0, 0)),
            pl.BlockSpec((None, D_MODEL, 1024), lambda l, j: (l, 0, j)),
            pl.BlockSpec((None, 1, 1024), lambda l, j: (l, 0, j)),
        ],
        out_specs=pl.BlockSpec((None, MOD_ROWS, 1024), lambda l, j: (l, 0, j)),
        out_shape=jax.ShapeDtypeStruct((DEPTH, MOD_ROWS, 3 * D_MODEL), F32),
        compiler_params=_cparams(2),
        name="modulation",
    )(cond, w_mod, b_mod.reshape(DEPTH, 1, 3 * D_MODEL))


def _head_rms(x, seg, gain):
    tot = _dot((x * x).astype(BF16), seg)
    return x * jax.lax.rsqrt(tot * (1.0 / HEAD_DIM) + RMS_EPS) * gain


def _rope(x, cos, sin_a, sin_b):
    n = x.shape[-1]
    return x * cos + pltpu.roll(x, n - 16, 1) * sin_a + pltpu.roll(x, 16, 1) * sin_b


def _inproj_kernel(*refs, latent):
    x_ref, sh_ref, sc_ref, w_ref, b_ref, qn_ref, kn_ref, seg_ref = refs[:8]
    refs = refs[8:]
    if latent:
        cos_ref, sa_ref, sb_ref = refs[:3]
        refs = refs[3:]
    ax_ref, z_ref, gates_ref, nat_ref, gqa_ref = refs[:5]
    refs = refs[5:]

    u_ref = refs[-1]
    refs = refs[:-1]
    h = (x_ref[...] * (1.0 + sc_ref[...]) + sh_ref[...]).astype(BF16)

    def proj(c0, n):
        return u_ref[:, c0:c0 + n] + b_ref[:, c0:c0 + n]

    u_ref[:, :C_AX] = _dot(h, w_ref[:, :C_AX])

    gq = _head_rms(proj(C_GQ, BRANCH_W), seg_ref[...], qn_ref[...])
    gk = _head_rms(proj(C_GK, KV_W), seg_ref[0:KV_W, 0:KV_W], kn_ref[...])
    gv = proj(C_GV, KV_W)
    if not latent:
        nk_ref, nv_ref, gk_ref, gv_ref = refs
        gk_ref[...] = gk
        gv_ref[...] = gv
        gqa_ref[:, 256:384] = gk.astype(BF16)
        gqa_ref[:, 384:512] = gv.astype(BF16)
    else:
        cos, sa, sb = cos_ref[...], sa_ref[...], sb_ref[...]
        gk = _rope(gk, cos, sa, sb)
        gq = _rope(gq, jnp.concatenate([cos, cos], axis=1), jnp.concatenate([sa, sa], axis=1),
                   jnp.concatenate([sb, sb], axis=1))
        k_ref, vt_ref = refs
        k_ref[...] = gk.astype(BF16)
        gvt = gv.T.astype(BF16)
        ones = jnp.ones((VT_ROWS - HEAD_DIM, gvt.shape[1]), BF16)
        for kv in range(GQA_KV_HEADS):
            vt_ref[kv * VT_ROWS:kv * VT_ROWS + HEAD_DIM, :] = gvt[kv * HEAD_DIM:(kv + 1) * HEAD_DIM]
            vt_ref[kv * VT_ROWS + HEAD_DIM:(kv + 1) * VT_ROWS, :] = ones
    gqa_ref[:, 0:256] = (gq * Q_SCALE).astype(BF16)

    u_ref[:, C_AX:] = _dot(h, w_ref[:, C_AX:])
    ax_ref[...] = proj(C_AX, BRANCH_W)
    cv = proj(C_CV, BRANCH_W)
    z_ref[...] = cv * jax.nn.sigmoid(proj(C_CGLU, BRANCH_W))
    for i, c0 in enumerate((C_AG, C_NG, C_CG, C_GG)):
        gates_ref[:, i * BRANCH_W:(i + 1) * BRANCH_W] = _silu(proj(c0, BRANCH_W)).astype(BF16)

    nat_ref[:, 0:256] = (proj(C_NQ, BRANCH_W) * Q_SCALE).astype(BF16)
    nk = proj(C_NK, BRANCH_W)
    nv = proj(C_NV, BRANCH_W)
    nat_ref[:, 256:512] = nk.astype(BF16)
    nat_ref[:, 512:768] = nv.astype(BF16)
    if not latent:
        nk_ref[...] = nk
        nv_ref[...] = nv


def _inproj(x, mod, mod_row0, layer, w_in, b_in, qn, kn, seg, rope_tabs, tm):
    B, L, D = x.shape
    nt = L // tm
    latent = rope_tabs is not None
    tok = lambda w: pl.BlockSpec((None, tm, w), lambda i, b: (b, i, 0))
    per_layer = lambda *shape: _layer_block(shape, layer)
    in_specs = [
        tok(D),
        pl.BlockSpec((None, None, 1, D), lambda i, b: (layer, mod_row0 + b, 0, 0)),
        pl.BlockSpec((None, None, 1, D), lambda i, b: (layer, mod_row0 + b, 0, 1)),
        per_layer(D, IN_WIDTH), per_layer(1, IN_WIDTH), per_layer(1, BRANCH_W), per_layer(1, KV_W),
        pl.BlockSpec((BRANCH_W, BRANCH_W), lambda i, b: (0, 0)),
    ]
    args = [x, mod, mod, w_in, b_in, qn, kn, seg]
    if latent:
        in_specs += [pl.BlockSpec((tm, KV_W), lambda i, b: (i, 0))] * 3
        args += list(rope_tabs)
    gqa_w = BRANCH_W if latent else BRANCH_W + 2 * KV_W
    out_specs = [tok(BRANCH_W), tok(BRANCH_W), tok(D), tok(3 * BRANCH_W), tok(gqa_w)]
    out_shape = [jax.ShapeDtypeStruct((B, L, BRANCH_W), F32), jax.ShapeDtypeStruct((B, L, BRANCH_W), F32),
                 jax.ShapeDtypeStruct((B, L, D), BF16), jax.ShapeDtypeStruct((B, L, 3 * BRANCH_W), BF16),
                 jax.ShapeDtypeStruct((B, L, gqa_w), BF16)]
    if latent:
        out_specs += [tok(KV_W), pl.BlockSpec((None, GQA_KV_HEADS * VT_ROWS, tm), lambda i, b: (b, 0, i))]
        out_shape += [jax.ShapeDtypeStruct((B, L, KV_W), BF16),
                      jax.ShapeDtypeStruct((B, GQA_KV_HEADS * VT_ROWS, L), BF16)]
    else:
        out_specs += [tok(BRANCH_W), tok(BRANCH_W), tok(KV_W), tok(KV_W)]
        out_shape += [jax.ShapeDtypeStruct((B, L, w), F32) for w in (BRANCH_W, BRANCH_W, KV_W, KV_W)]
    return pl.pallas_call(
        functools.partial(_inproj_kernel, latent=latent),
        grid=(nt, B),
        in_specs=in_specs,
        out_specs=out_specs,
        out_shape=out_shape,
        scratch_shapes=[pltpu.VMEM((tm, IN_WIDTH), F32)],
        compiler_params=_cparams(2),
        name="inproj_lat" if latent else "inproj_ctx",
    )(*args)


LOCAL_TM = 256
LOCAL_RC = 64
LOCAL_UC = 88
assert (LOCAL_TM + 8) % LOCAL_UC == 0 and LOCAL_UC % 8 == 0


def _local_kernel(ax_ref, axp_ref, axn_ref, z_ref, zp_ref, zn_ref, pw_ref, ps_ref, cw_ref, cb_ref, lg_ref, lb_ref,
                  cpw_ref, a_out_ref, c_out_ref, abuf, zbuf, ybuf, cbuf, ubuf, *, seq_len):
    tm = LOCAL_TM
    i = pl.program_id(1)
    first = i == 0
    last = i == pl.num_programs(1) - 1
    zeros = jnp.zeros((HALO, BRANCH_W), F32)
    for buf, cur, prv, nxt in ((abuf, ax_ref, axp_ref, axn_ref), (zbuf, z_ref, zp_ref, zn_ref)):
        buf[0:HALO, :] = jnp.where(first, zeros, prv[...])
        buf[HALO:HALO + tm, :] = cur[...]
        buf[HALO + tm:, :] = jnp.where(last, zeros, nxt[...])

    half_w = CONV_WIDTH // 2
    for res in range(8):
        taps = [8 * a + res for a in range(-2, 2) if -half_w <= 8 * a + res <= half_w]
        for u0 in range(0, tm + 8, LOCAL_UC):
            part = None
            for d in taps:
                term = zbuf[HALO + u0 + d - res:HALO + u0 + d - res + LOCAL_UC, :] * cw_ref[d + half_w:d + half_w + 1, :]
                part = term if part is None else part + term
            ubuf[res, u0:u0 + LOCAL_UC, :] = part

    lane = jax.lax.broadcasted_iota(jnp.int32, (LOCAL_RC, 128), 1)
    low = lane < POOL_GROUP_W
    for r0 in range(0, tm, LOCAL_RC):
        t = i * tm + r0 + jax.lax.broadcasted_iota(jnp.int32, (LOCAL_RC, 128), 0)

        def sh(off, c0):
            return abuf[HALO + r0 + off:HALO + r0 + off + LOCAL_RC, c0:c0 + 128]

        def centred(total, half, x):
            cnt = jnp.minimum(t + half, seq_len) - jnp.maximum(t - half, 0)
            return total / cnt.astype(F32) - x

        x0 = sh(0, 0)
        s2 = x0 + sh(-1, 0)
        s4 = s2 + sh(-2, 0) + sh(1, 0)
        ybuf[r0:r0 + LOCAL_RC, 0:128] = centred(jnp.where(low, s2, s4), jnp.where(low, 1, 2), x0)
        x1 = sh(0, 128)
        s8 = x1
        for off in (-4, -3, -2, -1, 1, 2, 3):
            s8 = s8 + sh(off, 128)
        s16 = s8
        for off in (-8, -7, -6, -5, 4, 5, 6, 7):
            s16 = s16 + sh(off, 128)
        ybuf[r0:r0 + LOCAL_RC, 128:256] = centred(jnp.where(low, s8, s16), jnp.where(low, 4, 8), x1)

        acc = jnp.zeros((LOCAL_RC, BRANCH_W), F32) + cb_ref[...]
        for res in range(8):
            acc = acc + ubuf[res, r0 + res:r0 + res + LOCAL_RC, :]
        cbuf[r0:r0 + LOCAL_RC, :] = acc

    a = _dot(ybuf[...].astype(BF16), pw_ref[...]) * ps_ref[...]
    a_out_ref[...] = a.astype(BF16)

    cz = cbuf[...]
    mu = jnp.mean(cz, axis=-1, keepdims=True)
    d = cz - mu
    var = jnp.mean(d * d, axis=-1, keepdims=True)
    zn = _silu(d * jax.lax.rsqrt(var + LN_EPS) * lg_ref[...] + lb_ref[...])
    c_out_ref[...] = _dot(zn.astype(BF16), cpw_ref[...]).astype(BF16)


def _local_mix(ax, z, layer, pool_bd, pool_scale, conv_w, conv_b, ln_g, ln_b, conv_pw):
    B, L, C = ax.shape
    tm = LOCAL_TM
    nt = L // tm
    hb = tm // HALO
    nhb = L // HALO
    tok = pl.BlockSpec((None, tm, C), lambda b, i: (b, i, 0))
    prv = pl.BlockSpec((None, HALO, C), lambda b, i: (b, jnp.maximum(i * hb - 1, 0), 0))
    nxt = pl.BlockSpec((None, HALO, C), lambda b, i: (b, jnp.minimum((i + 1) * hb, nhb - 1), 0))
    per_layer = lambda *shape: _layer_block(shape, layer)
    return pl.pallas_call(
        functools.partial(_local_kernel, seq_len=L),
        grid=(B, nt),
        in_specs=[tok, prv, nxt, tok, prv, nxt, per_layer(C, C), per_layer(1, C), per_layer(CONV_WIDTH, C),
                  per_layer(1, C), per_layer(1, C), per_layer(1, C), per_layer(C, C)],
        out_specs=[tok, tok],
        out_shape=[jax.ShapeDtypeStruct((B, L, C), BF16)] * 2,
        scratch_shapes=[pltpu.VMEM((tm + 2 * HALO, C), F32), pltpu.VMEM((tm + 2 * HALO, C), F32),
                        pltpu.VMEM((tm, C), F32), pltpu.VMEM((tm, C), F32), pltpu.VMEM((8, tm + 8, C), F32)],
        compiler_params=_cparams(2),
        name="local_mix",
    )(ax, ax, ax, z, z, z, pool_bd, pool_scale, conv_w, conv_b, ln_g, ln_b, conv_pw)


def _head_mask(shape, h):
    lane = jax.lax.broadcasted_iota(jnp.int32, shape, 1)
    return (lane >= h * HEAD_DIM) & (lane < (h + 1) * HEAD_DIM)


def _softmax_pv(s, v):
    m = jnp.max(s, axis=-1, keepdims=True)
    p = jnp.exp2(s - m)
    l = jnp.sum(p, axis=-1, keepdims=True)
    return _dot(p.astype(BF16), v) / l


def _attn_ctx_kernel(nat_ref, gqa_ref, n_out_ref, g_out_ref):
    q, k, v = nat_ref[:, 0:256], nat_ref[:, 256:512], nat_ref[:, 512:768]
    acc = jnp.zeros(q.shape, F32)
    for h in range(NAT_HEADS):
        msk = _head_mask(q.shape, h)
        o = _softmax_pv(_dot_t(jnp.where(msk, q, jnp.zeros_like(q)), k), v)
        acc = acc + jnp.where(msk, o, 0.0)
    n_out_ref[...] = acc.astype(BF16)

    k, v = gqa_ref[:, 256:384], gqa_ref[:, 384:512]
    sides = []
    for side in range(2):
        q = gqa_ref[:, side * 128:(side + 1) * 128]
        acc = jnp.zeros(q.shape, F32)
        for kv in range(GQA_KV_HEADS):
            msk = _head_mask(q.shape, kv)
            o = _softmax_pv(_dot_t(jnp.where(msk, q, jnp.zeros_like(q)), k), v)
            acc = acc + jnp.where(msk, o, 0.0)
        sides.append(acc)
    low = _head_mask(sides[0].shape, 0)
    g_out_ref[:, 0:128] = jnp.where(low, sides[0], pltpu.roll(sides[1], HEAD_DIM, 1)).astype(BF16)
    g_out_ref[:, 128:256] = jnp.where(low, pltpu.roll(sides[0], HEAD_DIM, 1), sides[1]).astype(BF16)


def _attn_ctx(nat, gqa):
    B, L, _ = nat.shape
    return pl.pallas_call(
        _attn_ctx_kernel,
        grid=(B,),
        in_specs=[pl.BlockSpec((None, L, 3 * BRANCH_W), lambda b: (b, 0, 0)),
                  pl.BlockSpec((None, L, BRANCH_W + 2 * KV_W), lambda b: (b, 0, 0))],
        out_specs=[pl.BlockSpec((None, L, BRANCH_W), lambda b: (b, 0, 0))] * 2,
        out_shape=[jax.ShapeDtypeStruct((B, L, BRANCH_W), BF16)] * 2,
        compiler_params=_cparams(1),
        name="attn_ctx",
    )(nat, gqa)


NAT_RB = 8
NAT_NLOC = NAT_ROWS * GRID_W


def _nat_kernel(q_ref, k_ref, v_ref, kc_ref, vc_ref, bias_ref, o_ref, *, rows):
    i = pl.program_id(1)
    kc = kc_ref[...]
    vc = vc_ref[...]
    nq = NAT_HEADS * GRID_W
    diag = (jax.lax.broadcasted_iota(jnp.int32, (nq, BRANCH_W), 0) // GRID_W
            == jax.lax.broadcasted_iota(jnp.int32, (nq, BRANCH_W), 1) // HEAD_DIM)
    for rr in range(NAT_RB):
        r = i * NAT_RB + rr
        row_start = jnp.clip(r - NAT_ROWS // 2, 0, rows - NAT_ROWS)
        off = row_start - r + NAT_ROWS - 1
        base = pl.multiple_of(row_start * GRID_W, GRID_W)
        kw = k_ref[pl.ds(base, NAT_NLOC), :]
        vw = v_ref[pl.ds(base, NAT_NLOC), :]
        q_r = q_ref[rr * GRID_W:(rr + 1) * GRID_W, :]
        qs = jnp.where(diag, jnp.concatenate([q_r] * NAT_HEADS, axis=0), jnp.zeros((nq, BRANCH_W), BF16))
        bias = jnp.concatenate([bias_ref[off + 2 * jj] for jj in range(NAT_ROWS // 2)], axis=1)
        s_loc = _dot_t(qs, kw) + bias
        s_ctx = _dot_t(qs, kc)
        m = jnp.maximum(jnp.max(s_loc, axis=-1, keepdims=True), jnp.max(s_ctx, axis=-1, keepdims=True))
        p_loc = jnp.exp2(s_loc - m)
        p_ctx = jnp.exp2(s_ctx - m)
        l = jnp.sum(p_loc, axis=-1, keepdims=True) + jnp.sum(p_ctx, axis=-1, keepdims=True)
        o = (_dot(p_loc.astype(BF16), vw) + _dot(p_ctx.astype(BF16), vc)) / l
        o = jnp.where(diag, o, 0.0)
        o_r = o[0:GRID_W]
        for h in range(1, NAT_HEADS):
            o_r = o_r + o[h * GRID_W:(h + 1) * GRID_W]
        o_ref[rr * GRID_W:(rr + 1) * GRID_W, :] = o_r.astype(BF16)


def _nat_lat(nat, ctx_k, ctx_v, layer, bias_tiles):
    B, L, _ = nat.shape
    rows = L // GRID_W
    lc = ctx_k.shape[2]
    tq = NAT_RB * GRID_W
    seq = lambda col: pl.BlockSpec((None, L, BRANCH_W), lambda b, i: (b, 0, col))
    ctx = pl.BlockSpec((None, None, lc, BRANCH_W), lambda b, i: (b, layer, 0, 0))
    return pl.pallas_call(
        functools.partial(_nat_kernel, rows=rows),
        grid=(B, rows // NAT_RB),
        in_specs=[pl.BlockSpec((None, tq, BRANCH_W), lambda b, i: (b, i, 0)), seq(1), seq(2), ctx, ctx,
                  _layer_block(bias_tiles.shape[1:], layer)],
        out_specs=pl.BlockSpec((None, tq, BRANCH_W), lambda b, i: (b, i, 0)),
        out_shape=jax.ShapeDtypeStruct((B, L, BRANCH_W), BF16),
        compiler_params=_cparams(2),
        name="nat_lat",
    )(nat, nat, nat, ctx_k, ctx_v, bias_tiles)


GQA_TQ = 256
GQA_NCHUNK = 6


def _gqa_kernel(q_ref, k_lat_ref, vt_lat_ref, k_ctx_ref, vt_ctx_ref, o_ref, k_ref, vt_ref, s_a, s_b):
    n_lat = k_lat_ref.shape[0]
    k_ref[0:n_lat, :] = k_lat_ref[...]
    k_ref[n_lat:, :] = k_ctx_ref[...]
    vt_ref[:, 0:n_lat] = vt_lat_ref[...]
    vt_ref[:, n_lat:] = vt_ctx_ref[...]
    kc_len = k_ref.shape[0] // GQA_NCHUNK
    nqt = q_ref.shape[0] // GQA_TQ
    heads = [(side, kv) for side in range(2) for kv in range(GQA_KV_HEADS)]
    bufs = (s_a, s_b)

    def masked_q(t):
        row = pl.multiple_of(t * GQA_TQ, GQA_TQ)
        out = []
        for side, kv in heads:
            q = q_ref[pl.ds(row, GQA_TQ), side * 128:(side + 1) * 128]
            out.append(jnp.where(_head_mask(q.shape, kv), q, jnp.zeros_like(q)))
        return out

    qh0 = masked_q(0)
    for h in range(len(heads)):
        s_a[h] = _dot_t(k_ref[0:kc_len, :], qh0[h])

    def qtile(t, carry):
        qh_cur = masked_q(t)
        qh_next = masked_q(jnp.minimum(t + 1, nqt - 1))
        state = [(jnp.full((1, GQA_TQ), NEG_INF, F32), jnp.zeros((VT_ROWS, GQA_TQ), F32)) for _ in heads]
        for c in range(GQA_NCHUNK):
            cur, nxt = bufs[c % 2], bufs[(c + 1) % 2]
            cn = (c + 1) % GQA_NCHUNK
            qn = qh_cur if c + 1 < GQA_NCHUNK else qh_next
            kn = k_ref[cn * kc_len:(cn + 1) * kc_len, :]
            vt = vt_ref[:, c * kc_len:(c + 1) * kc_len]
            for h, (side, kv) in enumerate(heads):
                nxt[h] = _dot_t(kn, qn[h])
                m, acc = state[h]
                s = cur[h]
                m_new = jnp.maximum(m, jnp.max(s, axis=0, keepdims=True))
                p = jnp.exp2(s - m_new).astype(BF16)
                acc = jnp.exp2(m - m_new) * acc + _dot(vt[kv * VT_ROWS:(kv + 1) * VT_ROWS], p)
                state[h] = (m_new, acc)
        outs = [acc[0:HEAD_DIM] / acc[HEAD_DIM:HEAD_DIM + 1] for _, acc in state]
        row = pl.multiple_of(t * GQA_TQ, GQA_TQ)
        for half in range(2):
            o_t = jnp.concatenate([outs[heads.index((side, half))] for side in range(2)], axis=0)
            o_ref[pl.ds(row, GQA_TQ), half * 128:(half + 1) * 128] = o_t.T.astype(BF16)
        return carry

    jax.lax.fori_loop(0, nqt, qtile, 0)


def _gqa_lat(q, k, vt, ctx_k, ctx_vt, layer):
    B, L, _ = q.shape
    lc = ctx_k.shape[2]
    lk = L + lc
    nvt = GQA_KV_HEADS * VT_ROWS
    assert GQA_NCHUNK % 2 == 0 and lk % (GQA_NCHUNK * 128) == 0 and L % GQA_TQ == 0 and L % 128 == 0
    return pl.pallas_call(
        _gqa_kernel,
        grid=(B,),
        in_specs=[pl.BlockSpec((None, L, BRANCH_W), lambda b: (b, 0, 0)),
                  pl.BlockSpec((None, L, KV_W), lambda b: (b, 0, 0)),
                  pl.BlockSpec((None, nvt, L), lambda b: (b, 0, 0)),
                  pl.BlockSpec((None, None, lc, KV_W), lambda b: (b, layer, 0, 0)),
                  pl.BlockSpec((None, None, nvt, lc), lambda b: (b, layer, 0, 0))],
        out_specs=pl.BlockSpec((None, L, BRANCH_W), lambda b: (b, 0, 0)),
        out_shape=jax.ShapeDtypeStruct((B, L, BRANCH_W), BF16),
        scratch_shapes=[pltpu.VMEM((lk, KV_W), BF16), pltpu.VMEM((nvt, lk), BF16)]
        + [pltpu.VMEM((GQA_HEADS, lk // GQA_NCHUNK, GQA_TQ), F32)] * 2,
        compiler_params=_cparams(1),
        name="gqa_lat",
    )(q, k, vt, ctx_k, ctx_vt)


def _ctx_prep_kernel(nk_ref, nv_ref, gk_ref, gv_ref, nk_o, nv_o, gk_o, gvt_o):
    nk_o[...] = nk_ref[...].astype(BF16)
    nv_o[...] = nv_ref[...].astype(BF16)
    gk_o[...] = gk_ref[...].astype(BF16)
    gvt = gv_ref[...].T.astype(BF16)
    ones = jnp.ones((VT_ROWS - HEAD_DIM, gvt.shape[1]), BF16)
    for kv in range(GQA_KV_HEADS):
        gvt_o[kv * VT_ROWS:kv * VT_ROWS + HEAD_DIM, :] = gvt[kv * HEAD_DIM:(kv + 1) * HEAD_DIM]
        gvt_o[kv * VT_ROWS + HEAD_DIM:(kv + 1) * VT_ROWS, :] = ones


def _ctx_prep(ck_n, cv_n, ck_g, cv_g):
    B, depth, lc, _ = ck_n.shape
    nvt = GQA_KV_HEADS * VT_ROWS
    blk = lambda r, c: pl.BlockSpec((None, None, r, c), lambda b, l: (b, l, 0, 0))
    return pl.pallas_call(
        _ctx_prep_kernel,
        grid=(B, depth),
        in_specs=[blk(lc, BRANCH_W), blk(lc, BRANCH_W), blk(lc, KV_W), blk(lc, KV_W)],
        out_specs=[blk(lc, BRANCH_W), blk(lc, BRANCH_W), blk(lc, KV_W), blk(nvt, lc)],
        out_shape=[jax.ShapeDtypeStruct((B, depth, lc, BRANCH_W), BF16)] * 2
        + [jax.ShapeDtypeStruct((B, depth, lc, KV_W), BF16), jax.ShapeDtypeStruct((B, depth, nvt, lc), BF16)],
        compiler_params=_cparams(2),
        name="ctx_prep",
    )(ck_n, cv_n, ck_g, cv_g)


OUT_RC = 256


def _outproj_kernel(a_ref, n_ref, c_ref, g_ref, gates_ref, x_ref, gm_ref, w_ref, b_ref, lg_ref, lb_ref, o_ref):
    for r0 in range(0, x_ref.shape[0], OUT_RC):
        rows = slice(r0, r0 + OUT_RC)
        mixed = jnp.concatenate([m_ref[rows, :] * gates_ref[rows, i * BRANCH_W:(i + 1) * BRANCH_W]
                                 for i, m_ref in enumerate((a_ref, n_ref, c_ref, g_ref))], axis=1)
        out = _dot(mixed, w_ref[...]) + b_ref[...]
        y = DEEPNORM_ALPHA * x_ref[rows, :] + gm_ref[...] * out
        mu = jnp.mean(y, axis=-1, keepdims=True)
        d = y - mu
        var = jnp.mean(d * d, axis=-1, keepdims=True)
        o_ref[rows, :] = d * jax.lax.rsqrt(var + LN_EPS) * lg_ref[...] + lb_ref[...]


def _outproj(a, n, c, g, gates, x, mod, mod_row0, layer, w_out, b_out, ln_g, ln_b, tm):
    B, L, D = x.shape
    tok = lambda w: pl.BlockSpec((None, tm, w), lambda i, b: (b, i, 0))
    per_layer = lambda *shape: _layer_block(shape, layer)
    return pl.pallas_call(
        _outproj_kernel,
        grid=(L // tm, B),
        in_specs=[tok(BRANCH_W)] * 4 + [
            tok(D), tok(D),
            pl.BlockSpec((None, None, 1, D), lambda i, b: (layer, mod_row0 + b, 0, 2)),
            per_layer(D, D), per_layer(1, D), per_layer(1, D), per_layer(1, D)],
        out_specs=tok(D),
        out_shape=jax.ShapeDtypeStruct((B, L, D), F32),
        compiler_params=_cparams(2),
        name="outproj",
    )(a, n, c, g, gates, x, mod, w_out, b_out, ln_g, ln_b)


_GQA_HEAD_ORDER = (0, 2, 1, 3)


def _rope_tables(seq_len):
    half = HEAD_DIM // 2
    nf = half // 2
    t = jnp.arange(seq_len)
    inv = ROPE_THETA ** (-jnp.arange(nf, dtype=F32) * 2.0 / half)

    def tabs(pos):
        ang = pos.astype(F32)[:, None] * inv[None, :]
        return jnp.cos(ang), jnp.sin(ang)

    cr, sr = tabs(t // GRID_W)
    cc, sc = tabs(t % GRID_W)
    zero = jnp.zeros_like(sr)
    cos = jnp.concatenate([cr, cr, cc, cc], axis=1)
    sin_a = jnp.concatenate([-sr, zero, -sc, zero], axis=1)
    sin_b = jnp.concatenate([zero, sr, zero, sc], axis=1)
    return tuple(jnp.tile(a, (1, 2)) for a in (cos, sin_a, sin_b))


def _nat_bias_tiles(bias_tab):
    col = np.arange(GRID_W)
    col_start = np.clip(col - NAT_COLS // 2, 0, GRID_W - NAT_COLS)
    col_in = (col[None, :] >= col_start[:, None]) & (col[None, :] < col_start[:, None] + NAT_COLS)
    dc = np.clip(col[None, :] - col[:, None], -(NAT_COLS - 1), NAT_COLS - 1) + NAT_COLS - 1
    full = jnp.where(col_in, bias_tab[..., dc] * LOG2E, NEG_INF)
    pair = jnp.concatenate([full[:, :, :-1], full[:, :, 1:]], axis=-1)
    return jnp.transpose(pair, (0, 2, 1, 3, 4)).reshape(-1, 2 * NAT_ROWS - 2, NAT_HEADS * GRID_W, 2 * GRID_W)


def kernel(x_prompt, x_sample, c, cache_nat_k, cache_nat_v, cache_gqa_k, cache_gqa_v, c_ctx, w_mod, b_mod, w_in, b_in,
           pool_w, pool_scale, nat_bias, q_norm, k_norm, conv_w, conv_b, conv_ln_g, conv_ln_b, conv_pw, w_out, b_out,
           ln_g, ln_b):
    nb, seq, D = x_prompt.shape
    db, dseq, _ = x_sample.shape
    lc = cache_nat_k.shape[2]

    cond = jnp.zeros((MOD_ROWS, D), F32).at[0].set(c_ctx).at[1:1 + db].set(c)
    mod = _modulation(cond, w_mod, b_mod).reshape(DEPTH, MOD_ROWS, 1, 3 * D)

    def kernel_columns(a):
        return jnp.concatenate([a[..., c0:c0 + w] for _, c0, w in _IN_SEGMENTS], axis=-1)

    w_in_b = kernel_columns(w_in).astype(BF16)
    b_in_p = kernel_columns(b_in).reshape(DEPTH, 1, IN_WIDTH)
    w_out_b = w_out.astype(BF16)

    seg = jnp.asarray(np.kron(np.eye(GQA_HEADS), np.ones((HEAD_DIM, HEAD_DIM))), BF16)
    rope_tabs = _rope_tables(dseq)
    ck_n, cv_n, ck_g, cvt_g = _ctx_prep(cache_nat_k.reshape(db, DEPTH, lc, BRANCH_W),
                                        cache_nat_v.reshape(db, DEPTH, lc, BRANCH_W),
                                        cache_gqa_k.reshape(db, DEPTH, lc, KV_W),
                                        cache_gqa_v.reshape(db, DEPTH, lc, KV_W))

    row = lambda a: a.reshape(DEPTH, 1, -1)
    n_groups = len(POOL_WINDOWS)
    pool_bd = jnp.einsum('gh,lgcd->lgchd', jnp.eye(n_groups, dtype=F32), pool_w).reshape(DEPTH, BRANCH_W, BRANCH_W)
    in_w = (w_in_b, b_in_p, row(jnp.tile(q_norm, (1, GQA_HEADS))), row(jnp.tile(k_norm, (1, GQA_KV_HEADS))), seg)
    local_w = (pool_bd.astype(BF16), row(pool_scale), conv_w, row(conv_b), row(conv_ln_g), row(conv_ln_b),
               conv_pw.astype(BF16))
    out_w = (w_out_b, row(b_out), row(ln_g), row(ln_b))
    bias_tiles = _nat_bias_tiles(nat_bias)

    y_p = x_prompt.reshape(1, nb * seq, D)
    y_s = x_sample
    new_kv = []
    for l in range(DEPTH):
        ax, z, gates, nat, gqa, nk, nv, gk, gv = _inproj(y_p, mod, 0, l, *in_w, None, 512)
        new_kv.append((nk, nv, gk, gv))
        a_out, c_out = _local_mix(ax.reshape(nb, seq, BRANCH_W), z.reshape(nb, seq, BRANCH_W), l, *local_w)
        n_out, g_out = _attn_ctx(nat.reshape(nb, seq, -1), gqa.reshape(nb, seq, -1))
        flat = lambda a: a.reshape(1, nb * seq, BRANCH_W)
        y_p = _outproj(flat(a_out), flat(n_out), flat(c_out), flat(g_out), gates, y_p, mod, 0, l, *out_w, 512)

        ax, z, gates, nat, gq, gk, gvt = _inproj(y_s, mod, 1, l, *in_w, rope_tabs, 512)
        a_out, c_out = _local_mix(ax, z, l, *local_w)
        n_out = _nat_lat(nat, ck_n, cv_n, l, bias_tiles)
        g_out = _gqa_lat(gq, gk, gvt, ck_g, cvt_g, l)
        y_s = _outproj(a_out, n_out, c_out, g_out, gates, y_s, mod, 1, l, *out_w, 512)

    def stack(j, heads):
        return jnp.stack([kv[j].reshape(nb, seq, heads, HEAD_DIM) for kv in new_kv], axis=1)

    return (y_p.reshape(nb, seq, D), y_s, stack(0, NAT_HEADS), stack(1, NAT_HEADS), stack(2, GQA_KV_HEADS),
            stack(3, GQA_KV_HEADS))
```

```python
import functools

import numpy as np
import jax
import jax.numpy as jnp
from jax.experimental import pallas as pl
from jax.experimental.pallas import tpu as pltpu

F32 = jnp.float32
BF16 = jnp.bfloat16

D_MODEL = 1024
DEPTH = 4
GRID_W = 64
HEAD_DIM = 64
BRANCH_W = D_MODEL // 4
POOL_WINDOWS = (2, 4, 8, 16)
POOL_GROUP_W = BRANCH_W // len(POOL_WINDOWS)
NAT_HEADS = BRANCH_W // HEAD_DIM
NAT_ROWS = 8
NAT_COLS = 16
CONV_WIDTH = 31
GQA_HEADS = BRANCH_W // HEAD_DIM
GQA_KV_HEADS = GQA_HEADS // 2
KV_W = GQA_KV_HEADS * HEAD_DIM
ROPE_THETA = 10000.0
LN_EPS = 1e-5
RMS_EPS = 1e-6
NEG_INF = -1e30
DEEPNORM_ALPHA = (2 * DEPTH) ** 0.25
LOG2E = 1.4426950408889634
Q_SCALE = HEAD_DIM ** -0.5 * LOG2E
VT_ROWS = 80

_IN_SEGMENTS = (
    [("gq%d" % h, 2304 + h * HEAD_DIM, HEAD_DIM) for h in (0, 2, 1, 3)]
    + [("gk", 2560, 128), ("gv", 2688, 128), ("ax", 0, 256), ("cv", 1536, 256), ("cglu", 1792, 256),
       ("ag", 256, 256), ("ng", 1280, 256), ("cg", 2048, 256), ("gg", 2816, 256),
       ("nq", 512, 256), ("nk", 768, 256), ("nv", 1024, 256)])
IN_WIDTH = sum(w for _, _, w in _IN_SEGMENTS)
_IN_OFFSET = dict(zip((n for n, _, _ in _IN_SEGMENTS), np.cumsum([0] + [w for _, _, w in _IN_SEGMENTS])[:-1].tolist()))
C_GQ, C_GK, C_GV = _IN_OFFSET["gq0"], _IN_OFFSET["gk"], _IN_OFFSET["gv"]
C_AX, C_CV, C_CGLU = _IN_OFFSET["ax"], _IN_OFFSET["cv"], _IN_OFFSET["cglu"]
C_AG, C_NG, C_CG, C_GG = _IN_OFFSET["ag"], _IN_OFFSET["ng"], _IN_OFFSET["cg"], _IN_OFFSET["gg"]
C_NQ, C_NK, C_NV = _IN_OFFSET["nq"], _IN_OFFSET["nk"], _IN_OFFSET["nv"]

HALO = 16
VMEM_LIMIT = 56 * 1024 * 1024
MOD_ROWS = 16


def _cparams(n_axes):
    return pltpu.CompilerParams(dimension_semantics=("arbitrary",) * n_axes, vmem_limit_bytes=VMEM_LIMIT)


def _layer_block(shape, layer):
    return pl.BlockSpec((None,) + tuple(shape), lambda *_: (layer,) + (0,) * len(shape))


def _silu(x):
    return x * jax.nn.sigmoid(x)


def _dot(a, b):
    return jnp.dot(a, b, preferred_element_type=F32)


def _dot_t(a, b):
    return jax.lax.dot_general(a, b, (((1,), (1,)), ((), ())), preferred_element_type=F32)


def _mod_kernel(cond_ref, w_ref, b_ref, o_ref):
    a = _silu(cond_ref[...]).astype(BF16)
    o_ref[...] = _dot(a, w_ref[...].astype(BF16)) + b_ref[...]


def _modulation(cond, w_mod, b_mod):
    nj = 3 * D_MODEL // 1024
    return pl.pallas_call(
        _mod_kernel,
        grid=(DEPTH, nj),
        in_specs=[
            pl.BlockSpec((MOD_ROWS, D_MODEL), lambda l, j: (0, 0)),
            pl.BlockSpec((None, D_MODEL, 1024), lambda l, j: (l, 0, j)),
            pl.BlockSpec((None, 1, 1024), lambda l, j: (l, 0, j)),
        ],
        out_specs=pl.BlockSpec((None, MOD_ROWS, 1024), lambda l, j: (l, 0, j)),
        out_shape=jax.ShapeDtypeStruct((DEPTH, MOD_ROWS, 3 * D_MODEL), F32),
        compiler_params=_cparams(2),
        name="modulation",
    )(cond, w_mod, b_mod.reshape(DEPTH, 1, 3 * D_MODEL))


def _head_rms(x, seg, gain):
    tot = _dot((x * x).astype(BF16), seg)
    return x * jax.lax.rsqrt(tot * (1.0 / HEAD_DIM) + RMS_EPS) * gain


def _rope(x, cos, sin_a, sin_b):
    n = x.shape[-1]
    return x * cos + pltpu.roll(x, n - 16, 1) * sin_a + pltpu.roll(x, 16, 1) * sin_b


def _inproj_kernel(*refs, latent):
    x_ref, sh_ref, sc_ref, w_ref, b_ref, qn_ref, kn_ref, seg_ref = refs[:8]
    refs = refs[8:]
    if latent:
        cos_ref, sa_ref, sb_ref = refs[:3]
        refs = refs[3:]
    else:
        refs = refs[4:]
    ax_ref, z_ref, gates_ref, nat_ref, gqa_ref = refs[:5]
    refs = refs[5:]

    def put_cache(ref, val):
        seq = ref.shape[1]
        for sidx in range(ref.shape[0]):
            ref[sidx] = val[sidx * seq:(sidx + 1) * seq]

    u_ref = refs[-1]
    refs = refs[:-1]
    h = (x_ref[...] * (1.0 + sc_ref[...]) + sh_ref[...]).astype(BF16)

    def proj(c0, n):
        return u_ref[:, c0:c0 + n] + b_ref[:, c0:c0 + n]

    u_ref[:, :C_AX] = _dot(h, w_ref[:, :C_AX])

    gq = _head_rms(proj(C_GQ, BRANCH_W), seg_ref[...], qn_ref[...])
    gk = _head_rms(proj(C_GK, KV_W), seg_ref[0:KV_W, 0:KV_W], kn_ref[...])
    gv = proj(C_GV, KV_W)
    if not latent:
        nk_ref, nv_ref, gk_ref, gv_ref = refs
        put_cache(gk_ref, gk)
        put_cache(gv_ref, gv)
        gqa_ref[:, 256:384] = gk.astype(BF16)
        gqa_ref[:, 384:512] = gv.astype(BF16)
    else:
        cos, sa, sb = cos_ref[...], sa_ref[...], sb_ref[...]
        gk = _rope(gk, cos, sa, sb)
        gq = _rope(gq, jnp.concatenate([cos, cos], axis=1), jnp.concatenate([sa, sa], axis=1),
                   jnp.concatenate([sb, sb], axis=1))
        k_ref, vt_ref = refs
        k_ref[...] = gk.astype(BF16)
        gvt = gv.T.astype(BF16)
        ones = jnp.ones((VT_ROWS - HEAD_DIM, gvt.shape[1]), BF16)
        for kv in range(GQA_KV_HEADS):
            vt_ref[kv * VT_ROWS:kv * VT_ROWS + HEAD_DIM, :] = gvt[kv * HEAD_DIM:(kv + 1) * HEAD_DIM]
            vt_ref[kv * VT_ROWS + HEAD_DIM:(kv + 1) * VT_ROWS, :] = ones
    gqa_ref[:, 0:256] = (gq * Q_SCALE).astype(BF16)

    u_ref[:, C_AX:] = _dot(h, w_ref[:, C_AX:])
    ax_ref[...] = proj(C_AX, BRANCH_W)
    cv = proj(C_CV, BRANCH_W)
    z_ref[...] = cv * jax.nn.sigmoid(proj(C_CGLU, BRANCH_W))
    for i, c0 in enumerate((C_AG, C_NG, C_CG, C_GG)):
        gates_ref[:, i * BRANCH_W:(i + 1) * BRANCH_W] = _silu(proj(c0, BRANCH_W)).astype(BF16)

    nat_ref[:, 0:256] = (proj(C_NQ, BRANCH_W) * Q_SCALE).astype(BF16)
    nk = proj(C_NK, BRANCH_W)
    nv = proj(C_NV, BRANCH_W)
    nat_ref[:, 256:512] = nk.astype(BF16)
    nat_ref[:, 512:768] = nv.astype(BF16)
    if not latent:
        put_cache(nk_ref, nk)
        put_cache(nv_ref, nv)


def _inproj(x, mod, mod_row0, layer, w_in, b_in, qn, kn, seg, rope_tabs, new_cache, tm):
    B, L, D = x.shape
    nt = L // tm
    latent = rope_tabs is not None
    tok = lambda w: pl.BlockSpec((None, tm, w), lambda i, b: (b, i, 0))
    per_layer = lambda *shape: _layer_block(shape, layer)
    in_specs = [
        tok(D),
        pl.BlockSpec((None, None, 1, D), lambda i, b: (layer, mod_row0 + b, 0, 0)),
        pl.BlockSpec((None, None, 1, D), lambda i, b: (layer, mod_row0 + b, 0, 1)),
        per_layer(D, IN_WIDTH), per_layer(1, IN_WIDTH), per_layer(1, BRANCH_W), per_layer(1, KV_W),
        pl.BlockSpec((BRANCH_W, BRANCH_W), lambda i, b: (0, 0)),
    ]
    args = [x, mod, mod, w_in, b_in, qn, kn, seg]
    if latent:
        in_specs += [pl.BlockSpec((tm, KV_W), lambda i, b: (i, 0))] * 3
        args += list(rope_tabs)
    gqa_w = BRANCH_W if latent else BRANCH_W + 2 * KV_W
    out_specs = [tok(BRANCH_W), tok(BRANCH_W), tok(D), tok(3 * BRANCH_W), tok(gqa_w)]
    out_shape = [jax.ShapeDtypeStruct((B, L, BRANCH_W), F32), jax.ShapeDtypeStruct((B, L, BRANCH_W), F32),
                 jax.ShapeDtypeStruct((B, L, D), BF16), jax.ShapeDtypeStruct((B, L, 3 * BRANCH_W), BF16),
                 jax.ShapeDtypeStruct((B, L, gqa_w), BF16)]
    if latent:
        out_specs += [tok(KV_W), pl.BlockSpec((None, GQA_KV_HEADS * VT_ROWS, tm), lambda i, b: (b, 0, i))]
        out_shape += [jax.ShapeDtypeStruct((B, L, KV_W), BF16),
                      jax.ShapeDtypeStruct((B, GQA_KV_HEADS * VT_ROWS, L), BF16)]
        aliases = {}
    else:
        seq = new_cache[0].shape[2]
        assert B == 1 and tm % seq == 0
        aliases = {len(args) + j: len(out_specs) + j for j in range(len(new_cache))}
        in_specs += [pl.BlockSpec(memory_space=pl.ANY)] * len(new_cache)
        args += list(new_cache)
        out_specs += [pl.BlockSpec((tm // seq, None, seq, c.shape[3]), lambda i, b: (i, layer, 0, 0)) for c in new_cache]
        out_shape += [jax.ShapeDtypeStruct(c.shape, c.dtype) for c in new_cache]
    return pl.pallas_call(
        functools.partial(_inproj_kernel, latent=latent),
        grid=(nt, B),
        in_specs=in_specs,
        out_specs=out_specs,
        out_shape=out_shape,
        input_output_aliases=aliases,
        scratch_shapes=[pltpu.VMEM((tm, IN_WIDTH), F32)],
        compiler_params=_cparams(2),
        name="inproj_lat" if latent else "inproj_ctx",
    )(*args)


LOCAL_TM = 256
LOCAL_RC = 64
LOCAL_UC = 88
assert (LOCAL_TM + 8) % LOCAL_UC == 0 and LOCAL_UC % 8 == 0


def _local_kernel(ax_ref, axp_ref, axn_ref, z_ref, zp_ref, zn_ref, pw_ref, ps_ref, cw_ref, cb_ref, lg_ref, lb_ref,
                  cpw_ref, a_out_ref, c_out_ref, abuf, zbuf, ybuf, cbuf, ubuf, *, seq_len):
    tm = LOCAL_TM
    i = pl.program_id(1)
    first = i == 0
    last = i == pl.num_programs(1) - 1
    zeros = jnp.zeros((HALO, BRANCH_W), F32)
    for buf, cur, prv, nxt in ((abuf, ax_ref, axp_ref, axn_ref), (zbuf, z_ref, zp_ref, zn_ref)):
        buf[0:HALO, :] = jnp.where(first, zeros, prv[...])
        buf[HALO:HALO + tm, :] = cur[...]
        buf[HALO + tm:, :] = jnp.where(last, zeros, nxt[...])

    half_w = CONV_WIDTH // 2
    for res in range(8):
        taps = [8 * a + res for a in range(-2, 2) if -half_w <= 8 * a + res <= half_w]
        for u0 in range(0, tm + 8, LOCAL_UC):
            part = None
            for d in taps:
                term = zbuf[HALO + u0 + d - res:HALO + u0 + d - res + LOCAL_UC, :] * cw_ref[d + half_w:d + half_w + 1, :]
                part = term if part is None else part + term
            ubuf[res, u0:u0 + LOCAL_UC, :] = part

    lane = jax.lax.broadcasted_iota(jnp.int32, (LOCAL_RC, 128), 1)
    low = lane < POOL_GROUP_W
    for r0 in range(0, tm, LOCAL_RC):
        t = i * tm + r0 + jax.lax.broadcasted_iota(jnp.int32, (LOCAL_RC, 128), 0)

        def sh(off, c0):
            return abuf[HALO + r0 + off:HALO + r0 + off + LOCAL_RC, c0:c0 + 128]

        def centred(total, half, x):
            cnt = jnp.minimum(t + half, seq_len) - jnp.maximum(t - half, 0)
            return total / cnt.astype(F32) - x

        x0 = sh(0, 0)
        s2 = x0 + sh(-1, 0)
        s4 = s2 + sh(-2, 0) + sh(1, 0)
        ybuf[r0:r0 + LOCAL_RC, 0:128] = centred(jnp.where(low, s2, s4), jnp.where(low, 1, 2), x0)
        x1 = sh(0, 128)
        s8 = x1
        for off in (-4, -3, -2, -1, 1, 2, 3):
            s8 = s8 + sh(off, 128)
        s16 = s8
        for off in (-8, -7, -6, -5, 4, 5, 6, 7):
            s16 = s16 + sh(off, 128)
        ybuf[r0:r0 + LOCAL_RC, 128:256] = centred(jnp.where(low, s8, s16), jnp.where(low, 4, 8), x1)

        acc = jnp.zeros((LOCAL_RC, BRANCH_W), F32) + cb_ref[...]
        for res in range(8):
            acc = acc + ubuf[res, r0 + res:r0 + res + LOCAL_RC, :]
        cbuf[r0:r0 + LOCAL_RC, :] = acc

    a = _dot(ybuf[...].astype(BF16), pw_ref[...]) * ps_ref[...]
    a_out_ref[...] = a.astype(BF16)

    cz = cbuf[...]
    mu = jnp.mean(cz, axis=-1, keepdims=True)
    d = cz - mu
    var = jnp.mean(d * d, axis=-1, keepdims=True)
    zn = _silu(d * jax.lax.rsqrt(var + LN_EPS) * lg_ref[...] + lb_ref[...])
    c_out_ref[...] = _dot(zn.astype(BF16), cpw_ref[...]).astype(BF16)


def _local_mix(ax, z, layer, pool_bd, pool_scale, conv_w, conv_b, ln_g, ln_b, conv_pw):
    B, L, C = ax.shape
    tm = LOCAL_TM
    nt = L // tm
    hb = tm // HALO
    nhb = L // HALO
    tok = pl.BlockSpec((None, tm, C), lambda b, i: (b, i, 0))
    prv = pl.BlockSpec((None, HALO, C), lambda b, i: (b, jnp.maximum(i * hb - 1, 0), 0))
    nxt = pl.BlockSpec((None, HALO, C), lambda b, i: (b, jnp.minimum((i + 1) * hb, nhb - 1), 0))
    per_layer = lambda *shape: _layer_block(shape, layer)
    return pl.pallas_call(
        functools.partial(_local_kernel, seq_len=L),
        grid=(B, nt),
        in_specs=[tok, prv, nxt, tok, prv, nxt, per_layer(C, C), per_layer(1, C), per_layer(CONV_WIDTH, C),
                  per_layer(1, C), per_layer(1, C), per_layer(1, C), per_layer(C, C)],
        out_specs=[tok, tok],
        out_shape=[jax.ShapeDtypeStruct((B, L, C), BF16)] * 2,
        scratch_shapes=[pltpu.VMEM((tm + 2 * HALO, C), F32), pltpu.VMEM((tm + 2 * HALO, C), F32),
                        pltpu.VMEM((tm, C), F32), pltpu.VMEM((tm, C), F32), pltpu.VMEM((8, tm + 8, C), F32)],
        compiler_params=_cparams(2),
        name="local_mix",
    )(ax, ax, ax, z, z, z, pool_bd, pool_scale, conv_w, conv_b, ln_g, ln_b, conv_pw)


def _head_mask(shape, h):
    lane = jax.lax.broadcasted_iota(jnp.int32, shape, 1)
    return (lane >= h * HEAD_DIM) & (lane < (h + 1) * HEAD_DIM)


def _softmax_pv(s, v):
    m = jnp.max(s, axis=-1, keepdims=True)
    p = jnp.exp2(s - m)
    l = jnp.sum(p, axis=-1, keepdims=True)
    return _dot(p.astype(BF16), v) / l


def _attn_ctx_kernel(nat_ref, gqa_ref, n_out_ref, g_out_ref):
    q, k, v = nat_ref[:, 0:256], nat_ref[:, 256:512], nat_ref[:, 512:768]
    acc = jnp.zeros(q.shape, F32)
    for h in range(NAT_HEADS):
        msk = _head_mask(q.shape, h)
        o = _softmax_pv(_dot_t(jnp.where(msk, q, jnp.zeros_like(q)), k), v)
        acc = acc + jnp.where(msk, o, 0.0)
    n_out_ref[...] = acc.astype(BF16)

    k, v = gqa_ref[:, 256:384], gqa_ref[:, 384:512]
    sides = []
    for side in range(2):
        q = gqa_ref[:, side * 128:(side + 1) * 128]
        acc = jnp.zeros(q.shape, F32)
        for kv in range(GQA_KV_HEADS):
            msk = _head_mask(q.shape, kv)
            o = _softmax_pv(_dot_t(jnp.where(msk, q, jnp.zeros_like(q)), k), v)
            acc = acc + jnp.where(msk, o, 0.0)
        sides.append(acc)
    low = _head_mask(sides[0].shape, 0)
    g_out_ref[:, 0:128] = jnp.where(low, sides[0], pltpu.roll(sides[1], HEAD_DIM, 1)).astype(BF16)
    g_out_ref[:, 128:256] = jnp.where(low, pltpu.roll(sides[0], HEAD_DIM, 1), sides[1]).astype(BF16)


def _attn_ctx(nat, gqa):
    B, L, _ = nat.shape
    return pl.pallas_call(
        _attn_ctx_kernel,
        grid=(B,),
        in_specs=[pl.BlockSpec((None, L, 3 * BRANCH_W), lambda b: (b, 0, 0)),
                  pl.BlockSpec((None, L, BRANCH_W + 2 * KV_W), lambda b: (b, 0, 0))],
        out_specs=[pl.BlockSpec((None, L, BRANCH_W), lambda b: (b, 0, 0))] * 2,
        out_shape=[jax.ShapeDtypeStruct((B, L, BRANCH_W), BF16)] * 2,
        compiler_params=_cparams(1),
        name="attn_ctx",
    )(nat, gqa)


NAT_RB = 8
NAT_NLOC = NAT_ROWS * GRID_W


def _nat_kernel(q_ref, k_ref, v_ref, kc_ref, vc_ref, bias_ref, o_ref, s_a, s_b, *, rows):
    i = pl.program_id(1)
    kc = kc_ref[...]
    vc = vc_ref[...]
    nq = NAT_HEADS * GRID_W
    diag = (jax.lax.broadcasted_iota(jnp.int32, (nq, BRANCH_W), 0) // GRID_W
            == jax.lax.broadcasted_iota(jnp.int32, (nq, BRANCH_W), 1) // HEAD_DIM)
    bufs = (s_a, s_b)

    def window(rr):
        r = i * NAT_RB + rr
        row_start = jnp.clip(r - NAT_ROWS // 2, 0, rows - NAT_ROWS)
        return row_start - r + NAT_ROWS - 1, pl.multiple_of(row_start * GRID_W, GRID_W)

    def scores(rr, buf):
        off, base = window(rr)
        q_r = q_ref[rr * GRID_W:(rr + 1) * GRID_W, :]
        qs = jnp.where(diag, jnp.concatenate([q_r] * NAT_HEADS, axis=0), jnp.zeros((nq, BRANCH_W), BF16))
        bias = jnp.concatenate([bias_ref[off + 2 * jj] for jj in range(NAT_ROWS // 2)], axis=1)
        buf[:, 0:NAT_NLOC] = _dot_t(qs, k_ref[pl.ds(base, NAT_NLOC), :]) + bias
        buf[:, NAT_NLOC:] = _dot_t(qs, kc)

    scores(0, s_a)
    for rr in range(NAT_RB):
        cur = bufs[rr % 2]
        if rr + 1 < NAT_RB:
            scores(rr + 1, bufs[(rr + 1) % 2])
        _, base = window(rr)
        vw = v_ref[pl.ds(base, NAT_NLOC), :]
        m = jnp.max(cur[...], axis=-1, keepdims=True)
        p_loc = jnp.exp2(cur[:, 0:NAT_NLOC] - m)
        p_ctx = jnp.exp2(cur[:, NAT_NLOC:] - m)
        l = jnp.sum(p_loc, axis=-1, keepdims=True) + jnp.sum(p_ctx, axis=-1, keepdims=True)
        o = (_dot(p_loc.astype(BF16), vw) + _dot(p_ctx.astype(BF16), vc)) / l
        o = jnp.where(diag, o, 0.0)
        o_r = o[0:GRID_W]
        for h in range(1, NAT_HEADS):
            o_r = o_r + o[h * GRID_W:(h + 1) * GRID_W]
        o_ref[rr * GRID_W:(rr + 1) * GRID_W, :] = o_r.astype(BF16)


def _nat_lat(nat, ctx_k, ctx_v, layer, bias_tiles):
    B, L, _ = nat.shape
    rows = L // GRID_W
    lc = ctx_k.shape[2]
    tq = NAT_RB * GRID_W
    seq = lambda col: pl.BlockSpec((None, L, BRANCH_W), lambda b, i: (b, 0, col))
    ctx = pl.BlockSpec((None, None, lc, BRANCH_W), lambda b, i: (b, layer, 0, 0))
    return pl.pallas_call(
        functools.partial(_nat_kernel, rows=rows),
        grid=(B, rows // NAT_RB),
        in_specs=[pl.BlockSpec((None, tq, BRANCH_W), lambda b, i: (b, i, 0)), seq(1), seq(2), ctx, ctx,
                  _layer_block(bias_tiles.shape[1:], layer)],
        out_specs=pl.BlockSpec((None, tq, BRANCH_W), lambda b, i: (b, i, 0)),
        out_shape=jax.ShapeDtypeStruct((B, L, BRANCH_W), BF16),
        scratch_shapes=[pltpu.VMEM((NAT_HEADS * GRID_W, NAT_NLOC + lc), F32)] * 2,
        compiler_params=_cparams(2),
        name="nat_lat",
    )(nat, nat, nat, ctx_k, ctx_v, bias_tiles)


GQA_TQ = 256
GQA_NCHUNK = 6


def _gqa_kernel(q_ref, k_lat_ref, vt_lat_ref, k_ctx_ref, vt_ctx_ref, o_ref, k_ref, vt_ref, s_a, s_b):
    n_lat = k_lat_ref.shape[0]
    k_ref[0:n_lat, :] = k_lat_ref[...]
    k_ref[n_lat:, :] = k_ctx_ref[...]
    vt_ref[:, 0:n_lat] = vt_lat_ref[...]
    vt_ref[:, n_lat:] = vt_ctx_ref[...]
    kc_len = k_ref.shape[0] // GQA_NCHUNK
    nqt = q_ref.shape[0] // GQA_TQ
    heads = [(side, kv) for side in range(2) for kv in range(GQA_KV_HEADS)]
    bufs = (s_a, s_b)

    def masked_q(t):
        row = pl.multiple_of(t * GQA_TQ, GQA_TQ)
        out = []
        for side, kv in heads:
            q = q_ref[pl.ds(row, GQA_TQ), side * 128:(side + 1) * 128]
            out.append(jnp.where(_head_mask(q.shape, kv), q, jnp.zeros_like(q)))
        return out

    qh0 = masked_q(0)
    for h in range(len(heads)):
        s_a[h] = _dot_t(k_ref[0:kc_len, :], qh0[h])

    def qtile(t, carry):
        qh_cur = masked_q(t)
        qh_next = masked_q(jnp.minimum(t + 1, nqt - 1))
        state = [(jnp.full((1, GQA_TQ), NEG_INF, F32), jnp.zeros((VT_ROWS, GQA_TQ), F32)) for _ in heads]
        for c in range(GQA_NCHUNK):
            cur, nxt = bufs[c % 2], bufs[(c + 1) % 2]
            cn = (c + 1) % GQA_NCHUNK
            qn = qh_cur if c + 1 < GQA_NCHUNK else qh_next
            kn = k_ref[cn * kc_len:(cn + 1) * kc_len, :]
            vt = vt_ref[:, c * kc_len:(c + 1) * kc_len]
            for h, (side, kv) in enumerate(heads):
                nxt[h] = _dot_t(kn, qn[h])
                m, acc = state[h]
                s = cur[h]
                m_new = jnp.maximum(m, jnp.max(s, axis=0, keepdims=True))
                p = jnp.exp2(s - m_new).astype(BF16)
                acc = jnp.exp2(m - m_new) * acc + _dot(vt[kv * VT_ROWS:(kv + 1) * VT_ROWS], p)
                state[h] = (m_new, acc)
        outs = [acc[0:HEAD_DIM] / acc[HEAD_DIM:HEAD_DIM + 1] for _, acc in state]
        row = pl.multiple_of(t * GQA_TQ, GQA_TQ)
        for half in range(2):
            o_t = jnp.concatenate([outs[heads.index((side, half))] for side in range(2)], axis=0)
            o_ref[pl.ds(row, GQA_TQ), half * 128:(half + 1) * 128] = o_t.T.astype(BF16)
        return carry

    jax.lax.fori_loop(0, nqt, qtile, 0)


def _gqa_lat(q, k, vt, ctx_k, ctx_vt, layer):
    B, L, _ = q.shape
    lc = ctx_k.shape[2]
    lk = L + lc
    nvt = GQA_KV_HEADS * VT_ROWS
    assert GQA_NCHUNK % 2 == 0 and lk % (GQA_NCHUNK * 128) == 0 and L % GQA_TQ == 0 and L % 128 == 0
    return pl.pallas_call(
        _gqa_kernel,
        grid=(B,),
        in_specs=[pl.BlockSpec((None, L, BRANCH_W), lambda b: (b, 0, 0)),
                  pl.BlockSpec((None, L, KV_W), lambda b: (b, 0, 0)),
                  pl.BlockSpec((None, nvt, L), lambda b: (b, 0, 0)),
                  pl.BlockSpec((None, None, lc, KV_W), lambda b: (b, layer, 0, 0)),
                  pl.BlockSpec((None, None, nvt, lc), lambda b: (b, layer, 0, 0))],
        out_specs=pl.BlockSpec((None, L, BRANCH_W), lambda b: (b, 0, 0)),
        out_shape=jax.ShapeDtypeStruct((B, L, BRANCH_W), BF16),
        scratch_shapes=[pltpu.VMEM((lk, KV_W), BF16), pltpu.VMEM((nvt, lk), BF16)]
        + [pltpu.VMEM((GQA_HEADS, lk // GQA_NCHUNK, GQA_TQ), F32)] * 2,
        compiler_params=_cparams(1),
        name="gqa_lat",
    )(q, k, vt, ctx_k, ctx_vt)


def _ctx_prep_kernel(nk_ref, nv_ref, gk_ref, gv_ref, nk_o, nv_o, gk_o, gvt_o):
    nk_o[...] = nk_ref[...].astype(BF16)
    nv_o[...] = nv_ref[...].astype(BF16)
    gk_o[...] = gk_ref[...].astype(BF16)
    gvt = gv_ref[...].T.astype(BF16)
    ones = jnp.ones((VT_ROWS - HEAD_DIM, gvt.shape[1]), BF16)
    for kv in range(GQA_KV_HEADS):
        gvt_o[kv * VT_ROWS:kv * VT_ROWS + HEAD_DIM, :] = gvt[kv * HEAD_DIM:(kv + 1) * HEAD_DIM]
        gvt_o[kv * VT_ROWS + HEAD_DIM:(kv + 1) * VT_ROWS, :] = ones


def _ctx_prep(ck_n, cv_n, ck_g, cv_g):
    B, depth, lc, _ = ck_n.shape
    nvt = GQA_KV_HEADS * VT_ROWS
    blk = lambda r, c: pl.BlockSpec((None, None, r, c), lambda b, l: (b, l, 0, 0))
    return pl.pallas_call(
        _ctx_prep_kernel,
        grid=(B, depth),
        in_specs=[blk(lc, BRANCH_W), blk(lc, BRANCH_W), blk(lc, KV_W), blk(lc, KV_W)],
        out_specs=[blk(lc, BRANCH_W), blk(lc, BRANCH_W), blk(lc, KV_W), blk(nvt, lc)],
        out_shape=[jax.ShapeDtypeStruct((B, depth, lc, BRANCH_W), BF16)] * 2
        + [jax.ShapeDtypeStruct((B, depth, lc, KV_W), BF16), jax.ShapeDtypeStruct((B, depth, nvt, lc), BF16)],
        compiler_params=_cparams(2),
        name="ctx_prep",
    )(ck_n, cv_n, ck_g, cv_g)


OUT_RC = 256


def _outproj_kernel(a_ref, n_ref, c_ref, g_ref, gates_ref, x_ref, gm_ref, w_ref, b_ref, lg_ref, lb_ref, o_ref):
    for r0 in range(0, x_ref.shape[0], OUT_RC):
        rows = slice(r0, r0 + OUT_RC)
        mixed = jnp.concatenate([m_ref[rows, :] * gates_ref[rows, i * BRANCH_W:(i + 1) * BRANCH_W]
                                 for i, m_ref in enumerate((a_ref, n_ref, c_ref, g_ref))], axis=1)
        out = _dot(mixed, w_ref[...]) + b_ref[...]
        y = DEEPNORM_ALPHA * x_ref[rows, :] + gm_ref[...] * out
        mu = jnp.mean(y, axis=-1, keepdims=True)
        d = y - mu
        var = jnp.mean(d * d, axis=-1, keepdims=True)
        o_ref[rows, :] = d * jax.lax.rsqrt(var + LN_EPS) * lg_ref[...] + lb_ref[...]


def _outproj(a, n, c, g, gates, x, mod, mod_row0, layer, w_out, b_out, ln_g, ln_b, tm):
    B, L, D = x.shape
    tok = lambda w: pl.BlockSpec((None, tm, w), lambda i, b: (b, i, 0))
    per_layer = lambda *shape: _layer_block(shape, layer)
    return pl.pallas_call(
        _outproj_kernel,
        grid=(L // tm, B),
        in_specs=[tok(BRANCH_W)] * 4 + [
            tok(D), tok(D),
            pl.BlockSpec((None, None, 1, D), lambda i, b: (layer, mod_row0 + b, 0, 2)),
            per_layer(D, D), per_layer(1, D), per_layer(1, D), per_layer(1, D)],
        out_specs=tok(D),
        out_shape=jax.ShapeDtypeStruct((B, L, D), F32),
        compiler_params=_cparams(2),
        name="outproj",
    )(a, n, c, g, gates, x, mod, w_out, b_out, ln_g, ln_b)


_GQA_HEAD_ORDER = (0, 2, 1, 3)


def _rope_tables(seq_len):
    half = HEAD_DIM // 2
    nf = half // 2
    t = jnp.arange(seq_len)
    inv = ROPE_THETA ** (-jnp.arange(nf, dtype=F32) * 2.0 / half)

    def tabs(pos):
        ang = pos.astype(F32)[:, None] * inv[None, :]
        return jnp.cos(ang), jnp.sin(ang)

    cr, sr = tabs(t // GRID_W)
    cc, sc = tabs(t % GRID_W)
    zero = jnp.zeros_like(sr)
    cos = jnp.concatenate([cr, cr, cc, cc], axis=1)
    sin_a = jnp.concatenate([-sr, zero, -sc, zero], axis=1)
    sin_b = jnp.concatenate([zero, sr, zero, sc], axis=1)
    return tuple(jnp.tile(a, (1, 2)) for a in (cos, sin_a, sin_b))


def _nat_bias_tiles(bias_tab):
    col = np.arange(GRID_W)
    col_start = np.clip(col - NAT_COLS // 2, 0, GRID_W - NAT_COLS)
    col_in = (col[None, :] >= col_start[:, None]) & (col[None, :] < col_start[:, None] + NAT_COLS)
    edge = GRID_W - NAT_COLS
    n_diff = 2 * GRID_W - 1
    padded = jnp.concatenate([jnp.repeat(bias_tab[..., :1], edge, axis=-1), bias_tab,
                              jnp.repeat(bias_tab[..., -1:], edge, axis=-1)], axis=-1)
    flat = jnp.tile(padded, (1, 1, 1, GRID_W + 1))
    toeplitz = flat[..., GRID_W - 1:GRID_W - 1 + GRID_W * (n_diff - 1)].reshape(
        bias_tab.shape[:-1] + (GRID_W, n_diff - 1))[..., :GRID_W]
    full = jnp.where(col_in, toeplitz * LOG2E, NEG_INF)
    pair = jnp.concatenate([full[:, :, :-1], full[:, :, 1:]], axis=-1)
    return jnp.transpose(pair, (0, 2, 1, 3, 4)).reshape(-1, 2 * NAT_ROWS - 2, NAT_HEADS * GRID_W, 2 * GRID_W)


def kernel(x_prompt, x_sample, c, cache_nat_k, cache_nat_v, cache_gqa_k, cache_gqa_v, c_ctx, w_mod, b_mod, w_in, b_in,
           pool_w, pool_scale, nat_bias, q_norm, k_norm, conv_w, conv_b, conv_ln_g, conv_ln_b, conv_pw, w_out, b_out,
           ln_g, ln_b):
    nb, seq, D = x_prompt.shape
    db, dseq, _ = x_sample.shape
    lc = cache_nat_k.shape[2]

    cond = jnp.zeros((MOD_ROWS, D), F32).at[0].set(c_ctx).at[1:1 + db].set(c)
    mod = _modulation(cond, w_mod, b_mod).reshape(DEPTH, MOD_ROWS, 1, 3 * D)

    def kernel_columns(a):
        return jnp.concatenate([a[..., c0:c0 + w] for _, c0, w in _IN_SEGMENTS], axis=-1)

    w_in_b = kernel_columns(w_in).astype(BF16)
    b_in_p = kernel_columns(b_in).reshape(DEPTH, 1, IN_WIDTH)
    w_out_b = w_out.astype(BF16)

    seg = jnp.asarray(np.kron(np.eye(GQA_HEADS), np.ones((HEAD_DIM, HEAD_DIM))), BF16)
    rope_tabs = _rope_tables(dseq)
    ck_n, cv_n, ck_g, cvt_g = _ctx_prep(cache_nat_k.reshape(db, DEPTH, lc, BRANCH_W),
                                        cache_nat_v.reshape(db, DEPTH, lc, BRANCH_W),
                                        cache_gqa_k.reshape(db, DEPTH, lc, KV_W),
                                        cache_gqa_v.reshape(db, DEPTH, lc, KV_W))

    row = lambda a: a.reshape(DEPTH, 1, -1)
    n_groups = len(POOL_WINDOWS)
    pool_bd = jnp.einsum('gh,lgcd->lgchd', jnp.eye(n_groups, dtype=F32), pool_w).reshape(DEPTH, BRANCH_W, BRANCH_W)
    in_w = (w_in_b, b_in_p, row(jnp.tile(q_norm, (1, GQA_HEADS))), row(jnp.tile(k_norm, (1, GQA_KV_HEADS))), seg)
    local_w = (pool_bd.astype(BF16), row(pool_scale), conv_w, row(conv_b), row(conv_ln_g), row(conv_ln_b),
               conv_pw.astype(BF16))
    out_w = (w_out_b, row(b_out), row(ln_g), row(ln_b))
    bias_tiles = _nat_bias_tiles(nat_bias)

    y_p = x_prompt.reshape(1, nb * seq, D)
    y_s = x_sample
    new_cache = [jnp.zeros((nb, DEPTH, seq, w), F32) for w in (BRANCH_W, BRANCH_W, KV_W, KV_W)]
    for l in range(DEPTH):
        ax, z, gates, nat, gqa, *new_cache = _inproj(y_p, mod, 0, l, *in_w, None, new_cache, 512)
        a_out, c_out = _local_mix(ax.reshape(nb, seq, BRANCH_W), z.reshape(nb, seq, BRANCH_W), l, *local_w)
        n_out, g_out = _attn_ctx(nat.reshape(nb, seq, -1), gqa.reshape(nb, seq, -1))
        flat = lambda a: a.reshape(1, nb * seq, BRANCH_W)
        y_p = _outproj(flat(a_out), flat(n_out), flat(c_out), flat(g_out), gates, y_p, mod, 0, l, *out_w, 512)

        ax, z, gates, nat, gq, gk, gvt = _inproj(y_s, mod, 1, l, *in_w, rope_tabs, None, 512)
        a_out, c_out = _local_mix(ax, z, l, *local_w)
        n_out = _nat_lat(nat, ck_n, cv_n, l, bias_tiles)
        g_out = _gqa_lat(gq, gk, gvt, ck_g, cvt_g, l)
        y_s = _outproj(a_out, n_out, c_out, g_out, gates, y_s, mod, 1, l, *out_w, 512)

    heads = (NAT_HEADS, NAT_HEADS, GQA_KV_HEADS, GQA_KV_HEADS)
    return (y_p.reshape(nb, seq, D), y_s) + tuple(c.reshape(nb, DEPTH, seq, h, HEAD_DIM)
                                                  for c, h in zip(new_cache, heads))
```

```python
import functools

import numpy as np
import jax
import jax.numpy as jnp
from jax.experimental import pallas as pl
from jax.experimental.pallas import tpu as pltpu

F32 = jnp.float32
BF16 = jnp.bfloat16

D_MODEL = 1024
DEPTH = 4
GRID_W = 64
HEAD_DIM = 64
BRANCH_W = D_MODEL // 4
POOL_WINDOWS = (2, 4, 8, 16)
POOL_GROUP_W = BRANCH_W // len(POOL_WINDOWS)
NAT_HEADS = BRANCH_W // HEAD_DIM
NAT_ROWS = 8
NAT_COLS = 16
CONV_WIDTH = 31
GQA_HEADS = BRANCH_W // HEAD_DIM
GQA_KV_HEADS = GQA_HEADS // 2
KV_W = GQA_KV_HEADS * HEAD_DIM
ROPE_THETA = 10000.0
LN_EPS = 1e-5
RMS_EPS = 1e-6
NEG_INF = -1e30
DEEPNORM_ALPHA = (2 * DEPTH) ** 0.25
LOG2E = 1.4426950408889634
Q_SCALE = HEAD_DIM ** -0.5 * LOG2E
VT_ROWS = 80

_IN_SEGMENTS = (
    [("gq%d" % h, 2304 + h * HEAD_DIM, HEAD_DIM) for h in (0, 2, 1, 3)]
    + [("gk", 2560, 128), ("gv", 2688, 128), ("ax", 0, 256), ("cv", 1536, 256), ("cglu", 1792, 256),
       ("ag", 256, 256), ("ng", 1280, 256), ("cg", 2048, 256), ("gg", 2816, 256),
       ("nq", 512, 256), ("nk", 768, 256), ("nv", 1024, 256)])
IN_WIDTH = sum(w for _, _, w in _IN_SEGMENTS)
_IN_OFFSET = dict(zip((n for n, _, _ in _IN_SEGMENTS), np.cumsum([0] + [w for _, _, w in _IN_SEGMENTS])[:-1].tolist()))
C_GQ, C_GK, C_GV = _IN_OFFSET["gq0"], _IN_OFFSET["gk"], _IN_OFFSET["gv"]
C_AX, C_CV, C_CGLU = _IN_OFFSET["ax"], _IN_OFFSET["cv"], _IN_OFFSET["cglu"]
C_AG, C_NG, C_CG, C_GG = _IN_OFFSET["ag"], _IN_OFFSET["ng"], _IN_OFFSET["cg"], _IN_OFFSET["gg"]
C_NQ, C_NK, C_NV = _IN_OFFSET["nq"], _IN_OFFSET["nk"], _IN_OFFSET["nv"]

HALO = 16
VMEM_LIMIT = 56 * 1024 * 1024
MOD_ROWS = 16


def _cparams(n_axes):
    return pltpu.CompilerParams(dimension_semantics=("arbitrary",) * n_axes, vmem_limit_bytes=VMEM_LIMIT)


def _layer_block(shape, layer):
    return pl.BlockSpec((None,) + tuple(shape), lambda *_: (layer,) + (0,) * len(shape))


def _silu(x):
    return x * jax.nn.sigmoid(x)


def _dot(a, b):
    return jnp.dot(a, b, preferred_element_type=F32)


def _dot_t(a, b):
    return jax.lax.dot_general(a, b, (((1,), (1,)), ((), ())), preferred_element_type=F32)


def _mod_kernel(cond_ref, w_ref, b_ref, o_ref):
    a = _silu(cond_ref[...]).astype(BF16)
    o_ref[...] = _dot(a, w_ref[...].astype(BF16)) + b_ref[...]


def _modulation(cond, w_mod, b_mod):
    nj = 3 * D_MODEL // 1024
    return pl.pallas_call(
        _mod_kernel,
        grid=(DEPTH, nj),
        in_specs=[
            pl.BlockSpec((MOD_ROWS, D_MODEL), lambda l, j: (0, 0)),
            pl.BlockSpec((None, D_MODEL, 1024), lambda l, j: (l, 0, j)),
            pl.BlockSpec((None, 1, 1024), lambda l, j: (l, 0, j)),
        ],
        out_specs=pl.BlockSpec((None, MOD_ROWS, 1024), lambda l, j: (l, 0, j)),
        out_shape=jax.ShapeDtypeStruct((DEPTH, MOD_ROWS, 3 * D_MODEL), F32),
        compiler_params=_cparams(2),
        name="modulation",
    )(cond, w_mod, b_mod.reshape(DEPTH, 1, 3 * D_MODEL))


def _head_rms(x, seg, gain):
    tot = _dot((x * x).astype(BF16), seg)
    return x * jax.lax.rsqrt(tot * (1.0 / HEAD_DIM) + RMS_EPS) * gain


def _rope(x, cos, sin_a, sin_b):
    n = x.shape[-1]
    return x * cos + pltpu.roll(x, n - 16, 1) * sin_a + pltpu.roll(x, 16, 1) * sin_b


def _inproj_kernel(*refs, latent, seq_tiles, seq_len):
    (x_ref, xp_ref, xn_ref, sh_ref, sc_ref, w_ref, b_ref, qn_ref, kn_ref, seg_ref,
     pw_ref, ps_ref, cw_ref, cb_ref, lg_ref, lb_ref, cpw_ref) = refs[:17]
    refs = refs[17:]
    if latent:
        cos_ref, sa_ref, sb_ref = refs[:3]
        refs = refs[3:]
    else:
        refs = refs[4:]
    a_out_ref, c_out_ref, gates_ref, nat_ref, gqa_ref = refs[:5]
    refs = refs[5:]
    abuf, zbuf, ybuf, cbuf, ubuf = refs[-5:]
    refs = refs[:-5]
    tm = x_ref.shape[0]
    tile = pl.program_id(0) % seq_tiles

    def put_cache(ref, val):
        seq = ref.shape[1]
        for sidx in range(ref.shape[0]):
            ref[sidx] = val[sidx * seq:(sidx + 1) * seq]

    u_ref = refs[-1]
    refs = refs[:-1]

    def modulate(x):
        return (x * (1.0 + sc_ref[...]) + sh_ref[...]).astype(BF16)

    h = modulate(x_ref[...])
    h_ext = jnp.concatenate([modulate(xp_ref[...]), h, modulate(xn_ref[...])], axis=0)

    def proj(c0, n):
        return u_ref[:, c0:c0 + n] + b_ref[:, c0:c0 + n]

    loc = _dot(h_ext, w_ref[:, C_AX:C_AG]) + b_ref[:, C_AX:C_AG]
    row = jax.lax.broadcasted_iota(jnp.int32, (tm + 2 * HALO, 1), 0)
    inside = ((row >= HALO) | (tile > 0)) & ((row < HALO + tm) | (tile < seq_tiles - 1))
    abuf[...] = jnp.where(inside, loc[:, 0:BRANCH_W], 0.0)
    zbuf[...] = jnp.where(inside, loc[:, BRANCH_W:2 * BRANCH_W] * jax.nn.sigmoid(loc[:, 2 * BRANCH_W:]), 0.0)
    pieces = _local_mixers(abuf, zbuf, ybuf, cbuf, ubuf, pw_ref, ps_ref, cw_ref, cb_ref, lg_ref, lb_ref, cpw_ref,
                           tile * tm, seq_len)
    result = []

    def advance(n):
        for _ in range(n):
            if not result:
                try:
                    next(pieces)
                except StopIteration as done:
                    result.append(done.value)

    u_ref[:, :C_AX] = _dot(h, w_ref[:, :C_AX])
    advance(8)
    u_ref[:, C_AG:] = _dot(h, w_ref[:, C_AG:])

    gq = _head_rms(proj(C_GQ, BRANCH_W), seg_ref[...], qn_ref[...])
    gk = _head_rms(proj(C_GK, KV_W), seg_ref[0:KV_W, 0:KV_W], kn_ref[...])
    gv = proj(C_GV, KV_W)
    if not latent:
        nk_ref, nv_ref, gk_ref, gv_ref = refs
        put_cache(gk_ref, gk)
        put_cache(gv_ref, gv)
        gqa_ref[:, 256:384] = gk.astype(BF16)
        gqa_ref[:, 384:512] = gv.astype(BF16)
    else:
        cos, sa, sb = cos_ref[...], sa_ref[...], sb_ref[...]
        gk = _rope(gk, cos, sa, sb)
        gq = _rope(gq, jnp.concatenate([cos, cos], axis=1), jnp.concatenate([sa, sa], axis=1),
                   jnp.concatenate([sb, sb], axis=1))
        k_ref, vt_ref = refs
        k_ref[...] = gk.astype(BF16)
        gvt = gv.T.astype(BF16)
        ones = jnp.ones((VT_ROWS - HEAD_DIM, gvt.shape[1]), BF16)
        for kv in range(GQA_KV_HEADS):
            vt_ref[kv * VT_ROWS:kv * VT_ROWS + HEAD_DIM, :] = gvt[kv * HEAD_DIM:(kv + 1) * HEAD_DIM]
            vt_ref[kv * VT_ROWS + HEAD_DIM:(kv + 1) * VT_ROWS, :] = ones
    gqa_ref[:, 0:256] = (gq * Q_SCALE).astype(BF16)

    while not result:
        advance(1)
    a, c = result[0]
    a_out_ref[...] = (a * _silu(proj(C_AG, BRANCH_W))).astype(BF16)
    c_out_ref[...] = (c * _silu(proj(C_CG, BRANCH_W))).astype(BF16)
    gates_ref[:, 0:BRANCH_W] = _silu(proj(C_NG, BRANCH_W)).astype(BF16)
    gates_ref[:, BRANCH_W:] = _silu(proj(C_GG, BRANCH_W)).astype(BF16)

    nat_ref[:, 0:256] = (proj(C_NQ, BRANCH_W) * Q_SCALE).astype(BF16)
    nk = proj(C_NK, BRANCH_W)
    nv = proj(C_NV, BRANCH_W)
    nat_ref[:, 256:512] = nk.astype(BF16)
    nat_ref[:, 512:768] = nv.astype(BF16)
    if not latent:
        put_cache(nk_ref, nk)
        put_cache(nv_ref, nv)


def _inproj(x, mod, mod_row0, layer, w_in, b_in, qn, kn, seg, local_w, rope_tabs, new_cache, tm):
    B, L, D = x.shape
    nt = L // tm
    latent = rope_tabs is not None
    seq_tiles = nt if latent else 1
    hb = tm // HALO
    nhb = L // HALO
    tok = lambda w: pl.BlockSpec((None, tm, w), lambda i, b: (b, i, 0))
    per_layer = lambda *shape: _layer_block(shape, layer)
    C = BRANCH_W
    in_specs = [
        tok(D),
        pl.BlockSpec((None, HALO, D), lambda i, b: (b, jnp.maximum(i * hb - 1, 0), 0)),
        pl.BlockSpec((None, HALO, D), lambda i, b: (b, jnp.minimum((i + 1) * hb, nhb - 1), 0)),
        pl.BlockSpec((None, None, 1, D), lambda i, b: (layer, mod_row0 + b, 0, 0)),
        pl.BlockSpec((None, None, 1, D), lambda i, b: (layer, mod_row0 + b, 0, 1)),
        per_layer(D, IN_WIDTH), per_layer(1, IN_WIDTH), per_layer(1, BRANCH_W), per_layer(1, KV_W),
        pl.BlockSpec((BRANCH_W, BRANCH_W), lambda i, b: (0, 0)),
        per_layer(C, C), per_layer(1, C), per_layer(CONV_WIDTH, C), per_layer(1, C), per_layer(1, C), per_layer(1, C),
        per_layer(C, C),
    ]
    args = [x, x, x, mod, mod, w_in, b_in, qn, kn, seg, *local_w]
    if latent:
        in_specs += [pl.BlockSpec((tm, KV_W), lambda i, b: (i, 0))] * 3
        args += list(rope_tabs)
    gqa_w = BRANCH_W if latent else BRANCH_W + 2 * KV_W
    out_specs = [tok(BRANCH_W), tok(BRANCH_W), tok(2 * BRANCH_W), tok(3 * BRANCH_W), tok(gqa_w)]
    out_shape = [jax.ShapeDtypeStruct((B, L, BRANCH_W), BF16), jax.ShapeDtypeStruct((B, L, BRANCH_W), BF16),
                 jax.ShapeDtypeStruct((B, L, 2 * BRANCH_W), BF16), jax.ShapeDtypeStruct((B, L, 3 * BRANCH_W), BF16),
                 jax.ShapeDtypeStruct((B, L, gqa_w), BF16)]
    if latent:
        out_specs += [tok(KV_W), pl.BlockSpec((None, GQA_KV_HEADS * VT_ROWS, tm), lambda i, b: (b, 0, i))]
        out_shape += [jax.ShapeDtypeStruct((B, L, KV_W), BF16),
                      jax.ShapeDtypeStruct((B, GQA_KV_HEADS * VT_ROWS, L), BF16)]
        aliases = {}
    else:
        seq = new_cache[0].shape[2]
        assert B == 1 and tm == seq
        aliases = {len(args) + j: len(out_specs) + j for j in range(len(new_cache))}
        in_specs += [pl.BlockSpec(memory_space=pl.ANY)] * len(new_cache)
        args += list(new_cache)
        out_specs += [pl.BlockSpec((1, None, seq, c.shape[3]), lambda i, b: (i, layer, 0, 0)) for c in new_cache]
        out_shape += [jax.ShapeDtypeStruct(c.shape, c.dtype) for c in new_cache]
    return pl.pallas_call(
        functools.partial(_inproj_kernel, latent=latent, seq_tiles=seq_tiles, seq_len=L if latent else tm),
        grid=(nt, B),
        in_specs=in_specs,
        out_specs=out_specs,
        out_shape=out_shape,
        input_output_aliases=aliases,
        scratch_shapes=[pltpu.VMEM((tm, IN_WIDTH), F32),
                        pltpu.VMEM((tm + 2 * HALO, C), F32), pltpu.VMEM((tm + 2 * HALO, C), F32),
                        pltpu.VMEM((tm, C), F32), pltpu.VMEM((tm, C), F32), pltpu.VMEM((8, tm + 8, C), F32)],
        compiler_params=_cparams(2),
        name="inproj_lat" if latent else "inproj_ctx",
    )(*args)


LOCAL_RC = 64


def _conv_chunk(rows):
    return max(c for c in range(8, 105, 8) if rows % c == 0)


def _local_mixers(abuf, zbuf, ybuf, cbuf, ubuf, pw_ref, ps_ref, cw_ref, cb_ref, lg_ref, lb_ref, cpw_ref, t0, seq_len):
    tm = ybuf.shape[0]
    LOCAL_UC = _conv_chunk(tm + 8)

    half_w = CONV_WIDTH // 2
    for res in range(8):
        taps = [8 * a + res for a in range(-2, 2) if -half_w <= 8 * a + res <= half_w]
        for u0 in range(0, tm + 8, LOCAL_UC):
            part = None
            for d in taps:
                term = zbuf[HALO + u0 + d - res:HALO + u0 + d - res + LOCAL_UC, :] * cw_ref[d + half_w:d + half_w + 1, :]
                part = term if part is None else part + term
            ubuf[res, u0:u0 + LOCAL_UC, :] = part
        yield

    lane = jax.lax.broadcasted_iota(jnp.int32, (LOCAL_RC, 128), 1)
    low = lane < POOL_GROUP_W
    for r0 in range(0, tm, LOCAL_RC):
        t = t0 + r0 + jax.lax.broadcasted_iota(jnp.int32, (LOCAL_RC, 128), 0)

        def sh(off, c0):
            return abuf[HALO + r0 + off:HALO + r0 + off + LOCAL_RC, c0:c0 + 128]

        def centred(total, half, x):
            cnt = jnp.minimum(t + half, seq_len) - jnp.maximum(t - half, 0)
            return total / cnt.astype(F32) - x

        x0 = sh(0, 0)
        s2 = x0 + sh(-1, 0)
        s4 = s2 + sh(-2, 0) + sh(1, 0)
        ybuf[r0:r0 + LOCAL_RC, 0:128] = centred(jnp.where(low, s2, s4), jnp.where(low, 1, 2), x0)
        x1 = sh(0, 128)
        s8 = x1
        for off in (-4, -3, -2, -1, 1, 2, 3):
            s8 = s8 + sh(off, 128)
        s16 = s8
        for off in (-8, -7, -6, -5, 4, 5, 6, 7):
            s16 = s16 + sh(off, 128)
        ybuf[r0:r0 + LOCAL_RC, 128:256] = centred(jnp.where(low, s8, s16), jnp.where(low, 4, 8), x1)

        acc = jnp.zeros((LOCAL_RC, BRANCH_W), F32) + cb_ref[...]
        for res in range(8):
            acc = acc + ubuf[res, r0 + res:r0 + res + LOCAL_RC, :]
        cbuf[r0:r0 + LOCAL_RC, :] = acc
        yield

    a = _dot(ybuf[...].astype(BF16), pw_ref[...]) * ps_ref[...]

    cz = cbuf[...]
    mu = jnp.mean(cz, axis=-1, keepdims=True)
    d = cz - mu
    var = jnp.mean(d * d, axis=-1, keepdims=True)
    zn = _silu(d * jax.lax.rsqrt(var + LN_EPS) * lg_ref[...] + lb_ref[...])
    return a, _dot(zn.astype(BF16), cpw_ref[...])


def _head_mask(shape, h):
    lane = jax.lax.broadcasted_iota(jnp.int32, shape, 1)
    return (lane >= h * HEAD_DIM) & (lane < (h + 1) * HEAD_DIM)


def _softmax_pv(s, v):
    m = jnp.max(s, axis=-1, keepdims=True)
    p = jnp.exp2(s - m)
    l = jnp.sum(p, axis=-1, keepdims=True)
    return _dot(p.astype(BF16), v) / l


def _attn_ctx_kernel(nat_ref, gqa_ref, n_out_ref, g_out_ref):
    q, k, v = nat_ref[:, 0:256], nat_ref[:, 256:512], nat_ref[:, 512:768]
    acc = jnp.zeros(q.shape, F32)
    for h in range(NAT_HEADS):
        msk = _head_mask(q.shape, h)
        o = _softmax_pv(_dot_t(jnp.where(msk, q, jnp.zeros_like(q)), k), v)
        acc = acc + jnp.where(msk, o, 0.0)
    n_out_ref[...] = acc.astype(BF16)

    k, v = gqa_ref[:, 256:384], gqa_ref[:, 384:512]
    sides = []
    for side in range(2):
        q = gqa_ref[:, side * 128:(side + 1) * 128]
        acc = jnp.zeros(q.shape, F32)
        for kv in range(GQA_KV_HEADS):
            msk = _head_mask(q.shape, kv)
            o = _softmax_pv(_dot_t(jnp.where(msk, q, jnp.zeros_like(q)), k), v)
            acc = acc + jnp.where(msk, o, 0.0)
        sides.append(acc)
    low = _head_mask(sides[0].shape, 0)
    g_out_ref[:, 0:128] = jnp.where(low, sides[0], pltpu.roll(sides[1], HEAD_DIM, 1)).astype(BF16)
    g_out_ref[:, 128:256] = jnp.where(low, pltpu.roll(sides[0], HEAD_DIM, 1), sides[1]).astype(BF16)


def _attn_ctx(nat, gqa):
    B, L, _ = nat.shape
    return pl.pallas_call(
        _attn_ctx_kernel,
        grid=(B,),
        in_specs=[pl.BlockSpec((None, L, 3 * BRANCH_W), lambda b: (b, 0, 0)),
                  pl.BlockSpec((None, L, BRANCH_W + 2 * KV_W), lambda b: (b, 0, 0))],
        out_specs=[pl.BlockSpec((None, L, BRANCH_W), lambda b: (b, 0, 0))] * 2,
        out_shape=[jax.ShapeDtypeStruct((B, L, BRANCH_W), BF16)] * 2,
        compiler_params=_cparams(1),
        name="attn_ctx",
    )(nat, gqa)


NAT_RB = 8
NAT_NLOC = NAT_ROWS * GRID_W


def _nat_kernel(q_ref, k_ref, v_ref, kc_ref, vc_ref, bias_ref, o_ref, s_a, s_b, *, rows):
    i = pl.program_id(1)
    kc = kc_ref[...]
    vc = vc_ref[...]
    nq = NAT_HEADS * GRID_W
    diag = (jax.lax.broadcasted_iota(jnp.int32, (nq, BRANCH_W), 0) // GRID_W
            == jax.lax.broadcasted_iota(jnp.int32, (nq, BRANCH_W), 1) // HEAD_DIM)
    bufs = (s_a, s_b)

    def window(rr):
        r = i * NAT_RB + rr
        row_start = jnp.clip(r - NAT_ROWS // 2, 0, rows - NAT_ROWS)
        return row_start - r + NAT_ROWS - 1, pl.multiple_of(row_start * GRID_W, GRID_W)

    def scores(rr, buf):
        off, base = window(rr)
        q_r = q_ref[rr * GRID_W:(rr + 1) * GRID_W, :]
        qs = jnp.where(diag, jnp.concatenate([q_r] * NAT_HEADS, axis=0), jnp.zeros((nq, BRANCH_W), BF16))
        bias = jnp.concatenate([bias_ref[off + 2 * jj] for jj in range(NAT_ROWS // 2)], axis=1)
        buf[:, 0:NAT_NLOC] = _dot_t(qs, k_ref[pl.ds(base, NAT_NLOC), :]) + bias
        buf[:, NAT_NLOC:] = _dot_t(qs, kc)

    scores(0, s_a)
    for rr in range(NAT_RB):
        cur = bufs[rr % 2]
        if rr + 1 < NAT_RB:
            scores(rr + 1, bufs[(rr + 1) % 2])
        _, base = window(rr)
        vw = v_ref[pl.ds(base, NAT_NLOC), :]
        m = jnp.max(cur[...], axis=-1, keepdims=True)
        p_loc = jnp.exp2(cur[:, 0:NAT_NLOC] - m)
        p_ctx = jnp.exp2(cur[:, NAT_NLOC:] - m)
        l = jnp.sum(p_loc, axis=-1, keepdims=True) + jnp.sum(p_ctx, axis=-1, keepdims=True)
        o = (_dot(p_loc.astype(BF16), vw) + _dot(p_ctx.astype(BF16), vc)) / l
        o = jnp.where(diag, o, 0.0)
        o_r = o[0:GRID_W]
        for h in range(1, NAT_HEADS):
            o_r = o_r + o[h * GRID_W:(h + 1) * GRID_W]
        o_ref[rr * GRID_W:(rr + 1) * GRID_W, :] = o_r.astype(BF16)


def _nat_lat(nat, ctx_k, ctx_v, layer, bias_tiles):
    B, L, _ = nat.shape
    rows = L // GRID_W
    lc = ctx_k.shape[2]
    tq = NAT_RB * GRID_W
    seq = lambda col: pl.BlockSpec((None, L, BRANCH_W), lambda b, i: (b, 0, col))
    ctx = pl.BlockSpec((None, None, lc, BRANCH_W), lambda b, i: (b, layer, 0, 0))
    return pl.pallas_call(
        functools.partial(_nat_kernel, rows=rows),
        grid=(B, rows // NAT_RB),
        in_specs=[pl.BlockSpec((None, tq, BRANCH_W), lambda b, i: (b, i, 0)), seq(1), seq(2), ctx, ctx,
                  _layer_block(bias_tiles.shape[1:], layer)],
        out_specs=pl.BlockSpec((None, tq, BRANCH_W), lambda b, i: (b, i, 0)),
        out_shape=jax.ShapeDtypeStruct((B, L, BRANCH_W), BF16),
        scratch_shapes=[pltpu.VMEM((NAT_HEADS * GRID_W, NAT_NLOC + lc), F32)] * 2,
        compiler_params=_cparams(2),
        name="nat_lat",
    )(nat, nat, nat, ctx_k, ctx_v, bias_tiles)


GQA_TQ = 256
GQA_NCHUNK = 6


def _gqa_kernel(q_ref, k_lat_ref, vt_lat_ref, k_ctx_ref, vt_ctx_ref, o_ref, k_ref, vt_ref, s_a, s_b):
    n_lat = k_lat_ref.shape[0]
    k_ref[0:n_lat, :] = k_lat_ref[...]
    k_ref[n_lat:, :] = k_ctx_ref[...]
    vt_ref[:, 0:n_lat] = vt_lat_ref[...]
    vt_ref[:, n_lat:] = vt_ctx_ref[...]
    kc_len = k_ref.shape[0] // GQA_NCHUNK
    nqt = q_ref.shape[0] // GQA_TQ
    heads = [(side, kv) for side in range(2) for kv in range(GQA_KV_HEADS)]
    bufs = (s_a, s_b)

    def masked_q(t):
        row = pl.multiple_of(t * GQA_TQ, GQA_TQ)
        out = []
        for side, kv in heads:
            q = q_ref[pl.ds(row, GQA_TQ), side * 128:(side + 1) * 128]
            out.append(jnp.where(_head_mask(q.shape, kv), q, jnp.zeros_like(q)))
        return out

    qh0 = masked_q(0)
    for h in range(len(heads)):
        s_a[h] = _dot_t(k_ref[0:kc_len, :], qh0[h])

    def qtile(t, carry):
        qh_cur = masked_q(t)
        qh_next = masked_q(jnp.minimum(t + 1, nqt - 1))
        state = [(jnp.full((1, GQA_TQ), NEG_INF, F32), jnp.zeros((VT_ROWS, GQA_TQ), F32)) for _ in heads]
        for c in range(GQA_NCHUNK):
            cur, nxt = bufs[c % 2], bufs[(c + 1) % 2]
            cn = (c + 1) % GQA_NCHUNK
            qn = qh_cur if c + 1 < GQA_NCHUNK else qh_next
            kn = k_ref[cn * kc_len:(cn + 1) * kc_len, :]
            vt = vt_ref[:, c * kc_len:(c + 1) * kc_len]
            for h, (side, kv) in enumerate(heads):
                nxt[h] = _dot_t(kn, qn[h])
                m, acc = state[h]
                s = cur[h]
                m_new = jnp.maximum(m, jnp.max(s, axis=0, keepdims=True))
                p = jnp.exp2(s - m_new).astype(BF16)
                acc = jnp.exp2(m - m_new) * acc + _dot(vt[kv * VT_ROWS:(kv + 1) * VT_ROWS], p)
                state[h] = (m_new, acc)
        outs = [acc[0:HEAD_DIM] / acc[HEAD_DIM:HEAD_DIM + 1] for _, acc in state]
        row = pl.multiple_of(t * GQA_TQ, GQA_TQ)
        for half in range(2):
            o_t = jnp.concatenate([outs[heads.index((side, half))] for side in range(2)], axis=0)
            o_ref[pl.ds(row, GQA_TQ), half * 128:(half + 1) * 128] = o_t.T.astype(BF16)
        return carry

    jax.lax.fori_loop(0, nqt, qtile, 0)


def _gqa_lat(q, k, vt, ctx_k, ctx_vt, layer):
    B, L, _ = q.shape
    lc = ctx_k.shape[2]
    lk = L + lc
    nvt = GQA_KV_HEADS * VT_ROWS
    assert GQA_NCHUNK % 2 == 0 and lk % (GQA_NCHUNK * 128) == 0 and L % GQA_TQ == 0 and L % 128 == 0
    return pl.pallas_call(
        _gqa_kernel,
        grid=(B,),
        in_specs=[pl.BlockSpec((None, L, BRANCH_W), lambda b: (b, 0, 0)),
                  pl.BlockSpec((None, L, KV_W), lambda b: (b, 0, 0)),
                  pl.BlockSpec((None, nvt, L), lambda b: (b, 0, 0)),
                  pl.BlockSpec((None, None, lc, KV_W), lambda b: (b, layer, 0, 0)),
                  pl.BlockSpec((None, None, nvt, lc), lambda b: (b, layer, 0, 0))],
        out_specs=pl.BlockSpec((None, L, BRANCH_W), lambda b: (b, 0, 0)),
        out_shape=jax.ShapeDtypeStruct((B, L, BRANCH_W), BF16),
        scratch_shapes=[pltpu.VMEM((lk, KV_W), BF16), pltpu.VMEM((nvt, lk), BF16)]
        + [pltpu.VMEM((GQA_HEADS, lk // GQA_NCHUNK, GQA_TQ), F32)] * 2,
        compiler_params=_cparams(1),
        name="gqa_lat",
    )(q, k, vt, ctx_k, ctx_vt)


def _ctx_prep_kernel(nk_ref, nv_ref, gk_ref, gv_ref, nk_o, nv_o, gk_o, gvt_o):
    nk_o[...] = nk_ref[...].astype(BF16)
    nv_o[...] = nv_ref[...].astype(BF16)
    gk_o[...] = gk_ref[...].astype(BF16)
    gvt = gv_ref[...].T.astype(BF16)
    ones = jnp.ones((VT_ROWS - HEAD_DIM, gvt.shape[1]), BF16)
    for kv in range(GQA_KV_HEADS):
        gvt_o[kv * VT_ROWS:kv * VT_ROWS + HEAD_DIM, :] = gvt[kv * HEAD_DIM:(kv + 1) * HEAD_DIM]
        gvt_o[kv * VT_ROWS + HEAD_DIM:(kv + 1) * VT_ROWS, :] = ones


def _ctx_prep(ck_n, cv_n, ck_g, cv_g):
    B, depth, lc, _ = ck_n.shape
    nvt = GQA_KV_HEADS * VT_ROWS
    blk = lambda r, c: pl.BlockSpec((None, None, r, c), lambda b, l: (b, l, 0, 0))
    return pl.pallas_call(
        _ctx_prep_kernel,
        grid=(B, depth),
        in_specs=[blk(lc, BRANCH_W), blk(lc, BRANCH_W), blk(lc, KV_W), blk(lc, KV_W)],
        out_specs=[blk(lc, BRANCH_W), blk(lc, BRANCH_W), blk(lc, KV_W), blk(nvt, lc)],
        out_shape=[jax.ShapeDtypeStruct((B, depth, lc, BRANCH_W), BF16)] * 2
        + [jax.ShapeDtypeStruct((B, depth, lc, KV_W), BF16), jax.ShapeDtypeStruct((B, depth, nvt, lc), BF16)],
        compiler_params=_cparams(2),
        name="ctx_prep",
    )(ck_n, cv_n, ck_g, cv_g)


OUT_RC = 256


def _outproj_kernel(a_ref, n_ref, c_ref, g_ref, gates_ref, x_ref, gm_ref, w_ref, b_ref, lg_ref, lb_ref, o_ref):
    for r0 in range(0, x_ref.shape[0], OUT_RC):
        rows = slice(r0, r0 + OUT_RC)
        mixed = jnp.concatenate([a_ref[rows, :], n_ref[rows, :] * gates_ref[rows, 0:BRANCH_W],
                                 c_ref[rows, :], g_ref[rows, :] * gates_ref[rows, BRANCH_W:]], axis=1)
        out = _dot(mixed, w_ref[...]) + b_ref[...]
        y = DEEPNORM_ALPHA * x_ref[rows, :] + gm_ref[...] * out
        mu = jnp.mean(y, axis=-1, keepdims=True)
        d = y - mu
        var = jnp.mean(d * d, axis=-1, keepdims=True)
        o_ref[rows, :] = d * jax.lax.rsqrt(var + LN_EPS) * lg_ref[...] + lb_ref[...]


def _outproj(a, n, c, g, gates, x, mod, mod_row0, layer, w_out, b_out, ln_g, ln_b, tm):
    B, L, D = x.shape
    tok = lambda w: pl.BlockSpec((None, tm, w), lambda i, b: (b, i, 0))
    per_layer = lambda *shape: _layer_block(shape, layer)
    return pl.pallas_call(
        _outproj_kernel,
        grid=(L // tm, B),
        in_specs=[tok(BRANCH_W)] * 4 + [
            tok(2 * BRANCH_W), tok(D),
            pl.BlockSpec((None, None, 1, D), lambda i, b: (layer, mod_row0 + b, 0, 2)),
            per_layer(D, D), per_layer(1, D), per_layer(1, D), per_layer(1, D)],
        out_specs=tok(D),
        out_shape=jax.ShapeDtypeStruct((B, L, D), F32),
        compiler_params=_cparams(2),
        name="outproj",
    )(a, n, c, g, gates, x, mod, w_out, b_out, ln_g, ln_b)


_GQA_HEAD_ORDER = (0, 2, 1, 3)


def _rope_tables(seq_len):
    half = HEAD_DIM // 2
    nf = half // 2
    t = jnp.arange(seq_len)
    inv = ROPE_THETA ** (-jnp.arange(nf, dtype=F32) * 2.0 / half)

    def tabs(pos):
        ang = pos.astype(F32)[:, None] * inv[None, :]
        return jnp.cos(ang), jnp.sin(ang)

    cr, sr = tabs(t // GRID_W)
    cc, sc = tabs(t % GRID_W)
    zero = jnp.zeros_like(sr)
    cos = jnp.concatenate([cr, cr, cc, cc], axis=1)
    sin_a = jnp.concatenate([-sr, zero, -sc, zero], axis=1)
    sin_b = jnp.concatenate([zero, sr, zero, sc], axis=1)
    return tuple(jnp.tile(a, (1, 2)) for a in (cos, sin_a, sin_b))


def _nat_bias_tiles(bias_tab):
    col = np.arange(GRID_W)
    col_start = np.clip(col - NAT_COLS // 2, 0, GRID_W - NAT_COLS)
    col_in = (col[None, :] >= col_start[:, None]) & (col[None, :] < col_start[:, None] + NAT_COLS)
    edge = GRID_W - NAT_COLS
    n_diff = 2 * GRID_W - 1
    padded = jnp.concatenate([jnp.repeat(bias_tab[..., :1], edge, axis=-1), bias_tab,
                              jnp.repeat(bias_tab[..., -1:], edge, axis=-1)], axis=-1)
    flat = jnp.tile(padded, (1, 1, 1, GRID_W + 1))
    toeplitz = flat[..., GRID_W - 1:GRID_W - 1 + GRID_W * (n_diff - 1)].reshape(
        bias_tab.shape[:-1] + (GRID_W, n_diff - 1))[..., :GRID_W]
    full = jnp.where(col_in, toeplitz * LOG2E, NEG_INF)
    pair = jnp.concatenate([full[:, :, :-1], full[:, :, 1:]], axis=-1)
    return jnp.transpose(pair, (0, 2, 1, 3, 4)).reshape(-1, 2 * NAT_ROWS - 2, NAT_HEADS * GRID_W, 2 * GRID_W)


def kernel(x_prompt, x_sample, c, cache_nat_k, cache_nat_v, cache_gqa_k, cache_gqa_v, c_ctx, w_mod, b_mod, w_in, b_in,
           pool_w, pool_scale, nat_bias, q_norm, k_norm, conv_w, conv_b, conv_ln_g, conv_ln_b, conv_pw, w_out, b_out,
           ln_g, ln_b):
    nb, seq, D = x_prompt.shape
    db, dseq, _ = x_sample.shape
    lc = cache_nat_k.shape[2]

    cond = jnp.zeros((MOD_ROWS, D), F32).at[0].set(c_ctx).at[1:1 + db].set(c)
    mod = _modulation(cond, w_mod, b_mod).reshape(DEPTH, MOD_ROWS, 1, 3 * D)

    def kernel_columns(a):
        return jnp.concatenate([a[..., c0:c0 + w] for _, c0, w in _IN_SEGMENTS], axis=-1)

    w_in_b = kernel_columns(w_in).astype(BF16)
    b_in_p = kernel_columns(b_in).reshape(DEPTH, 1, IN_WIDTH)
    w_out_b = w_out.astype(BF16)

    seg = jnp.asarray(np.kron(np.eye(GQA_HEADS), np.ones((HEAD_DIM, HEAD_DIM))), BF16)
    rope_tabs = _rope_tables(dseq)
    ck_n, cv_n, ck_g, cvt_g = _ctx_prep(cache_nat_k.reshape(db, DEPTH, lc, BRANCH_W),
                                        cache_nat_v.reshape(db, DEPTH, lc, BRANCH_W),
                                        cache_gqa_k.reshape(db, DEPTH, lc, KV_W),
                                        cache_gqa_v.reshape(db, DEPTH, lc, KV_W))

    row = lambda a: a.reshape(DEPTH, 1, -1)
    n_groups = len(POOL_WINDOWS)
    pool_bd = jnp.einsum('gh,lgcd->lgchd', jnp.eye(n_groups, dtype=F32), pool_w).reshape(DEPTH, BRANCH_W, BRANCH_W)
    in_w = (w_in_b, b_in_p, row(jnp.tile(q_norm, (1, GQA_HEADS))), row(jnp.tile(k_norm, (1, GQA_KV_HEADS))), seg)
    local_w = (pool_bd.astype(BF16), row(pool_scale), conv_w, row(conv_b), row(conv_ln_g), row(conv_ln_b),
               conv_pw.astype(BF16))
    out_w = (w_out_b, row(b_out), row(ln_g), row(ln_b))
    bias_tiles = _nat_bias_tiles(nat_bias)

    y_p = x_prompt.reshape(1, nb * seq, D)
    y_s = x_sample
    new_cache = [jnp.zeros((nb, DEPTH, seq, w), F32) for w in (BRANCH_W, BRANCH_W, KV_W, KV_W)]
    for l in range(DEPTH):
        a_out, c_out, gates, nat, gqa, *new_cache = _inproj(y_p, mod, 0, l, *in_w, local_w, None, new_cache, seq)
        n_out, g_out = _attn_ctx(nat.reshape(nb, seq, -1), gqa.reshape(nb, seq, -1))
        flat = lambda a: a.reshape(1, nb * seq, BRANCH_W)
        y_p = _outproj(a_out, flat(n_out), c_out, flat(g_out), gates, y_p, mod, 0, l, *out_w, 512)

        a_out, c_out, gates, nat, gq, gk, gvt = _inproj(y_s, mod, 1, l, *in_w, local_w, rope_tabs, None, 512)
        n_out = _nat_lat(nat, ck_n, cv_n, l, bias_tiles)
        g_out = _gqa_lat(gq, gk, gvt, ck_g, cvt_g, l)
        y_s = _outproj(a_out, n_out, c_out, g_out, gates, y_s, mod, 1, l, *out_w, 512)

    heads = (NAT_HEADS, NAT_HEADS, GQA_KV_HEADS, GQA_KV_HEADS)
    return (y_p.reshape(nb, seq, D), y_s) + tuple(c.reshape(nb, DEPTH, seq, h, HEAD_DIM)
                                                  for c, h in zip(new_cache, heads))
```

```python
import functools

import numpy as np
import jax
import jax.numpy as jnp
from jax.experimental import pallas as pl
from jax.experimental.pallas import tpu as pltpu

F32 = jnp.float32
BF16 = jnp.bfloat16

D_MODEL = 1024
DEPTH = 4
GRID_W = 64
HEAD_DIM = 64
BRANCH_W = D_MODEL // 4
POOL_WINDOWS = (2, 4, 8, 16)
POOL_GROUP_W = BRANCH_W // len(POOL_WINDOWS)
NAT_HEADS = BRANCH_W // HEAD_DIM
NAT_ROWS = 8
NAT_COLS = 16
CONV_WIDTH = 31
GQA_HEADS = BRANCH_W // HEAD_DIM
GQA_KV_HEADS = GQA_HEADS // 2
KV_W = GQA_KV_HEADS * HEAD_DIM
ROPE_THETA = 10000.0
LN_EPS = 1e-5
RMS_EPS = 1e-6
NEG_INF = -1e30
DEEPNORM_ALPHA = (2 * DEPTH) ** 0.25
LOG2E = 1.4426950408889634
Q_SCALE = HEAD_DIM ** -0.5 * LOG2E
VT_ROWS = 80

_GQA_HEAD_ORDER = (0, 2, 1, 3)
_IN_SEGMENTS = (
    [("gq%d" % h, 2304 + h * HEAD_DIM, HEAD_DIM) for h in _GQA_HEAD_ORDER]
    + [("gk", 2560, 128), ("gv", 2688, 128), ("ax", 0, 256), ("cv", 1536, 256), ("cglu", 1792, 256),
       ("ag", 256, 256), ("ng", 1280, 256), ("cg", 2048, 256), ("gg", 2816, 256),
       ("nq", 512, 256), ("nk", 768, 256), ("nv", 1024, 256)])
IN_WIDTH = sum(w for _, _, w in _IN_SEGMENTS)
_IN_OFFSET = dict(zip((n for n, _, _ in _IN_SEGMENTS), np.cumsum([0] + [w for _, _, w in _IN_SEGMENTS])[:-1].tolist()))
C_GQ, C_GK, C_GV = _IN_OFFSET["gq0"], _IN_OFFSET["gk"], _IN_OFFSET["gv"]
C_AX, C_CV, C_CGLU = _IN_OFFSET["ax"], _IN_OFFSET["cv"], _IN_OFFSET["cglu"]
C_AG, C_NG, C_CG, C_GG = _IN_OFFSET["ag"], _IN_OFFSET["ng"], _IN_OFFSET["cg"], _IN_OFFSET["gg"]
C_NQ, C_NK, C_NV = _IN_OFFSET["nq"], _IN_OFFSET["nk"], _IN_OFFSET["nv"]

PROJ_TM = 512
HALO = 16
VMEM_LIMIT = 56 * 1024 * 1024
MOD_ROWS = 16


def _cparams(n_axes):
    return pltpu.CompilerParams(dimension_semantics=("arbitrary",) * n_axes, vmem_limit_bytes=VMEM_LIMIT)


def _layer_block(shape, layer):
    return pl.BlockSpec((None,) + tuple(shape), lambda *_: (layer,) + (0,) * len(shape))


def _silu(x):
    return x * jax.nn.sigmoid(x)


def _dot(a, b):
    return jnp.dot(a, b, preferred_element_type=F32)


def _dot_t(a, b):
    return jax.lax.dot_general(a, b, (((1,), (1,)), ((), ())), preferred_element_type=F32)


def _mod_kernel(cond_ref, w_ref, b_ref, o_ref):
    a = _silu(cond_ref[...]).astype(BF16)
    o_ref[...] = _dot(a, w_ref[...].astype(BF16)) + b_ref[...]


def _modulation(cond, w_mod, b_mod):
    nj = 3 * D_MODEL // 1024
    return pl.pallas_call(
        _mod_kernel,
        grid=(DEPTH, nj),
        in_specs=[
            pl.BlockSpec((MOD_ROWS, D_MODEL), lambda l, j: (0, 0)),
            pl.BlockSpec((None, D_MODEL, 1024), lambda l, j: (l, 0, j)),
            pl.BlockSpec((None, 1, 1024), lambda l, j: (l, 0, j)),
        ],
        out_specs=pl.BlockSpec((None, MOD_ROWS, 1024), lambda l, j: (l, 0, j)),
        out_shape=jax.ShapeDtypeStruct((DEPTH, MOD_ROWS, 3 * D_MODEL), F32),
        compiler_params=_cparams(2),
        name="modulation",
    )(cond, w_mod, b_mod.reshape(DEPTH, 1, 3 * D_MODEL))


def _head_rms(x, seg, gain):
    tot = _dot((x * x).astype(BF16), seg)
    return x * jax.lax.rsqrt(tot * (1.0 / HEAD_DIM) + RMS_EPS) * gain


def _rope(x, cos, sin_a, sin_b):
    n = x.shape[-1]
    return x * cos + pltpu.roll(x, n - 16, 1) * sin_a + pltpu.roll(x, 16, 1) * sin_b


def _inproj_kernel(*refs, latent, seq_tiles, seq_len):
    (x_ref, xp_ref, xn_ref, sh_ref, sc_ref, w_ref, b_ref, qn_ref, kn_ref, seg_ref,
     pw_ref, ps_ref, cw_ref, cb_ref, lg_ref, lb_ref, cpw_ref) = refs[:17]
    refs = refs[17:]
    if latent:
        cos_ref, sa_ref, sb_ref = refs[:3]
        refs = refs[3:]
    else:
        refs = refs[4:]
    a_out_ref, c_out_ref, gates_ref, nat_ref, gqa_ref = refs[:5]
    refs = refs[5:]
    abuf, zbuf, ybuf, cbuf, ubuf = refs[-5:]
    refs = refs[:-5]
    tm = x_ref.shape[0]
    tile = pl.program_id(0) % seq_tiles

    def put_cache(ref, val):
        seq = ref.shape[1]
        for sidx in range(ref.shape[0]):
            ref[sidx] = val[sidx * seq:(sidx + 1) * seq]

    u_ref = refs[-1]
    refs = refs[:-1]

    def modulate(x):
        return (x * (1.0 + sc_ref[...]) + sh_ref[...]).astype(BF16)

    h = modulate(x_ref[...])
    h_ext = jnp.concatenate([modulate(xp_ref[...]), h, modulate(xn_ref[...])], axis=0)

    def proj(c0, n):
        return u_ref[:, c0:c0 + n] + b_ref[:, c0:c0 + n]

    loc = _dot(h_ext, w_ref[:, C_AX:C_AG]) + b_ref[:, C_AX:C_AG]
    row = jax.lax.broadcasted_iota(jnp.int32, (tm + 2 * HALO, 1), 0)
    inside = ((row >= HALO) | (tile > 0)) & ((row < HALO + tm) | (tile < seq_tiles - 1))
    abuf[...] = jnp.where(inside, loc[:, 0:BRANCH_W], 0.0)
    zbuf[...] = jnp.where(inside, loc[:, BRANCH_W:2 * BRANCH_W] * jax.nn.sigmoid(loc[:, 2 * BRANCH_W:]), 0.0)
    pieces = _local_mixers(abuf, zbuf, ybuf, cbuf, ubuf, pw_ref, ps_ref, cw_ref, cb_ref, lg_ref, lb_ref, cpw_ref,
                           tile * tm, seq_len)
    result = []

    def advance(n):
        for _ in range(n):
            if not result:
                try:
                    next(pieces)
                except StopIteration as done:
                    result.append(done.value)

    u_ref[:, :C_AX] = _dot(h, w_ref[:, :C_AX])
    advance(8)
    u_ref[:, C_AG:] = _dot(h, w_ref[:, C_AG:])

    gq = _head_rms(proj(C_GQ, BRANCH_W), seg_ref[...], qn_ref[...])
    gk = _head_rms(proj(C_GK, KV_W), seg_ref[0:KV_W, 0:KV_W], kn_ref[...])
    gv = proj(C_GV, KV_W)
    if not latent:
        nk_ref, nv_ref, gk_ref, gv_ref = refs
        put_cache(gk_ref, gk)
        put_cache(gv_ref, gv)
        gqa_ref[:, 256:384] = gk.astype(BF16)
        gqa_ref[:, 384:512] = gv.astype(BF16)
    else:
        cos, sa, sb = cos_ref[...], sa_ref[...], sb_ref[...]
        gk = _rope(gk, cos, sa, sb)
        gq = _rope(gq, jnp.concatenate([cos, cos], axis=1), jnp.concatenate([sa, sa], axis=1),
                   jnp.concatenate([sb, sb], axis=1))
        k_ref, vt_ref = refs
        k_ref[...] = gk.astype(BF16)
        gvt = gv.T.astype(BF16)
        ones = jnp.ones((VT_ROWS - HEAD_DIM, gvt.shape[1]), BF16)
        for kv in range(GQA_KV_HEADS):
            vt_ref[kv * VT_ROWS:kv * VT_ROWS + HEAD_DIM, :] = gvt[kv * HEAD_DIM:(kv + 1) * HEAD_DIM]
            vt_ref[kv * VT_ROWS + HEAD_DIM:(kv + 1) * VT_ROWS, :] = ones
    gqa_ref[:, 0:256] = (gq * Q_SCALE).astype(BF16)

    while not result:
        advance(1)
    a, c = result[0]
    a_out_ref[...] = (a * _silu(proj(C_AG, BRANCH_W))).astype(BF16)
    c_out_ref[...] = (c * _silu(proj(C_CG, BRANCH_W))).astype(BF16)
    gates_ref[:, 0:BRANCH_W] = _silu(proj(C_NG, BRANCH_W)).astype(BF16)
    gates_ref[:, BRANCH_W:] = _silu(proj(C_GG, BRANCH_W)).astype(BF16)

    nat_ref[:, 0:256] = (proj(C_NQ, BRANCH_W) * Q_SCALE).astype(BF16)
    nk = proj(C_NK, BRANCH_W)
    nv = proj(C_NV, BRANCH_W)
    nat_ref[:, 256:512] = nk.astype(BF16)
    nat_ref[:, 512:768] = nv.astype(BF16)
    if not latent:
        put_cache(nk_ref, nk)
        put_cache(nv_ref, nv)


def _inproj(x, mod, mod_row0, layer, w_in, b_in, qn, kn, seg, local_w, rope_tabs, new_cache, tm):
    B, L, D = x.shape
    nt = L // tm
    latent = rope_tabs is not None
    seq_tiles = nt if latent else 1
    hb = tm // HALO
    nhb = L // HALO
    tok = lambda w: pl.BlockSpec((None, tm, w), lambda i, b: (b, i, 0))
    per_layer = lambda *shape: _layer_block(shape, layer)
    C = BRANCH_W
    in_specs = [
        tok(D),
        pl.BlockSpec((None, HALO, D), lambda i, b: (b, jnp.maximum(i * hb - 1, 0), 0)),
        pl.BlockSpec((None, HALO, D), lambda i, b: (b, jnp.minimum((i + 1) * hb, nhb - 1), 0)),
        pl.BlockSpec((None, None, 1, D), lambda i, b: (layer, mod_row0 + b, 0, 0)),
        pl.BlockSpec((None, None, 1, D), lambda i, b: (layer, mod_row0 + b, 0, 1)),
        per_layer(D, IN_WIDTH), per_layer(1, IN_WIDTH), per_layer(1, BRANCH_W), per_layer(1, KV_W),
        pl.BlockSpec((BRANCH_W, BRANCH_W), lambda i, b: (0, 0)),
        per_layer(C, C), per_layer(1, C), per_layer(CONV_WIDTH, C), per_layer(1, C), per_layer(1, C), per_layer(1, C),
        per_layer(C, C),
    ]
    args = [x, x, x, mod, mod, w_in, b_in, qn, kn, seg, *local_w]
    if latent:
        in_specs += [pl.BlockSpec((tm, KV_W), lambda i, b: (i, 0))] * 3
        args += list(rope_tabs)
    gqa_w = BRANCH_W if latent else BRANCH_W + 2 * KV_W
    out_specs = [tok(BRANCH_W), tok(BRANCH_W), tok(2 * BRANCH_W), tok(3 * BRANCH_W), tok(gqa_w)]
    out_shape = [jax.ShapeDtypeStruct((B, L, BRANCH_W), BF16), jax.ShapeDtypeStruct((B, L, BRANCH_W), BF16),
                 jax.ShapeDtypeStruct((B, L, 2 * BRANCH_W), BF16), jax.ShapeDtypeStruct((B, L, 3 * BRANCH_W), BF16),
                 jax.ShapeDtypeStruct((B, L, gqa_w), BF16)]
    if latent:
        out_specs += [tok(KV_W), pl.BlockSpec((None, GQA_KV_HEADS * VT_ROWS, tm), lambda i, b: (b, 0, i))]
        out_shape += [jax.ShapeDtypeStruct((B, L, KV_W), BF16),
                      jax.ShapeDtypeStruct((B, GQA_KV_HEADS * VT_ROWS, L), BF16)]
        aliases = {}
    else:
        seq = new_cache[0].shape[2]
        assert B == 1 and tm == seq
        aliases = {len(args) + j: len(out_specs) + j for j in range(len(new_cache))}
        in_specs += [pl.BlockSpec(memory_space=pl.ANY)] * len(new_cache)
        args += list(new_cache)
        out_specs += [pl.BlockSpec((1, None, seq, c.shape[3]), lambda i, b: (i, layer, 0, 0)) for c in new_cache]
        out_shape += [jax.ShapeDtypeStruct(c.shape, c.dtype) for c in new_cache]
    return pl.pallas_call(
        functools.partial(_inproj_kernel, latent=latent, seq_tiles=seq_tiles, seq_len=L if latent else tm),
        grid=(nt, B),
        in_specs=in_specs,
        out_specs=out_specs,
        out_shape=out_shape,
        input_output_aliases=aliases,
        scratch_shapes=[pltpu.VMEM((tm, IN_WIDTH), F32),
                        pltpu.VMEM((tm + 2 * HALO, C), F32), pltpu.VMEM((tm + 2 * HALO, C), F32),
                        pltpu.VMEM((tm, C), F32), pltpu.VMEM((tm, C), F32), pltpu.VMEM((8, tm + 8, C), F32)],
        compiler_params=_cparams(2),
        name="inproj_lat" if latent else "inproj_ctx",
    )(*args)


LOCAL_RC = 64


def _conv_chunk(rows):
    return max(c for c in range(8, 105, 8) if rows % c == 0)


def _local_mixers(abuf, zbuf, ybuf, cbuf, ubuf, pw_ref, ps_ref, cw_ref, cb_ref, lg_ref, lb_ref, cpw_ref, t0, seq_len):
    tm = ybuf.shape[0]
    LOCAL_UC = _conv_chunk(tm + 8)

    half_w = CONV_WIDTH // 2
    for res in range(8):
        taps = [8 * a + res for a in range(-2, 2) if -half_w <= 8 * a + res <= half_w]
        for u0 in range(0, tm + 8, LOCAL_UC):
            part = None
            for d in taps:
                term = zbuf[HALO + u0 + d - res:HALO + u0 + d - res + LOCAL_UC, :] * cw_ref[d + half_w:d + half_w + 1, :]
                part = term if part is None else part + term
            ubuf[res, u0:u0 + LOCAL_UC, :] = part
        yield

    lane = jax.lax.broadcasted_iota(jnp.int32, (LOCAL_RC, 128), 1)
    low = lane < POOL_GROUP_W
    for r0 in range(0, tm, LOCAL_RC):
        t = t0 + r0 + jax.lax.broadcasted_iota(jnp.int32, (LOCAL_RC, 128), 0)

        def sh(off, c0):
            return abuf[HALO + r0 + off:HALO + r0 + off + LOCAL_RC, c0:c0 + 128]

        def centred(total, half, x):
            cnt = jnp.minimum(t + half, seq_len) - jnp.maximum(t - half, 0)
            return total / cnt.astype(F32) - x

        x0 = sh(0, 0)
        s2 = x0 + sh(-1, 0)
        s4 = s2 + sh(-2, 0) + sh(1, 0)
        ybuf[r0:r0 + LOCAL_RC, 0:128] = centred(jnp.where(low, s2, s4), jnp.where(low, 1, 2), x0)
        x1 = sh(0, 128)
        s8 = x1
        for off in (-4, -3, -2, -1, 1, 2, 3):
            s8 = s8 + sh(off, 128)
        s16 = s8
        for off in (-8, -7, -6, -5, 4, 5, 6, 7):
            s16 = s16 + sh(off, 128)
        ybuf[r0:r0 + LOCAL_RC, 128:256] = centred(jnp.where(low, s8, s16), jnp.where(low, 4, 8), x1)

        acc = jnp.zeros((LOCAL_RC, BRANCH_W), F32) + cb_ref[...]
        for res in range(8):
            acc = acc + ubuf[res, r0 + res:r0 + res + LOCAL_RC, :]
        cbuf[r0:r0 + LOCAL_RC, :] = acc
        yield

    a = _dot(ybuf[...].astype(BF16), pw_ref[...]) * ps_ref[...]

    cz = cbuf[...]
    mu = jnp.mean(cz, axis=-1, keepdims=True)
    d = cz - mu
    var = jnp.mean(d * d, axis=-1, keepdims=True)
    zn = _silu(d * jax.lax.rsqrt(var + LN_EPS) * lg_ref[...] + lb_ref[...])
    return a, _dot(zn.astype(BF16), cpw_ref[...])


def _head_mask(shape, h):
    lane = jax.lax.broadcasted_iota(jnp.int32, shape, 1)
    return (lane >= h * HEAD_DIM) & (lane < (h + 1) * HEAD_DIM)


def _softmax_pv(s, v):
    m = jnp.max(s, axis=-1, keepdims=True)
    p = jnp.exp2(s - m)
    l = jnp.sum(p, axis=-1, keepdims=True)
    return _dot(p.astype(BF16), v) / l


def _attn_ctx_kernel(nat_ref, gqa_ref, n_out_ref, g_out_ref):
    q, k, v = nat_ref[:, 0:256], nat_ref[:, 256:512], nat_ref[:, 512:768]
    acc = jnp.zeros(q.shape, F32)
    for h in range(NAT_HEADS):
        msk = _head_mask(q.shape, h)
        o = _softmax_pv(_dot_t(jnp.where(msk, q, jnp.zeros_like(q)), k), v)
        acc = acc + jnp.where(msk, o, 0.0)
    n_out_ref[...] = acc.astype(BF16)

    k, v = gqa_ref[:, 256:384], gqa_ref[:, 384:512]
    sides = []
    for side in range(2):
        q = gqa_ref[:, side * 128:(side + 1) * 128]
        acc = jnp.zeros(q.shape, F32)
        for kv in range(GQA_KV_HEADS):
            msk = _head_mask(q.shape, kv)
            o = _softmax_pv(_dot_t(jnp.where(msk, q, jnp.zeros_like(q)), k), v)
            acc = acc + jnp.where(msk, o, 0.0)
        sides.append(acc)
    low = _head_mask(sides[0].shape, 0)
    g_out_ref[:, 0:128] = jnp.where(low, sides[0], pltpu.roll(sides[1], HEAD_DIM, 1)).astype(BF16)
    g_out_ref[:, 128:256] = jnp.where(low, pltpu.roll(sides[0], HEAD_DIM, 1), sides[1]).astype(BF16)


def _attn_ctx(nat, gqa):
    B, L, _ = nat.shape
    return pl.pallas_call(
        _attn_ctx_kernel,
        grid=(B,),
        in_specs=[pl.BlockSpec((None, L, 3 * BRANCH_W), lambda b: (b, 0, 0)),
                  pl.BlockSpec((None, L, BRANCH_W + 2 * KV_W), lambda b: (b, 0, 0))],
        out_specs=[pl.BlockSpec((None, L, BRANCH_W), lambda b: (b, 0, 0))] * 2,
        out_shape=[jax.ShapeDtypeStruct((B, L, BRANCH_W), BF16)] * 2,
        compiler_params=_cparams(1),
        name="attn_ctx",
    )(nat, gqa)


NAT_RB = 8
NAT_NLOC = NAT_ROWS * GRID_W


def _nat_kernel(q_ref, k_ref, v_ref, kc_ref, vc_ref, bias_ref, o_ref, s_a, s_b, *, rows):
    i = pl.program_id(1)
    kc = kc_ref[...]
    vc = vc_ref[...]
    nq = NAT_HEADS * GRID_W
    diag = (jax.lax.broadcasted_iota(jnp.int32, (nq, BRANCH_W), 0) // GRID_W
            == jax.lax.broadcasted_iota(jnp.int32, (nq, BRANCH_W), 1) // HEAD_DIM)
    bufs = (s_a, s_b)

    def window(rr):
        r = i * NAT_RB + rr
        row_start = jnp.clip(r - NAT_ROWS // 2, 0, rows - NAT_ROWS)
        return row_start - r + NAT_ROWS - 1, pl.multiple_of(row_start * GRID_W, GRID_W)

    def scores(rr, buf):
        off, base = window(rr)
        q_r = q_ref[rr * GRID_W:(rr + 1) * GRID_W, :]
        qs = jnp.where(diag, jnp.concatenate([q_r] * NAT_HEADS, axis=0), jnp.zeros((nq, BRANCH_W), BF16))
        bias = jnp.concatenate([bias_ref[off + 2 * jj] for jj in range(NAT_ROWS // 2)], axis=1)
        buf[:, 0:NAT_NLOC] = _dot_t(qs, k_ref[pl.ds(base, NAT_NLOC), :]) + bias
        buf[:, NAT_NLOC:] = _dot_t(qs, kc)

    scores(0, s_a)
    for rr in range(NAT_RB):
        cur = bufs[rr % 2]
        if rr + 1 < NAT_RB:
            scores(rr + 1, bufs[(rr + 1) % 2])
        _, base = window(rr)
        vw = v_ref[pl.ds(base, NAT_NLOC), :]
        m = jnp.max(cur[...], axis=-1, keepdims=True)
        p_loc = jnp.exp2(cur[:, 0:NAT_NLOC] - m)
        p_ctx = jnp.exp2(cur[:, NAT_NLOC:] - m)
        l = jnp.sum(p_loc, axis=-1, keepdims=True) + jnp.sum(p_ctx, axis=-1, keepdims=True)
        o = (_dot(p_loc.astype(BF16), vw) + _dot(p_ctx.astype(BF16), vc)) / l
        o = jnp.where(diag, o, 0.0)
        o_r = o[0:GRID_W]
        for h in range(1, NAT_HEADS):
            o_r = o_r + o[h * GRID_W:(h + 1) * GRID_W]
        o_ref[rr * GRID_W:(rr + 1) * GRID_W, :] = o_r.astype(BF16)


def _nat_lat(nat, ctx_k, ctx_v, layer, bias_tiles):
    B, L, _ = nat.shape
    rows = L // GRID_W
    lc = ctx_k.shape[2]
    tq = NAT_RB * GRID_W
    seq = lambda col: pl.BlockSpec((None, L, BRANCH_W), lambda b, i: (b, 0, col))
    ctx = pl.BlockSpec((None, None, lc, BRANCH_W), lambda b, i: (b, layer, 0, 0))
    return pl.pallas_call(
        functools.partial(_nat_kernel, rows=rows),
        grid=(B, rows // NAT_RB),
        in_specs=[pl.BlockSpec((None, tq, BRANCH_W), lambda b, i: (b, i, 0)), seq(1), seq(2), ctx, ctx,
                  _layer_block(bias_tiles.shape[1:], layer)],
        out_specs=pl.BlockSpec((None, tq, BRANCH_W), lambda b, i: (b, i, 0)),
        out_shape=jax.ShapeDtypeStruct((B, L, BRANCH_W), BF16),
        scratch_shapes=[pltpu.VMEM((NAT_HEADS * GRID_W, NAT_NLOC + lc), F32)] * 2,
        compiler_params=_cparams(2),
        name="nat_lat",
    )(nat, nat, nat, ctx_k, ctx_v, bias_tiles)


GQA_TQ = 256
GQA_NCHUNK = 6


def _gqa_kernel(q_ref, k_lat_ref, vt_lat_ref, k_ctx_ref, vt_ctx_ref, o_ref, k_ref, vt_ref, s_a, s_b):
    n_lat = k_lat_ref.shape[0]
    k_ref[0:n_lat, :] = k_lat_ref[...]
    k_ref[n_lat:, :] = k_ctx_ref[...]
    vt_ref[:, 0:n_lat] = vt_lat_ref[...]
    vt_ref[:, n_lat:] = vt_ctx_ref[...]
    kc_len = k_ref.shape[0] // GQA_NCHUNK
    nqt = q_ref.shape[0] // GQA_TQ
    heads = [(side, kv) for side in range(2) for kv in range(GQA_KV_HEADS)]
    bufs = (s_a, s_b)

    def masked_q(t):
        row = pl.multiple_of(t * GQA_TQ, GQA_TQ)
        out = []
        for side, kv in heads:
            q = q_ref[pl.ds(row, GQA_TQ), side * 128:(side + 1) * 128]
            out.append(jnp.where(_head_mask(q.shape, kv), q, jnp.zeros_like(q)))
        return out

    qh0 = masked_q(0)
    for h in range(len(heads)):
        s_a[h] = _dot_t(k_ref[0:kc_len, :], qh0[h])

    def qtile(t, carry):
        qh_cur = masked_q(t)
        qh_next = masked_q(jnp.minimum(t + 1, nqt - 1))
        state = [(jnp.full((1, GQA_TQ), NEG_INF, F32), jnp.zeros((VT_ROWS, GQA_TQ), F32)) for _ in heads]
        for c in range(GQA_NCHUNK):
            cur, nxt = bufs[c % 2], bufs[(c + 1) % 2]
            cn = (c + 1) % GQA_NCHUNK
            qn = qh_cur if c + 1 < GQA_NCHUNK else qh_next
            kn = k_ref[cn * kc_len:(cn + 1) * kc_len, :]
            vt = vt_ref[:, c * kc_len:(c + 1) * kc_len]
            for h, (side, kv) in enumerate(heads):
                nxt[h] = _dot_t(kn, qn[h])
                m, acc = state[h]
                s = cur[h]
                m_new = jnp.maximum(m, jnp.max(s, axis=0, keepdims=True))
                p = jnp.exp2(s - m_new).astype(BF16)
                acc = jnp.exp2(m - m_new) * acc + _dot(vt[kv * VT_ROWS:(kv + 1) * VT_ROWS], p)
                state[h] = (m_new, acc)
        outs = [acc[0:HEAD_DIM] / acc[HEAD_DIM:HEAD_DIM + 1] for _, acc in state]
        row = pl.multiple_of(t * GQA_TQ, GQA_TQ)
        for half in range(2):
            o_t = jnp.concatenate([outs[heads.index((side, half))] for side in range(2)], axis=0)
            o_ref[pl.ds(row, GQA_TQ), half * 128:(half + 1) * 128] = o_t.T.astype(BF16)
        return carry

    jax.lax.fori_loop(0, nqt, qtile, 0)


def _gqa_lat(q, k, vt, ctx_k, ctx_vt, layer):
    B, L, _ = q.shape
    lc = ctx_k.shape[2]
    lk = L + lc
    nvt = GQA_KV_HEADS * VT_ROWS
    assert GQA_NCHUNK % 2 == 0 and lk % (GQA_NCHUNK * 128) == 0 and L % GQA_TQ == 0 and L % 128 == 0
    return pl.pallas_call(
        _gqa_kernel,
        grid=(B,),
        in_specs=[pl.BlockSpec((None, L, BRANCH_W), lambda b: (b, 0, 0)),
                  pl.BlockSpec((None, L, KV_W), lambda b: (b, 0, 0)),
                  pl.BlockSpec((None, nvt, L), lambda b: (b, 0, 0)),
                  pl.BlockSpec((None, None, lc, KV_W), lambda b: (b, layer, 0, 0)),
                  pl.BlockSpec((None, None, nvt, lc), lambda b: (b, layer, 0, 0))],
        out_specs=pl.BlockSpec((None, L, BRANCH_W), lambda b: (b, 0, 0)),
        out_shape=jax.ShapeDtypeStruct((B, L, BRANCH_W), BF16),
        scratch_shapes=[pltpu.VMEM((lk, KV_W), BF16), pltpu.VMEM((nvt, lk), BF16)]
        + [pltpu.VMEM((GQA_HEADS, lk // GQA_NCHUNK, GQA_TQ), F32)] * 2,
        compiler_params=_cparams(1),
        name="gqa_lat",
    )(q, k, vt, ctx_k, ctx_vt)


def _ctx_prep_kernel(nk_ref, nv_ref, gk_ref, gv_ref, nk_o, nv_o, gk_o, gvt_o):
    nk_o[...] = nk_ref[...].astype(BF16)
    nv_o[...] = nv_ref[...].astype(BF16)
    gk_o[...] = gk_ref[...].astype(BF16)
    gvt = gv_ref[...].T.astype(BF16)
    ones = jnp.ones((VT_ROWS - HEAD_DIM, gvt.shape[1]), BF16)
    for kv in range(GQA_KV_HEADS):
        gvt_o[kv * VT_ROWS:kv * VT_ROWS + HEAD_DIM, :] = gvt[kv * HEAD_DIM:(kv + 1) * HEAD_DIM]
        gvt_o[kv * VT_ROWS + HEAD_DIM:(kv + 1) * VT_ROWS, :] = ones


def _ctx_prep(ck_n, cv_n, ck_g, cv_g):
    B, depth, lc, _ = ck_n.shape
    nvt = GQA_KV_HEADS * VT_ROWS
    blk = lambda r, c: pl.BlockSpec((None, None, r, c), lambda b, l: (b, l, 0, 0))
    return pl.pallas_call(
        _ctx_prep_kernel,
        grid=(B, depth),
        in_specs=[blk(lc, BRANCH_W), blk(lc, BRANCH_W), blk(lc, KV_W), blk(lc, KV_W)],
        out_specs=[blk(lc, BRANCH_W), blk(lc, BRANCH_W), blk(lc, KV_W), blk(nvt, lc)],
        out_shape=[jax.ShapeDtypeStruct((B, depth, lc, BRANCH_W), BF16)] * 2
        + [jax.ShapeDtypeStruct((B, depth, lc, KV_W), BF16), jax.ShapeDtypeStruct((B, depth, nvt, lc), BF16)],
        compiler_params=_cparams(2),
        name="ctx_prep",
    )(ck_n, cv_n, ck_g, cv_g)


OUT_TM = 1024
OUT_RC = 256


def _outproj_kernel(a_ref, n_ref, c_ref, g_ref, gates_ref, x_ref, gm_ref, w_ref, b_ref, lg_ref, lb_ref, o_ref):
    for r0 in range(0, x_ref.shape[0], OUT_RC):
        rows = slice(r0, r0 + OUT_RC)
        mixed = jnp.concatenate([a_ref[rows, :], n_ref[rows, :] * gates_ref[rows, 0:BRANCH_W],
                                 c_ref[rows, :], g_ref[rows, :] * gates_ref[rows, BRANCH_W:]], axis=1)
        out = _dot(mixed, w_ref[...]) + b_ref[...]
        y = DEEPNORM_ALPHA * x_ref[rows, :] + gm_ref[...] * out
        mu = jnp.mean(y, axis=-1, keepdims=True)
        d = y - mu
        var = jnp.mean(d * d, axis=-1, keepdims=True)
        o_ref[rows, :] = d * jax.lax.rsqrt(var + LN_EPS) * lg_ref[...] + lb_ref[...]


def _outproj(a, n, c, g, gates, x, mod, mod_row0, layer, w_out, b_out, ln_g, ln_b):
    B, L, D = x.shape
    tm = min(OUT_TM, L)
    assert L % tm == 0 and tm % OUT_RC == 0
    tok = lambda w: pl.BlockSpec((None, tm, w), lambda i, b: (b, i, 0))
    per_layer = lambda *shape: _layer_block(shape, layer)
    return pl.pallas_call(
        _outproj_kernel,
        grid=(L // tm, B),
        in_specs=[tok(BRANCH_W)] * 4 + [
            tok(2 * BRANCH_W), tok(D),
            pl.BlockSpec((None, None, 1, D), lambda i, b: (layer, mod_row0 + b, 0, 2)),
            per_layer(D, D), per_layer(1, D), per_layer(1, D), per_layer(1, D)],
        out_specs=tok(D),
        out_shape=jax.ShapeDtypeStruct((B, L, D), F32),
        compiler_params=_cparams(2),
        name="outproj",
    )(a, n, c, g, gates, x, mod, w_out, b_out, ln_g, ln_b)


def _rope_tables(seq_len):
    half = HEAD_DIM // 2
    nf = half // 2
    t = np.arange(seq_len)
    inv = np.float32(ROPE_THETA) ** (-np.arange(nf, dtype=np.float32) * np.float32(2.0) / np.float32(half))

    def tabs(pos):
        ang = pos.astype(np.float32)[:, None] * inv[None, :]
        return np.cos(ang), np.sin(ang)

    cr, sr = tabs(t // GRID_W)
    cc, sc = tabs(t % GRID_W)
    zero = np.zeros_like(sr)
    cos = np.concatenate([cr, cr, cc, cc], axis=1)
    sin_a = np.concatenate([-sr, zero, -sc, zero], axis=1)
    sin_b = np.concatenate([zero, sr, zero, sc], axis=1)
    return tuple(jnp.asarray(np.tile(a, (1, 2)), F32) for a in (cos, sin_a, sin_b))


def _nat_bias_tiles(bias_tab):
    col = np.arange(GRID_W)
    col_start = np.clip(col - NAT_COLS // 2, 0, GRID_W - NAT_COLS)
    col_in = (col[None, :] >= col_start[:, None]) & (col[None, :] < col_start[:, None] + NAT_COLS)
    edge = GRID_W - NAT_COLS
    n_diff = 2 * GRID_W - 1
    padded = jnp.concatenate([jnp.repeat(bias_tab[..., :1], edge, axis=-1), bias_tab,
                              jnp.repeat(bias_tab[..., -1:], edge, axis=-1)], axis=-1)
    flat = jnp.tile(padded, (1, 1, 1, GRID_W + 1))
    toeplitz = flat[..., GRID_W - 1:GRID_W - 1 + GRID_W * (n_diff - 1)].reshape(
        bias_tab.shape[:-1] + (GRID_W, n_diff - 1))[..., :GRID_W]
    full = jnp.where(col_in, toeplitz * LOG2E, NEG_INF)
    pair = jnp.concatenate([full[:, :, :-1], full[:, :, 1:]], axis=-1)
    return jnp.transpose(pair, (0, 2, 1, 3, 4)).reshape(-1, 2 * NAT_ROWS - 2, NAT_HEADS * GRID_W, 2 * GRID_W)


def kernel(x_prompt, x_sample, c, cache_nat_k, cache_nat_v, cache_gqa_k, cache_gqa_v, c_ctx, w_mod, b_mod, w_in, b_in,
           pool_w, pool_scale, nat_bias, q_norm, k_norm, conv_w, conv_b, conv_ln_g, conv_ln_b, conv_pw, w_out, b_out,
           ln_g, ln_b):
    nb, seq, D = x_prompt.shape
    db, dseq, _ = x_sample.shape
    lc = cache_nat_k.shape[2]

    cond = jnp.zeros((MOD_ROWS, D), F32).at[0].set(c_ctx).at[1:1 + db].set(c)
    mod = _modulation(cond, w_mod, b_mod).reshape(DEPTH, MOD_ROWS, 1, 3 * D)

    def kernel_columns(a):
        return jnp.concatenate([a[..., c0:c0 + w] for _, c0, w in _IN_SEGMENTS], axis=-1)

    w_in_b = kernel_columns(w_in).astype(BF16)
    b_in_p = kernel_columns(b_in).reshape(DEPTH, 1, IN_WIDTH)
    w_out_b = w_out.astype(BF16)

    seg = jnp.asarray(np.kron(np.eye(GQA_HEADS), np.ones((HEAD_DIM, HEAD_DIM))), BF16)
    rope_tabs = _rope_tables(dseq)
    ck_n, cv_n, ck_g, cvt_g = _ctx_prep(cache_nat_k.reshape(db, DEPTH, lc, BRANCH_W),
                                        cache_nat_v.reshape(db, DEPTH, lc, BRANCH_W),
                                        cache_gqa_k.reshape(db, DEPTH, lc, KV_W),
                                        cache_gqa_v.reshape(db, DEPTH, lc, KV_W))

    row = lambda a: a.reshape(DEPTH, 1, -1)
    n_groups = len(POOL_WINDOWS)
    pool_bd = jnp.einsum('gh,lgcd->lgchd', jnp.eye(n_groups, dtype=F32), pool_w).reshape(DEPTH, BRANCH_W, BRANCH_W)
    in_w = (w_in_b, b_in_p, row(jnp.tile(q_norm, (1, GQA_HEADS))), row(jnp.tile(k_norm, (1, GQA_KV_HEADS))), seg)
    local_w = (pool_bd.astype(BF16), row(pool_scale), conv_w, row(conv_b), row(conv_ln_g), row(conv_ln_b),
               conv_pw.astype(BF16))
    out_w = (w_out_b, row(b_out), row(ln_g), row(ln_b))
    bias_tiles = _nat_bias_tiles(nat_bias)

    y_p = x_prompt.reshape(1, nb * seq, D)
    y_s = x_sample
    new_cache = [jnp.zeros((nb, DEPTH, seq, w), F32) for w in (BRANCH_W, BRANCH_W, KV_W, KV_W)]
    for l in range(DEPTH):
        a_out, c_out, gates, nat, gqa, *new_cache = _inproj(y_p, mod, 0, l, *in_w, local_w, None, new_cache, seq)
        n_out, g_out = _attn_ctx(nat.reshape(nb, seq, -1), gqa.reshape(nb, seq, -1))
        flat = lambda a: a.reshape(1, nb * seq, BRANCH_W)
        y_p = _outproj(a_out, flat(n_out), c_out, flat(g_out), gates, y_p, mod, 0, l, *out_w)

        a_out, c_out, gates, nat, gq, gk, gvt = _inproj(y_s, mod, 1, l, *in_w, local_w, rope_tabs, None, PROJ_TM)
        n_out = _nat_lat(nat, ck_n, cv_n, l, bias_tiles)
        g_out = _gqa_lat(gq, gk, gvt, ck_g, cvt_g, l)
        y_s = _outproj(a_out, n_out, c_out, g_out, gates, y_s, mod, 1, l, *out_w)

    heads = (NAT_HEADS, NAT_HEADS, GQA_KV_HEADS, GQA_KV_HEADS)
    return (y_p.reshape(nb, seq, D), y_s) + tuple(c.reshape(nb, DEPTH, seq, h, HEAD_DIM)
                                                  for c, h in zip(new_cache, heads))
```

```python
import functools

import numpy as np
import jax
import jax.numpy as jnp
from jax.experimental import pallas as pl
from jax.experimental.pallas import tpu as pltpu

F32 = jnp.float32
BF16 = jnp.bfloat16

D_MODEL = 1024
DEPTH = 4
GRID_W = 64
HEAD_DIM = 64
BRANCH_W = D_MODEL // 4
POOL_WINDOWS = (2, 4, 8, 16)
POOL_GROUP_W = BRANCH_W // len(POOL_WINDOWS)
NAT_HEADS = BRANCH_W // HEAD_DIM
NAT_ROWS = 8
NAT_COLS = 16
CONV_WIDTH = 31
GQA_HEADS = BRANCH_W // HEAD_DIM
GQA_KV_HEADS = GQA_HEADS // 2
KV_W = GQA_KV_HEADS * HEAD_DIM
ROPE_THETA = 10000.0
LN_EPS = 1e-5
RMS_EPS = 1e-6
NEG_INF = -1e30
DEEPNORM_ALPHA = (2 * DEPTH) ** 0.25
LOG2E = 1.4426950408889634
Q_SCALE = HEAD_DIM ** -0.5 * LOG2E
VT_ROWS = 80

_GQA_HEAD_ORDER = (0, 2, 1, 3)
_IN_SEGMENTS = (
    [("gq%d" % h, 2304 + h * HEAD_DIM, HEAD_DIM) for h in _GQA_HEAD_ORDER]
    + [("gk", 2560, 128), ("gv", 2688, 128), ("ax", 0, 256), ("cv", 1536, 256), ("cglu", 1792, 256),
       ("ag", 256, 256), ("ng", 1280, 256), ("cg", 2048, 256), ("gg", 2816, 256),
       ("nq", 512, 256), ("nk", 768, 256), ("nv", 1024, 256)])
IN_WIDTH = sum(w for _, _, w in _IN_SEGMENTS)
_IN_OFFSET = dict(zip((n for n, _, _ in _IN_SEGMENTS), np.cumsum([0] + [w for _, _, w in _IN_SEGMENTS])[:-1].tolist()))
C_GQ, C_GK, C_GV = _IN_OFFSET["gq0"], _IN_OFFSET["gk"], _IN_OFFSET["gv"]
C_AX, C_CV, C_CGLU = _IN_OFFSET["ax"], _IN_OFFSET["cv"], _IN_OFFSET["cglu"]
C_AG, C_NG, C_CG, C_GG = _IN_OFFSET["ag"], _IN_OFFSET["ng"], _IN_OFFSET["cg"], _IN_OFFSET["gg"]
C_NQ, C_NK, C_NV = _IN_OFFSET["nq"], _IN_OFFSET["nk"], _IN_OFFSET["nv"]

PROJ_TM = 512
HALO = 16
VMEM_LIMIT = 56 * 1024 * 1024
MOD_ROWS = 16


def _cparams(n_axes):
    return pltpu.CompilerParams(dimension_semantics=("arbitrary",) * n_axes, vmem_limit_bytes=VMEM_LIMIT)


def _layer_block(shape, layer):
    return pl.BlockSpec((None,) + tuple(shape), lambda *_: (layer,) + (0,) * len(shape))


def _silu(x):
    return x * jax.nn.sigmoid(x)


def _dot(a, b):
    return jnp.dot(a, b, preferred_element_type=F32)


def _dot_t(a, b):
    return jax.lax.dot_general(a, b, (((1,), (1,)), ((), ())), preferred_element_type=F32)


def _mod_kernel(cond_ref, w_ref, b_ref, o_ref):
    a = _silu(cond_ref[...]).astype(BF16)
    o_ref[...] = _dot(a, w_ref[...].astype(BF16)) + b_ref[...]


def _modulation(cond, w_mod, b_mod):
    nj = 3 * D_MODEL // 1024
    return pl.pallas_call(
        _mod_kernel,
        grid=(DEPTH, nj),
        in_specs=[
            pl.BlockSpec((MOD_ROWS, D_MODEL), lambda l, j: (0, 0)),
            pl.BlockSpec((None, D_MODEL, 1024), lambda l, j: (l, 0, j)),
            pl.BlockSpec((None, 1, 1024), lambda l, j: (l, 0, j)),
        ],
        out_specs=pl.BlockSpec((None, MOD_ROWS, 1024), lambda l, j: (l, 0, j)),
        out_shape=jax.ShapeDtypeStruct((DEPTH, MOD_ROWS, 3 * D_MODEL), F32),
        compiler_params=_cparams(2),
        name="modulation",
    )(cond, w_mod, b_mod.reshape(DEPTH, 1, 3 * D_MODEL))


def _head_rms(x, seg, gain):
    tot = _dot((x * x).astype(BF16), seg)
    return x * jax.lax.rsqrt(tot * (1.0 / HEAD_DIM) + RMS_EPS) * gain


def _rope(x, cos, sin_a, sin_b):
    n = x.shape[-1]
    return x * cos + pltpu.roll(x, n - 16, 1) * sin_a + pltpu.roll(x, 16, 1) * sin_b


def _inproj_kernel(*refs, latent, seq_tiles, seq_len):
    (x_ref, xp_ref, xn_ref, sh_ref, sc_ref, w_ref, b_ref, qn_ref, kn_ref, seg_ref,
     pw_ref, ps_ref, cw_ref, cb_ref, lg_ref, lb_ref, cpw_ref) = refs[:17]
    refs = refs[17:]
    if latent:
        cos_ref, sa_ref, sb_ref = refs[:3]
        refs = refs[3:]
    else:
        refs = refs[4:]
    a_out_ref, c_out_ref, gates_ref, nat_ref, gqa_ref = refs[:5]
    refs = refs[5:]
    abuf, zbuf, ybuf, cbuf, ubuf = refs[-5:]
    refs = refs[:-5]
    tm = x_ref.shape[0]
    tile = pl.program_id(0) % seq_tiles

    def put_cache(ref, val):
        seq = ref.shape[1]
        for sidx in range(ref.shape[0]):
            ref[sidx] = val[sidx * seq:(sidx + 1) * seq]

    u_ref = refs[-1]
    refs = refs[:-1]

    def modulate(x):
        return (x * (1.0 + sc_ref[...]) + sh_ref[...]).astype(BF16)

    h = modulate(x_ref[...])
    h_ext = jnp.concatenate([modulate(xp_ref[...]), h, modulate(xn_ref[...])], axis=0)

    def proj(c0, n):
        return u_ref[:, c0:c0 + n] + b_ref[:, c0:c0 + n]

    loc = _dot(h_ext, w_ref[:, C_AX:C_AG]) + b_ref[:, C_AX:C_AG]
    row = jax.lax.broadcasted_iota(jnp.int32, (tm + 2 * HALO, 1), 0)
    inside = ((row >= HALO) | (tile > 0)) & ((row < HALO + tm) | (tile < seq_tiles - 1))
    abuf[...] = jnp.where(inside, loc[:, 0:BRANCH_W], 0.0)
    zbuf[...] = jnp.where(inside, loc[:, BRANCH_W:2 * BRANCH_W] * jax.nn.sigmoid(loc[:, 2 * BRANCH_W:]), 0.0)
    pieces = _local_mixers(abuf, zbuf, ybuf, cbuf, ubuf, pw_ref, ps_ref, cw_ref, cb_ref, lg_ref, lb_ref, cpw_ref,
                           tile * tm, seq_len)
    result = []

    def advance(n):
        for _ in range(n):
            if not result:
                try:
                    next(pieces)
                except StopIteration as done:
                    result.append(done.value)

    u_ref[:, :C_AX] = _dot(h, w_ref[:, :C_AX])
    advance(8)
    u_ref[:, C_AG:] = _dot(h, w_ref[:, C_AG:])

    gq = _head_rms(proj(C_GQ, BRANCH_W), seg_ref[...], qn_ref[...])
    gk = _head_rms(proj(C_GK, KV_W), seg_ref[0:KV_W, 0:KV_W], kn_ref[...])
    gv = proj(C_GV, KV_W)
    if not latent:
        nk_ref, nv_ref, gk_ref, gv_ref = refs
        put_cache(gk_ref, gk)
        put_cache(gv_ref, gv)
        gqa_ref[:, 256:384] = gk.astype(BF16)
        gqa_ref[:, 384:512] = gv.astype(BF16)
    else:
        cos, sa, sb = cos_ref[...], sa_ref[...], sb_ref[...]
        gk = _rope(gk, cos, sa, sb)
        gq = _rope(gq, jnp.concatenate([cos, cos], axis=1), jnp.concatenate([sa, sa], axis=1),
                   jnp.concatenate([sb, sb], axis=1))
        k_ref, vt_ref = refs
        k_ref[...] = gk.astype(BF16)
        gvt = gv.T.astype(BF16)
        ones = jnp.ones((VT_ROWS - HEAD_DIM, gvt.shape[1]), BF16)
        for kv in range(GQA_KV_HEADS):
            vt_ref[kv * VT_ROWS:kv * VT_ROWS + HEAD_DIM, :] = gvt[kv * HEAD_DIM:(kv + 1) * HEAD_DIM]
            vt_ref[kv * VT_ROWS + HEAD_DIM:(kv + 1) * VT_ROWS, :] = ones
    gqa_ref[:, 0:256] = (gq * Q_SCALE).astype(BF16)

    while not result:
        advance(1)
    a, c = result[0]
    a_out_ref[...] = (a * _silu(proj(C_AG, BRANCH_W))).astype(BF16)
    c_out_ref[...] = (c * _silu(proj(C_CG, BRANCH_W))).astype(BF16)
    gates_ref[:, 0:BRANCH_W] = _silu(proj(C_NG, BRANCH_W)).astype(BF16)
    gates_ref[:, BRANCH_W:] = _silu(proj(C_GG, BRANCH_W)).astype(BF16)

    nat_ref[:, 0:256] = (proj(C_NQ, BRANCH_W) * Q_SCALE).astype(BF16)
    nk = proj(C_NK, BRANCH_W)
    nv = proj(C_NV, BRANCH_W)
    nat_ref[:, 256:512] = nk.astype(BF16)
    nat_ref[:, 512:768] = nv.astype(BF16)
    if not latent:
        put_cache(nk_ref, nk)
        put_cache(nv_ref, nv)


def _inproj(x, mod, mod_row0, layer, w_in, b_in, qn, kn, seg, local_w, rope_tabs, new_cache, tm):
    B, L, D = x.shape
    nt = L // tm
    latent = rope_tabs is not None
    seq_tiles = nt if latent else 1
    hb = tm // HALO
    nhb = L // HALO
    tok = lambda w: pl.BlockSpec((None, tm, w), lambda i, b: (b, i, 0))
    per_layer = lambda *shape: _layer_block(shape, layer)
    C = BRANCH_W
    in_specs = [
        tok(D),
        pl.BlockSpec((None, HALO, D), lambda i, b: (b, jnp.maximum(i * hb - 1, 0), 0)),
        pl.BlockSpec((None, HALO, D), lambda i, b: (b, jnp.minimum((i + 1) * hb, nhb - 1), 0)),
        pl.BlockSpec((None, None, 1, D), lambda i, b: (layer, mod_row0 + b, 0, 0)),
        pl.BlockSpec((None, None, 1, D), lambda i, b: (layer, mod_row0 + b, 0, 1)),
        per_layer(D, IN_WIDTH), per_layer(1, IN_WIDTH), per_layer(1, BRANCH_W), per_layer(1, KV_W),
        pl.BlockSpec((BRANCH_W, BRANCH_W), lambda i, b: (0, 0)),
        per_layer(C, C), per_layer(1, C), per_layer(CONV_WIDTH, C), per_layer(1, C), per_layer(1, C), per_layer(1, C),
        per_layer(C, C),
    ]
    args = [x, x, x, mod, mod, w_in, b_in, qn, kn, seg, *local_w]
    if latent:
        in_specs += [pl.BlockSpec((tm, KV_W), lambda i, b: (i, 0))] * 3
        args += list(rope_tabs)
    gqa_w = BRANCH_W if latent else BRANCH_W + 2 * KV_W
    out_specs = [tok(BRANCH_W), tok(BRANCH_W), tok(2 * BRANCH_W), tok(3 * BRANCH_W), tok(gqa_w)]
    out_shape = [jax.ShapeDtypeStruct((B, L, BRANCH_W), BF16), jax.ShapeDtypeStruct((B, L, BRANCH_W), BF16),
                 jax.ShapeDtypeStruct((B, L, 2 * BRANCH_W), BF16), jax.ShapeDtypeStruct((B, L, 3 * BRANCH_W), BF16),
                 jax.ShapeDtypeStruct((B, L, gqa_w), BF16)]
    if latent:
        out_specs += [tok(KV_W), pl.BlockSpec((None, GQA_KV_HEADS * VT_ROWS, tm), lambda i, b: (b, 0, i))]
        out_shape += [jax.ShapeDtypeStruct((B, L, KV_W), BF16),
                      jax.ShapeDtypeStruct((B, GQA_KV_HEADS * VT_ROWS, L), BF16)]
        aliases = {}
    else:
        seq = new_cache[0].shape[2]
        assert B == 1 and tm == seq
        aliases = {len(args) + j: len(out_specs) + j for j in range(len(new_cache))}
        in_specs += [pl.BlockSpec(memory_space=pl.ANY)] * len(new_cache)
        args += list(new_cache)
        out_specs += [pl.BlockSpec((1, None, seq, c.shape[3]), lambda i, b: (i, layer, 0, 0)) for c in new_cache]
        out_shape += [jax.ShapeDtypeStruct(c.shape, c.dtype) for c in new_cache]
    return pl.pallas_call(
        functools.partial(_inproj_kernel, latent=latent, seq_tiles=seq_tiles, seq_len=L if latent else tm),
        grid=(nt, B),
        in_specs=in_specs,
        out_specs=out_specs,
        out_shape=out_shape,
        input_output_aliases=aliases,
        scratch_shapes=[pltpu.VMEM((tm, IN_WIDTH), F32),
                        pltpu.VMEM((tm + 2 * HALO, C), F32), pltpu.VMEM((tm + 2 * HALO, C), F32),
                        pltpu.VMEM((tm, C), F32), pltpu.VMEM((tm, C), F32), pltpu.VMEM((8, tm + 8, C), F32)],
        compiler_params=_cparams(2),
        name="inproj_lat" if latent else "inproj_ctx",
    )(*args)


LOCAL_RC = 64


def _conv_chunk(rows):
    return max(c for c in range(8, 105, 8) if rows % c == 0)


def _local_mixers(abuf, zbuf, ybuf, cbuf, ubuf, pw_ref, ps_ref, cw_ref, cb_ref, lg_ref, lb_ref, cpw_ref, t0, seq_len):
    tm = ybuf.shape[0]
    LOCAL_UC = _conv_chunk(tm + 8)

    half_w = CONV_WIDTH // 2
    for res in range(8):
        taps = [8 * a + res for a in range(-2, 2) if -half_w <= 8 * a + res <= half_w]
        for u0 in range(0, tm + 8, LOCAL_UC):
            part = None
            for d in taps:
                term = zbuf[HALO + u0 + d - res:HALO + u0 + d - res + LOCAL_UC, :] * cw_ref[d + half_w:d + half_w + 1, :]
                part = term if part is None else part + term
            ubuf[res, u0:u0 + LOCAL_UC, :] = part
        yield

    lane = jax.lax.broadcasted_iota(jnp.int32, (LOCAL_RC, 128), 1)
    low = lane < POOL_GROUP_W
    for r0 in range(0, tm, LOCAL_RC):
        t = t0 + r0 + jax.lax.broadcasted_iota(jnp.int32, (LOCAL_RC, 128), 0)

        def sh(off, c0):
            return abuf[HALO + r0 + off:HALO + r0 + off + LOCAL_RC, c0:c0 + 128]

        def centred(total, half, x):
            cnt = jnp.minimum(t + half, seq_len) - jnp.maximum(t - half, 0)
            return total / cnt.astype(F32) - x

        x0 = sh(0, 0)
        s2 = x0 + sh(-1, 0)
        s4 = s2 + sh(-2, 0) + sh(1, 0)
        ybuf[r0:r0 + LOCAL_RC, 0:128] = centred(jnp.where(low, s2, s4), jnp.where(low, 1, 2), x0)
        x1 = sh(0, 128)
        s8 = x1
        for off in (-4, -3, -2, -1, 1, 2, 3):
            s8 = s8 + sh(off, 128)
        s16 = s8
        for off in (-8, -7, -6, -5, 4, 5, 6, 7):
            s16 = s16 + sh(off, 128)
        ybuf[r0:r0 + LOCAL_RC, 128:256] = centred(jnp.where(low, s8, s16), jnp.where(low, 4, 8), x1)

        acc = jnp.zeros((LOCAL_RC, BRANCH_W), F32) + cb_ref[...]
        for res in range(8):
            acc = acc + ubuf[res, r0 + res:r0 + res + LOCAL_RC, :]
        cbuf[r0:r0 + LOCAL_RC, :] = acc
        yield

    a = _dot(ybuf[...].astype(BF16), pw_ref[...]) * ps_ref[...]

    cz = cbuf[...]
    mu = jnp.mean(cz, axis=-1, keepdims=True)
    d = cz - mu
    var = jnp.mean(d * d, axis=-1, keepdims=True)
    zn = _silu(d * jax.lax.rsqrt(var + LN_EPS) * lg_ref[...] + lb_ref[...])
    return a, _dot(zn.astype(BF16), cpw_ref[...])


def _head_mask(shape, h):
    lane = jax.lax.broadcasted_iota(jnp.int32, shape, 1)
    return (lane >= h * HEAD_DIM) & (lane < (h + 1) * HEAD_DIM)


def _softmax_pv(s, v):
    m = jnp.max(s, axis=-1, keepdims=True)
    p = jnp.exp2(s - m)
    l = jnp.sum(p, axis=-1, keepdims=True)
    return _dot(p.astype(BF16), v) / l


def _attn_ctx_kernel(nat_ref, gqa_ref, n_out_ref, g_out_ref):
    q, k, v = nat_ref[:, 0:256], nat_ref[:, 256:512], nat_ref[:, 512:768]
    acc = jnp.zeros(q.shape, F32)
    for h in range(NAT_HEADS):
        msk = _head_mask(q.shape, h)
        o = _softmax_pv(_dot_t(jnp.where(msk, q, jnp.zeros_like(q)), k), v)
        acc = acc + jnp.where(msk, o, 0.0)
    n_out_ref[...] = acc.astype(BF16)

    k, v = gqa_ref[:, 256:384], gqa_ref[:, 384:512]
    sides = []
    for side in range(2):
        q = gqa_ref[:, side * 128:(side + 1) * 128]
        acc = jnp.zeros(q.shape, F32)
        for kv in range(GQA_KV_HEADS):
            msk = _head_mask(q.shape, kv)
            o = _softmax_pv(_dot_t(jnp.where(msk, q, jnp.zeros_like(q)), k), v)
            acc = acc + jnp.where(msk, o, 0.0)
        sides.append(acc)
    low = _head_mask(sides[0].shape, 0)
    g_out_ref[:, 0:128] = jnp.where(low, sides[0], pltpu.roll(sides[1], HEAD_DIM, 1)).astype(BF16)
    g_out_ref[:, 128:256] = jnp.where(low, pltpu.roll(sides[0], HEAD_DIM, 1), sides[1]).astype(BF16)


def _attn_ctx(nat, gqa):
    B, L, _ = nat.shape
    return pl.pallas_call(
        _attn_ctx_kernel,
        grid=(B,),
        in_specs=[pl.BlockSpec((None, L, 3 * BRANCH_W), lambda b: (b, 0, 0)),
                  pl.BlockSpec((None, L, BRANCH_W + 2 * KV_W), lambda b: (b, 0, 0))],
        out_specs=[pl.BlockSpec((None, L, BRANCH_W), lambda b: (b, 0, 0))] * 2,
        out_shape=[jax.ShapeDtypeStruct((B, L, BRANCH_W), BF16)] * 2,
        compiler_params=_cparams(1),
        name="attn_ctx",
    )(nat, gqa)


NAT_RB = 8
NAT_NLOC = NAT_ROWS * GRID_W


def _nat_kernel(q_ref, k_ref, v_ref, kc_ref, vc_ref, bias_ref, o_ref, s_a, s_b, *, rows):
    i = pl.program_id(1)
    kc = kc_ref[...]
    vc = vc_ref[...]
    nq = NAT_HEADS * GRID_W
    diag = (jax.lax.broadcasted_iota(jnp.int32, (nq, BRANCH_W), 0) // GRID_W
            == jax.lax.broadcasted_iota(jnp.int32, (nq, BRANCH_W), 1) // HEAD_DIM)
    bufs = (s_a, s_b)

    def window(rr):
        r = i * NAT_RB + rr
        row_start = jnp.clip(r - NAT_ROWS // 2, 0, rows - NAT_ROWS)
        return row_start - r + NAT_ROWS - 1, pl.multiple_of(row_start * GRID_W, GRID_W)

    def scores(rr, buf):
        off, base = window(rr)
        q_r = q_ref[rr * GRID_W:(rr + 1) * GRID_W, :]
        qs = jnp.where(diag, jnp.concatenate([q_r] * NAT_HEADS, axis=0), jnp.zeros((nq, BRANCH_W), BF16))
        bias = jnp.concatenate([bias_ref[off + 2 * jj] for jj in range(NAT_ROWS // 2)], axis=1)
        buf[:, 0:NAT_NLOC] = _dot_t(qs, k_ref[pl.ds(base, NAT_NLOC), :]) + bias
        buf[:, NAT_NLOC:] = _dot_t(qs, kc)

    scores(0, s_a)
    for rr in range(NAT_RB):
        cur = bufs[rr % 2]
        if rr + 1 < NAT_RB:
            scores(rr + 1, bufs[(rr + 1) % 2])
        _, base = window(rr)
        vw = v_ref[pl.ds(base, NAT_NLOC), :]
        m = jnp.max(cur[...], axis=-1, keepdims=True)
        p_loc = jnp.exp2(cur[:, 0:NAT_NLOC] - m)
        p_ctx = jnp.exp2(cur[:, NAT_NLOC:] - m)
        l = jnp.sum(p_loc, axis=-1, keepdims=True) + jnp.sum(p_ctx, axis=-1, keepdims=True)
        o = (_dot(p_loc.astype(BF16), vw) + _dot(p_ctx.astype(BF16), vc)) / l
        o = jnp.where(diag, o, 0.0)
        o_r = o[0:GRID_W]
        for h in range(1, NAT_HEADS):
            o_r = o_r + o[h * GRID_W:(h + 1) * GRID_W]
        o_ref[rr * GRID_W:(rr + 1) * GRID_W, :] = o_r.astype(BF16)


def _nat_lat(nat, ctx_k, ctx_v, layer, bias_tiles):
    B, L, _ = nat.shape
    rows = L // GRID_W
    lc = ctx_k.shape[2]
    tq = NAT_RB * GRID_W
    seq = lambda col: pl.BlockSpec((None, L, BRANCH_W), lambda b, i: (b, 0, col))
    ctx = pl.BlockSpec((None, None, lc, BRANCH_W), lambda b, i: (b, layer, 0, 0))
    return pl.pallas_call(
        functools.partial(_nat_kernel, rows=rows),
        grid=(B, rows // NAT_RB),
        in_specs=[pl.BlockSpec((None, tq, BRANCH_W), lambda b, i: (b, i, 0)), seq(1), seq(2), ctx, ctx,
                  _layer_block(bias_tiles.shape[1:], layer)],
        out_specs=pl.BlockSpec((None, tq, BRANCH_W), lambda b, i: (b, i, 0)),
        out_shape=jax.ShapeDtypeStruct((B, L, BRANCH_W), BF16),
        scratch_shapes=[pltpu.VMEM((NAT_HEADS * GRID_W, NAT_NLOC + lc), F32)] * 2,
        compiler_params=_cparams(2),
        name="nat_lat",
    )(nat, nat, nat, ctx_k, ctx_v, bias_tiles)


GQA_TQ = 256
GQA_NCHUNK = 6


def _gqa_kernel(q_ref, k_lat_ref, vt_lat_ref, k_ctx_ref, vt_ctx_ref, o_ref, k_ref, vt_ref, s_a, s_b):
    n_lat = k_lat_ref.shape[0]
    k_ref[0:n_lat, :] = k_lat_ref[...]
    k_ref[n_lat:, :] = k_ctx_ref[...]
    vt_ref[:, 0:n_lat] = vt_lat_ref[...]
    vt_ref[:, n_lat:] = vt_ctx_ref[...]
    kc_len = k_ref.shape[0] // GQA_NCHUNK
    nqt = q_ref.shape[0] // GQA_TQ
    heads = [(side, kv) for side in range(2) for kv in range(GQA_KV_HEADS)]
    bufs = (s_a, s_b)

    def masked_q(t):
        row = pl.multiple_of(t * GQA_TQ, GQA_TQ)
        out = []
        for side, kv in heads:
            q = q_ref[pl.ds(row, GQA_TQ), side * 128:(side + 1) * 128]
            out.append(jnp.where(_head_mask(q.shape, kv), q, jnp.zeros_like(q)))
        return out

    qh0 = masked_q(0)
    for h in range(len(heads)):
        s_a[h] = _dot_t(k_ref[0:kc_len, :], qh0[h])

    def qtile(t, carry):
        qh_cur = masked_q(t)
        qh_next = masked_q(jnp.minimum(t + 1, nqt - 1))
        state = [(jnp.full((1, GQA_TQ), NEG_INF, F32), jnp.zeros((VT_ROWS, GQA_TQ), F32)) for _ in heads]
        for c in range(GQA_NCHUNK):
            cur, nxt = bufs[c % 2], bufs[(c + 1) % 2]
            cn = (c + 1) % GQA_NCHUNK
            qn = qh_cur if c + 1 < GQA_NCHUNK else qh_next
            kn = k_ref[cn * kc_len:(cn + 1) * kc_len, :]
            vt = vt_ref[:, c * kc_len:(c + 1) * kc_len]
            for h, (side, kv) in enumerate(heads):
                nxt[h] = _dot_t(kn, qn[h])
                m, acc = state[h]
                s = cur[h]
                m_new = jnp.maximum(m, jnp.max(s, axis=0, keepdims=True))
                p = jnp.exp2(s - m_new).astype(BF16)
                acc = jnp.exp2(m - m_new) * acc + _dot(vt[kv * VT_ROWS:(kv + 1) * VT_ROWS], p)
                state[h] = (m_new, acc)
        outs = [acc[0:HEAD_DIM] / acc[HEAD_DIM:HEAD_DIM + 1] for _, acc in state]
        row = pl.multiple_of(t * GQA_TQ, GQA_TQ)
        for half in range(2):
            o_t = jnp.concatenate([outs[heads.index((side, half))] for side in range(2)], axis=0)
            o_ref[pl.ds(row, GQA_TQ), half * 128:(half + 1) * 128] = o_t.T.astype(BF16)
        return carry

    jax.lax.fori_loop(0, nqt, qtile, 0)


def _gqa_lat(q, k, vt, ctx_k, ctx_vt, layer):
    B, L, _ = q.shape
    lc = ctx_k.shape[2]
    lk = L + lc
    nvt = GQA_KV_HEADS * VT_ROWS
    assert GQA_NCHUNK % 2 == 0 and lk % (GQA_NCHUNK * 128) == 0 and L % GQA_TQ == 0 and L % 128 == 0
    return pl.pallas_call(
        _gqa_kernel,
        grid=(B,),
        in_specs=[pl.BlockSpec((None, L, BRANCH_W), lambda b: (b, 0, 0)),
                  pl.BlockSpec((None, L, KV_W), lambda b: (b, 0, 0)),
                  pl.BlockSpec((None, nvt, L), lambda b: (b, 0, 0)),
                  pl.BlockSpec((None, None, lc, KV_W), lambda b: (b, layer, 0, 0)),
                  pl.BlockSpec((None, None, nvt, lc), lambda b: (b, layer, 0, 0))],
        out_specs=pl.BlockSpec((None, L, BRANCH_W), lambda b: (b, 0, 0)),
        out_shape=jax.ShapeDtypeStruct((B, L, BRANCH_W), BF16),
        scratch_shapes=[pltpu.VMEM((lk, KV_W), BF16), pltpu.VMEM((nvt, lk), BF16)]
        + [pltpu.VMEM((GQA_HEADS, lk // GQA_NCHUNK, GQA_TQ), F32)] * 2,
        compiler_params=_cparams(1),
        name="gqa_lat",
    )(q, k, vt, ctx_k, ctx_vt)


def _ctx_prep_kernel(nk_ref, nv_ref, gk_ref, gv_ref, nk_o, nv_o, gk_o, gvt_o):
    nk_o[...] = nk_ref[...].astype(BF16)
    nv_o[...] = nv_ref[...].astype(BF16)
    gk_o[...] = gk_ref[...].astype(BF16)
    gvt = gv_ref[...].T.astype(BF16)
    ones = jnp.ones((VT_ROWS - HEAD_DIM, gvt.shape[1]), BF16)
    for kv in range(GQA_KV_HEADS):
        gvt_o[kv * VT_ROWS:kv * VT_ROWS + HEAD_DIM, :] = gvt[kv * HEAD_DIM:(kv + 1) * HEAD_DIM]
        gvt_o[kv * VT_ROWS + HEAD_DIM:(kv + 1) * VT_ROWS, :] = ones


def _ctx_prep(ck_n, cv_n, ck_g, cv_g):
    B, depth, lc, _ = ck_n.shape
    nvt = GQA_KV_HEADS * VT_ROWS
    blk = lambda r, c: pl.BlockSpec((None, None, r, c), lambda b, l: (b, l, 0, 0))
    return pl.pallas_call(
        _ctx_prep_kernel,
        grid=(B, depth),
        in_specs=[blk(lc, BRANCH_W), blk(lc, BRANCH_W), blk(lc, KV_W), blk(lc, KV_W)],
        out_specs=[blk(lc, BRANCH_W), blk(lc, BRANCH_W), blk(lc, KV_W), blk(nvt, lc)],
        out_shape=[jax.ShapeDtypeStruct((B, depth, lc, BRANCH_W), BF16)] * 2
        + [jax.ShapeDtypeStruct((B, depth, lc, KV_W), BF16), jax.ShapeDtypeStruct((B, depth, nvt, lc), BF16)],
        compiler_params=_cparams(2),
        name="ctx_prep",
    )(ck_n, cv_n, ck_g, cv_g)


OUT_TM = 2048
OUT_RC = 256


def _outproj_kernel(a_ref, n_ref, c_ref, g_ref, gates_ref, x_ref, gm_ref, w_ref, b_ref, lg_ref, lb_ref, o_ref):
    for r0 in range(0, x_ref.shape[0], OUT_RC):
        rows = slice(r0, r0 + OUT_RC)
        mixed = jnp.concatenate([a_ref[rows, :], n_ref[rows, :] * gates_ref[rows, 0:BRANCH_W],
                                 c_ref[rows, :], g_ref[rows, :] * gates_ref[rows, BRANCH_W:]], axis=1)
        out = _dot(mixed, w_ref[...]) + b_ref[...]
        y = DEEPNORM_ALPHA * x_ref[rows, :] + gm_ref[...] * out
        mu = jnp.mean(y, axis=-1, keepdims=True)
        d = y - mu
        var = jnp.mean(d * d, axis=-1, keepdims=True)
        o_ref[rows, :] = d * jax.lax.rsqrt(var + LN_EPS) * lg_ref[...] + lb_ref[...]


def _outproj(a, n, c, g, gates, x, mod, mod_row0, layer, w_out, b_out, ln_g, ln_b):
    B, L, D = x.shape
    tm = min(OUT_TM, L)
    assert L % tm == 0 and tm % OUT_RC == 0
    tok = lambda w: pl.BlockSpec((None, tm, w), lambda i, b: (b, i, 0))
    per_layer = lambda *shape: _layer_block(shape, layer)
    return pl.pallas_call(
        _outproj_kernel,
        grid=(L // tm, B),
        in_specs=[tok(BRANCH_W)] * 4 + [
            tok(2 * BRANCH_W), tok(D),
            pl.BlockSpec((None, None, 1, D), lambda i, b: (layer, mod_row0 + b, 0, 2)),
            per_layer(D, D), per_layer(1, D), per_layer(1, D), per_layer(1, D)],
        out_specs=tok(D),
        out_shape=jax.ShapeDtypeStruct((B, L, D), F32),
        compiler_params=_cparams(2),
        name="outproj",
    )(a, n, c, g, gates, x, mod, w_out, b_out, ln_g, ln_b)


def _rope_tables(seq_len):
    half = HEAD_DIM // 2
    nf = half // 2
    t = np.arange(seq_len)
    inv = np.float32(ROPE_THETA) ** (-np.arange(nf, dtype=np.float32) * np.float32(2.0) / np.float32(half))

    def tabs(pos):
        ang = pos.astype(np.float32)[:, None] * inv[None, :]
        return np.cos(ang), np.sin(ang)

    cr, sr = tabs(t // GRID_W)
    cc, sc = tabs(t % GRID_W)
    zero = np.zeros_like(sr)
    cos = np.concatenate([cr, cr, cc, cc], axis=1)
    sin_a = np.concatenate([-sr, zero, -sc, zero], axis=1)
    sin_b = np.concatenate([zero, sr, zero, sc], axis=1)
    return tuple(jnp.asarray(np.tile(a, (1, 2)), F32) for a in (cos, sin_a, sin_b))


def _nat_bias_tiles(bias_tab):
    col = np.arange(GRID_W)
    col_start = np.clip(col - NAT_COLS // 2, 0, GRID_W - NAT_COLS)
    col_in = (col[None, :] >= col_start[:, None]) & (col[None, :] < col_start[:, None] + NAT_COLS)
    edge = GRID_W - NAT_COLS
    n_diff = 2 * GRID_W - 1
    padded = jnp.concatenate([jnp.repeat(bias_tab[..., :1], edge, axis=-1), bias_tab,
                              jnp.repeat(bias_tab[..., -1:], edge, axis=-1)], axis=-1)
    flat = jnp.tile(padded, (1, 1, 1, GRID_W + 1))
    toeplitz = flat[..., GRID_W - 1:GRID_W - 1 + GRID_W * (n_diff - 1)].reshape(
        bias_tab.shape[:-1] + (GRID_W, n_diff - 1))[..., :GRID_W]
    full = jnp.where(col_in, toeplitz * LOG2E, NEG_INF)
    pair = jnp.concatenate([full[:, :, :-1], full[:, :, 1:]], axis=-1)
    return jnp.transpose(pair, (0, 2, 1, 3, 4)).reshape(-1, 2 * NAT_ROWS - 2, NAT_HEADS * GRID_W, 2 * GRID_W)


def kernel(x_prompt, x_sample, c, cache_nat_k, cache_nat_v, cache_gqa_k, cache_gqa_v, c_ctx, w_mod, b_mod, w_in, b_in,
           pool_w, pool_scale, nat_bias, q_norm, k_norm, conv_w, conv_b, conv_ln_g, conv_ln_b, conv_pw, w_out, b_out,
           ln_g, ln_b):
    nb, seq, D = x_prompt.shape
    db, dseq, _ = x_sample.shape
    lc = cache_nat_k.shape[2]

    cond = jnp.zeros((MOD_ROWS, D), F32).at[0].set(c_ctx).at[1:1 + db].set(c)
    mod = _modulation(cond, w_mod, b_mod).reshape(DEPTH, MOD_ROWS, 1, 3 * D)

    def kernel_columns(a):
        return jnp.concatenate([a[..., c0:c0 + w] for _, c0, w in _IN_SEGMENTS], axis=-1)

    w_in_b = kernel_columns(w_in).astype(BF16)
    b_in_p = kernel_columns(b_in).reshape(DEPTH, 1, IN_WIDTH)
    w_out_b = w_out.astype(BF16)

    seg = jnp.asarray(np.kron(np.eye(GQA_HEADS), np.ones((HEAD_DIM, HEAD_DIM))), BF16)
    rope_tabs = _rope_tables(dseq)
    ck_n, cv_n, ck_g, cvt_g = _ctx_prep(cache_nat_k.reshape(db, DEPTH, lc, BRANCH_W),
                                        cache_nat_v.reshape(db, DEPTH, lc, BRANCH_W),
                                        cache_gqa_k.reshape(db, DEPTH, lc, KV_W),
                                        cache_gqa_v.reshape(db, DEPTH, lc, KV_W))

    row = lambda a: a.reshape(DEPTH, 1, -1)
    n_groups = len(POOL_WINDOWS)
    pool_bd = jnp.einsum('gh,lgcd->lgchd', jnp.eye(n_groups, dtype=F32), pool_w).reshape(DEPTH, BRANCH_W, BRANCH_W)
    in_w = (w_in_b, b_in_p, row(jnp.tile(q_norm, (1, GQA_HEADS))), row(jnp.tile(k_norm, (1, GQA_KV_HEADS))), seg)
    local_w = (pool_bd.astype(BF16), row(pool_scale), conv_w, row(conv_b), row(conv_ln_g), row(conv_ln_b),
               conv_pw.astype(BF16))
    out_w = (w_out_b, row(b_out), row(ln_g), row(ln_b))
    bias_tiles = _nat_bias_tiles(nat_bias)

    y_p = x_prompt.reshape(1, nb * seq, D)
    y_s = x_sample
    new_cache = [jnp.zeros((nb, DEPTH, seq, w), F32) for w in (BRANCH_W, BRANCH_W, KV_W, KV_W)]
    for l in range(DEPTH):
        a_out, c_out, gates, nat, gqa, *new_cache = _inproj(y_p, mod, 0, l, *in_w, local_w, None, new_cache, seq)
        n_out, g_out = _attn_ctx(nat.reshape(nb, seq, -1), gqa.reshape(nb, seq, -1))
        flat = lambda a: a.reshape(1, nb * seq, BRANCH_W)
        y_p = _outproj(a_out, flat(n_out), c_out, flat(g_out), gates, y_p, mod, 0, l, *out_w)

        a_out, c_out, gates, nat, gq, gk, gvt = _inproj(y_s, mod, 1, l, *in_w, local_w, rope_tabs, None, PROJ_TM)
        n_out = _nat_lat(nat, ck_n, cv_n, l, bias_tiles)
        g_out = _gqa_lat(gq, gk, gvt, ck_g, cvt_g, l)
        y_s = _outproj(a_out, n_out, c_out, g_out, gates, y_s, mod, 1, l, *out_w)

    heads = (NAT_HEADS, NAT_HEADS, GQA_KV_HEADS, GQA_KV_HEADS)
    return (y_p.reshape(nb, seq, D), y_s) + tuple(c.reshape(nb, DEPTH, seq, h, HEAD_DIM)
                                                  for c, h in zip(new_cache, heads))
```

```python
import functools

import numpy as np
import jax
import jax.numpy as jnp
from jax.experimental import pallas as pl
from jax.experimental.pallas import tpu as pltpu

F32 = jnp.float32
BF16 = jnp.bfloat16
LANES = 128
SUBLANES = 8

D_MODEL = 1024
DEPTH = 4
GRID_W = 64
HEAD_DIM = 64
BRANCH_W = D_MODEL // 4
POOL_WINDOWS = (2, 4, 8, 16)
POOL_GROUP_W = BRANCH_W // len(POOL_WINDOWS)
NAT_HEADS = BRANCH_W // HEAD_DIM
NAT_ROWS = 8
NAT_COLS = 16
CONV_WIDTH = 31
GQA_HEADS = BRANCH_W // HEAD_DIM
GQA_KV_HEADS = GQA_HEADS // 2
KV_W = GQA_KV_HEADS * HEAD_DIM
ROPE_THETA = 10000.0
LN_EPS = 1e-5
RMS_EPS = 1e-6
NEG_INF = -1e30
DEEPNORM_ALPHA = (2 * DEPTH) ** 0.25
LOG2E = 1.4426950408889634
Q_SCALE = HEAD_DIM ** -0.5 * LOG2E
VT_ROWS = 80

_MODEL_COLUMNS = (("ax", BRANCH_W), ("ag", BRANCH_W), ("nq", BRANCH_W), ("nk", BRANCH_W), ("nv", BRANCH_W),
                  ("ng", BRANCH_W), ("cv", BRANCH_W), ("cglu", BRANCH_W), ("cg", BRANCH_W),
                  ("gq", GQA_HEADS * HEAD_DIM), ("gk", KV_W), ("gv", KV_W), ("gg", BRANCH_W))
_MODEL_OFFSET = dict(zip((n for n, _ in _MODEL_COLUMNS), np.cumsum([0] + [w for _, w in _MODEL_COLUMNS])[:-1].tolist()))
_GQA_HEAD_ORDER = (0, 2, 1, 3)
_IN_SEGMENTS = (
    [("gq%d" % h, _MODEL_OFFSET["gq"] + h * HEAD_DIM, HEAD_DIM) for h in _GQA_HEAD_ORDER]
    + [(n, _MODEL_OFFSET[n], dict(_MODEL_COLUMNS)[n])
       for n in ("gk", "gv", "ax", "cv", "cglu", "ag", "ng", "cg", "gg", "nq", "nk", "nv")])
IN_WIDTH = sum(w for _, _, w in _IN_SEGMENTS)
_IN_OFFSET = dict(zip((n for n, _, _ in _IN_SEGMENTS), np.cumsum([0] + [w for _, _, w in _IN_SEGMENTS])[:-1].tolist()))
C_GQ, C_GK, C_GV = _IN_OFFSET["gq0"], _IN_OFFSET["gk"], _IN_OFFSET["gv"]
C_AX, C_CV, C_CGLU = _IN_OFFSET["ax"], _IN_OFFSET["cv"], _IN_OFFSET["cglu"]
C_AG, C_NG, C_CG, C_GG = _IN_OFFSET["ag"], _IN_OFFSET["ng"], _IN_OFFSET["cg"], _IN_OFFSET["gg"]
C_NQ, C_NK, C_NV = _IN_OFFSET["nq"], _IN_OFFSET["nk"], _IN_OFFSET["nv"]

ROPE_PAIR = HEAD_DIM // 4
PROJ_TM = 512
MOD_TN = 1024
HALO = 16
VMEM_LIMIT = 56 * 1024 * 1024
MOD_ROWS = 16


def _cparams(n_axes):
    return pltpu.CompilerParams(dimension_semantics=("arbitrary",) * n_axes, vmem_limit_bytes=VMEM_LIMIT)


def _layer_block(shape, layer):
    return pl.BlockSpec((None,) + tuple(shape), lambda *_: (layer,) + (0,) * len(shape))


def _silu(x):
    return x * jax.nn.sigmoid(x)


def _dot(a, b):
    return jnp.dot(a, b, preferred_element_type=F32)


def _dot_t(a, b):
    return jax.lax.dot_general(a, b, (((1,), (1,)), ((), ())), preferred_element_type=F32)


def _mod_kernel(cond_ref, w_ref, b_ref, o_ref):
    a = _silu(cond_ref[...]).astype(BF16)
    o_ref[...] = _dot(a, w_ref[...].astype(BF16)) + b_ref[...]


def _modulation(cond, w_mod, b_mod):
    return pl.pallas_call(
        _mod_kernel,
        grid=(DEPTH, 3 * D_MODEL // MOD_TN),
        in_specs=[
            pl.BlockSpec((MOD_ROWS, D_MODEL), lambda l, j: (0, 0)),
            pl.BlockSpec((None, D_MODEL, MOD_TN), lambda l, j: (l, 0, j)),
            pl.BlockSpec((None, 1, MOD_TN), lambda l, j: (l, 0, j)),
        ],
        out_specs=pl.BlockSpec((None, MOD_ROWS, MOD_TN), lambda l, j: (l, 0, j)),
        out_shape=jax.ShapeDtypeStruct((DEPTH, MOD_ROWS, 3 * D_MODEL), F32),
        compiler_params=_cparams(2),
        name="modulation",
    )(cond, w_mod, b_mod.reshape(DEPTH, 1, 3 * D_MODEL))


def _head_rms(x, seg, gain):
    tot = _dot((x * x).astype(BF16), seg)
    return x * jax.lax.rsqrt(tot * (1.0 / HEAD_DIM) + RMS_EPS) * gain


def _rope(x, cos, sin_a, sin_b):
    n = x.shape[-1]
    return x * cos + pltpu.roll(x, n - ROPE_PAIR, 1) * sin_a + pltpu.roll(x, ROPE_PAIR, 1) * sin_b


def _inproj_kernel(*refs, latent, seq_tiles, seq_len):
    (x_ref, xp_ref, xn_ref, sh_ref, sc_ref, w_ref, b_ref, qn_ref, kn_ref, seg_ref,
     pw_ref, ps_ref, cw_ref, cb_ref, lg_ref, lb_ref, cpw_ref) = refs[:17]
    refs = refs[17:]
    if latent:
        cos_ref, sa_ref, sb_ref = refs[:3]
        refs = refs[3:]
    else:
        refs = refs[4:]
    a_out_ref, c_out_ref, gates_ref, nat_ref, gqa_ref = refs[:5]
    refs = refs[5:]
    abuf, zbuf, ybuf, cbuf, ubuf = refs[-5:]
    refs = refs[:-5]
    tm = x_ref.shape[0]
    tile = pl.program_id(0) % seq_tiles

    def put_cache(ref, val):
        seq = ref.shape[1]
        for sidx in range(ref.shape[0]):
            ref[sidx] = val[sidx * seq:(sidx + 1) * seq]

    u_ref = refs[-1]
    refs = refs[:-1]

    def modulate(x):
        return (x * (1.0 + sc_ref[...]) + sh_ref[...]).astype(BF16)

    h = modulate(x_ref[...])
    h_ext = jnp.concatenate([modulate(xp_ref[...]), h, modulate(xn_ref[...])], axis=0)

    def proj(c0, n):
        return u_ref[:, c0:c0 + n] + b_ref[:, c0:c0 + n]

    loc = _dot(h_ext, w_ref[:, C_AX:C_AG]) + b_ref[:, C_AX:C_AG]
    row = jax.lax.broadcasted_iota(jnp.int32, (tm + 2 * HALO, 1), 0)
    inside = ((row >= HALO) | (tile > 0)) & ((row < HALO + tm) | (tile < seq_tiles - 1))
    abuf[...] = jnp.where(inside, loc[:, 0:BRANCH_W], 0.0)
    zbuf[...] = jnp.where(inside, loc[:, BRANCH_W:2 * BRANCH_W] * jax.nn.sigmoid(loc[:, 2 * BRANCH_W:]), 0.0)
    pieces = _local_mixers(abuf, zbuf, ybuf, cbuf, ubuf, pw_ref, ps_ref, cw_ref, cb_ref, lg_ref, lb_ref, cpw_ref,
                           tile * tm, seq_len)
    result = []

    def advance(n):
        for _ in range(n):
            if not result:
                try:
                    next(pieces)
                except StopIteration as done:
                    result.append(done.value)

    u_ref[:, :C_AX] = _dot(h, w_ref[:, :C_AX])
    advance(SUBLANES)
    u_ref[:, C_AG:] = _dot(h, w_ref[:, C_AG:])

    gq = _head_rms(proj(C_GQ, BRANCH_W), seg_ref[...], qn_ref[...])
    gk = _head_rms(proj(C_GK, KV_W), seg_ref[0:KV_W, 0:KV_W], kn_ref[...])
    gv = proj(C_GV, KV_W)
    if not latent:
        nk_ref, nv_ref, gk_ref, gv_ref = refs
        put_cache(gk_ref, gk)
        put_cache(gv_ref, gv)
        gqa_ref[:, BRANCH_W:BRANCH_W + KV_W] = gk.astype(BF16)
        gqa_ref[:, BRANCH_W + KV_W:] = gv.astype(BF16)
    else:
        cos, sa, sb = cos_ref[...], sa_ref[...], sb_ref[...]
        gk = _rope(gk, cos, sa, sb)
        gq = _rope(gq, jnp.concatenate([cos, cos], axis=1), jnp.concatenate([sa, sa], axis=1),
                   jnp.concatenate([sb, sb], axis=1))
        k_ref, vt_ref = refs
        k_ref[...] = gk.astype(BF16)
        gvt = gv.T.astype(BF16)
        ones = jnp.ones((VT_ROWS - HEAD_DIM, gvt.shape[1]), BF16)
        for kv in range(GQA_KV_HEADS):
            vt_ref[kv * VT_ROWS:kv * VT_ROWS + HEAD_DIM, :] = gvt[kv * HEAD_DIM:(kv + 1) * HEAD_DIM]
            vt_ref[kv * VT_ROWS + HEAD_DIM:(kv + 1) * VT_ROWS, :] = ones
    gqa_ref[:, 0:BRANCH_W] = (gq * Q_SCALE).astype(BF16)

    while not result:
        advance(1)
    a, c = result[0]
    a_out_ref[...] = (a * _silu(proj(C_AG, BRANCH_W))).astype(BF16)
    c_out_ref[...] = (c * _silu(proj(C_CG, BRANCH_W))).astype(BF16)
    gates_ref[:, 0:BRANCH_W] = _silu(proj(C_NG, BRANCH_W)).astype(BF16)
    gates_ref[:, BRANCH_W:] = _silu(proj(C_GG, BRANCH_W)).astype(BF16)

    nat_ref[:, 0:BRANCH_W] = (proj(C_NQ, BRANCH_W) * Q_SCALE).astype(BF16)
    nk = proj(C_NK, BRANCH_W)
    nv = proj(C_NV, BRANCH_W)
    nat_ref[:, BRANCH_W:2 * BRANCH_W] = nk.astype(BF16)
    nat_ref[:, 2 * BRANCH_W:] = nv.astype(BF16)
    if not latent:
        put_cache(nk_ref, nk)
        put_cache(nv_ref, nv)


def _inproj(x, mod, mod_row0, layer, w_in, b_in, qn, kn, seg, local_w, rope_tabs, new_cache, tm):
    B, L, D = x.shape
    nt = L // tm
    latent = rope_tabs is not None
    seq_tiles = nt if latent else 1
    hb = tm // HALO
    nhb = L // HALO
    tok = lambda w: pl.BlockSpec((None, tm, w), lambda i, b: (b, i, 0))
    per_layer = lambda *shape: _layer_block(shape, layer)
    C = BRANCH_W
    in_specs = [
        tok(D),
        pl.BlockSpec((None, HALO, D), lambda i, b: (b, jnp.maximum(i * hb - 1, 0), 0)),
        pl.BlockSpec((None, HALO, D), lambda i, b: (b, jnp.minimum((i + 1) * hb, nhb - 1), 0)),
        pl.BlockSpec((None, None, 1, D), lambda i, b: (layer, mod_row0 + b, 0, 0)),
        pl.BlockSpec((None, None, 1, D), lambda i, b: (layer, mod_row0 + b, 0, 1)),
        per_layer(D, IN_WIDTH), per_layer(1, IN_WIDTH), per_layer(1, BRANCH_W), per_layer(1, KV_W),
        pl.BlockSpec((BRANCH_W, BRANCH_W), lambda i, b: (0, 0)),
        per_layer(C, C), per_layer(1, C), per_layer(CONV_WIDTH, C), per_layer(1, C), per_layer(1, C), per_layer(1, C),
        per_layer(C, C),
    ]
    args = [x, x, x, mod, mod, w_in, b_in, qn, kn, seg, *local_w]
    if latent:
        in_specs += [pl.BlockSpec((tm, KV_W), lambda i, b: (i, 0))] * 3
        args += list(rope_tabs)
    gqa_w = BRANCH_W if latent else BRANCH_W + 2 * KV_W
    out_specs = [tok(BRANCH_W), tok(BRANCH_W), tok(2 * BRANCH_W), tok(3 * BRANCH_W), tok(gqa_w)]
    out_shape = [jax.ShapeDtypeStruct((B, L, BRANCH_W), BF16), jax.ShapeDtypeStruct((B, L, BRANCH_W), BF16),
                 jax.ShapeDtypeStruct((B, L, 2 * BRANCH_W), BF16), jax.ShapeDtypeStruct((B, L, 3 * BRANCH_W), BF16),
                 jax.ShapeDtypeStruct((B, L, gqa_w), BF16)]
    if latent:
        out_specs += [tok(KV_W), pl.BlockSpec((None, GQA_KV_HEADS * VT_ROWS, tm), lambda i, b: (b, 0, i))]
        out_shape += [jax.ShapeDtypeStruct((B, L, KV_W), BF16),
                      jax.ShapeDtypeStruct((B, GQA_KV_HEADS * VT_ROWS, L), BF16)]
        aliases = {}
    else:
        seq = new_cache[0].shape[2]
        assert B == 1 and tm == seq
        aliases = {len(args) + j: len(out_specs) + j for j in range(len(new_cache))}
        in_specs += [pl.BlockSpec(memory_space=pl.ANY)] * len(new_cache)
        args += list(new_cache)
        out_specs += [pl.BlockSpec((1, None, seq, c.shape[3]), lambda i, b: (i, layer, 0, 0)) for c in new_cache]
        out_shape += [jax.ShapeDtypeStruct(c.shape, c.dtype) for c in new_cache]
    return pl.pallas_call(
        functools.partial(_inproj_kernel, latent=latent, seq_tiles=seq_tiles, seq_len=L if latent else tm),
        grid=(nt, B),
        in_specs=in_specs,
        out_specs=out_specs,
        out_shape=out_shape,
        input_output_aliases=aliases,
        scratch_shapes=[pltpu.VMEM((tm, IN_WIDTH), F32),
                        pltpu.VMEM((tm + 2 * HALO, C), F32), pltpu.VMEM((tm + 2 * HALO, C), F32),
                        pltpu.VMEM((tm, C), F32), pltpu.VMEM((tm, C), F32),
                        pltpu.VMEM((SUBLANES, tm + SUBLANES, C), F32)],
        compiler_params=_cparams(2),
        name="inproj_lat" if latent else "inproj_ctx",
    )(*args)


LOCAL_RC = 64


CONV_CHUNK_MAX = 104


def _conv_chunk(rows):
    return max(c for c in range(SUBLANES, CONV_CHUNK_MAX + 1, SUBLANES) if rows % c == 0)


def _local_mixers(abuf, zbuf, ybuf, cbuf, ubuf, pw_ref, ps_ref, cw_ref, cb_ref, lg_ref, lb_ref, cpw_ref, t0, seq_len):
    tm = ybuf.shape[0]
    LOCAL_UC = _conv_chunk(tm + SUBLANES)

    half_w = CONV_WIDTH // 2
    for res in range(SUBLANES):
        taps = [d for d in range(res - HALO, HALO, SUBLANES) if -half_w <= d <= half_w]
        for u0 in range(0, tm + SUBLANES, LOCAL_UC):
            part = None
            for d in taps:
                term = zbuf[HALO + u0 + d - res:HALO + u0 + d - res + LOCAL_UC, :] * cw_ref[d + half_w:d + half_w + 1, :]
                part = term if part is None else part + term
            ubuf[res, u0:u0 + LOCAL_UC, :] = part
        yield

    assert 2 * POOL_GROUP_W == LANES and POOL_WINDOWS == (2, 4, 8, 16)
    lane = jax.lax.broadcasted_iota(jnp.int32, (LOCAL_RC, LANES), 1)
    low = lane < POOL_GROUP_W
    for r0 in range(0, tm, LOCAL_RC):
        t = t0 + r0 + jax.lax.broadcasted_iota(jnp.int32, (LOCAL_RC, LANES), 0)

        def sh(off, c0):
            return abuf[HALO + r0 + off:HALO + r0 + off + LOCAL_RC, c0:c0 + LANES]

        def centred(total, half, x):
            cnt = jnp.minimum(t + half, seq_len) - jnp.maximum(t - half, 0)
            return total / cnt.astype(F32) - x

        x0 = sh(0, 0)
        s2 = x0 + sh(-1, 0)
        s4 = s2 + sh(-2, 0) + sh(1, 0)
        ybuf[r0:r0 + LOCAL_RC, 0:LANES] = centred(jnp.where(low, s2, s4), jnp.where(low, 1, 2), x0)
        x1 = sh(0, LANES)
        s8 = x1
        for off in (-4, -3, -2, -1, 1, 2, 3):
            s8 = s8 + sh(off, LANES)
        s16 = s8
        for off in (-8, -7, -6, -5, 4, 5, 6, 7):
            s16 = s16 + sh(off, LANES)
        ybuf[r0:r0 + LOCAL_RC, LANES:] = centred(jnp.where(low, s8, s16), jnp.where(low, 4, 8), x1)

        acc = jnp.zeros((LOCAL_RC, BRANCH_W), F32) + cb_ref[...]
        for res in range(SUBLANES):
            acc = acc + ubuf[res, r0 + res:r0 + res + LOCAL_RC, :]
        cbuf[r0:r0 + LOCAL_RC, :] = acc
        yield

    a = _dot(ybuf[...].astype(BF16), pw_ref[...]) * ps_ref[...]

    cz = cbuf[...]
    mu = jnp.mean(cz, axis=-1, keepdims=True)
    d = cz - mu
    var = jnp.mean(d * d, axis=-1, keepdims=True)
    zn = _silu(d * jax.lax.rsqrt(var + LN_EPS) * lg_ref[...] + lb_ref[...])
    return a, _dot(zn.astype(BF16), cpw_ref[...])


def _head_mask(shape, h):
    lane = jax.lax.broadcasted_iota(jnp.int32, shape, 1)
    return (lane >= h * HEAD_DIM) & (lane < (h + 1) * HEAD_DIM)


def _softmax_pv(s, v):
    m = jnp.max(s, axis=-1, keepdims=True)
    p = jnp.exp2(s - m)
    l = jnp.sum(p, axis=-1, keepdims=True)
    return _dot(p.astype(BF16), v) / l


def _attn_ctx_kernel(nat_ref, gqa_ref, n_out_ref, g_out_ref):
    q, k, v = (nat_ref[:, j * BRANCH_W:(j + 1) * BRANCH_W] for j in range(3))
    acc = jnp.zeros(q.shape, F32)
    for h in range(NAT_HEADS):
        msk = _head_mask(q.shape, h)
        o = _softmax_pv(_dot_t(jnp.where(msk, q, jnp.zeros_like(q)), k), v)
        acc = acc + jnp.where(msk, o, 0.0)
    n_out_ref[...] = acc.astype(BF16)

    k, v = gqa_ref[:, BRANCH_W:BRANCH_W + KV_W], gqa_ref[:, BRANCH_W + KV_W:]
    sides = []
    for side in range(2):
        q = gqa_ref[:, side * KV_W:(side + 1) * KV_W]
        acc = jnp.zeros(q.shape, F32)
        for kv in range(GQA_KV_HEADS):
            msk = _head_mask(q.shape, kv)
            o = _softmax_pv(_dot_t(jnp.where(msk, q, jnp.zeros_like(q)), k), v)
            acc = acc + jnp.where(msk, o, 0.0)
        sides.append(acc)
    low = _head_mask(sides[0].shape, 0)
    g_out_ref[:, 0:KV_W] = jnp.where(low, sides[0], pltpu.roll(sides[1], HEAD_DIM, 1)).astype(BF16)
    g_out_ref[:, KV_W:] = jnp.where(low, pltpu.roll(sides[0], HEAD_DIM, 1), sides[1]).astype(BF16)


def _attn_ctx(nat, gqa):
    B, L, _ = nat.shape
    return pl.pallas_call(
        _attn_ctx_kernel,
        grid=(B,),
        in_specs=[pl.BlockSpec((None, L, 3 * BRANCH_W), lambda b: (b, 0, 0)),
                  pl.BlockSpec((None, L, BRANCH_W + 2 * KV_W), lambda b: (b, 0, 0))],
        out_specs=[pl.BlockSpec((None, L, BRANCH_W), lambda b: (b, 0, 0))] * 2,
        out_shape=[jax.ShapeDtypeStruct((B, L, BRANCH_W), BF16)] * 2,
        compiler_params=_cparams(1),
        name="attn_ctx",
    )(nat, gqa)


NAT_RB = 8
NAT_NLOC = NAT_ROWS * GRID_W


def _nat_kernel(q_ref, k_ref, v_ref, kc_ref, vc_ref, bias_ref, o_ref, s_a, s_b, *, rows):
    i = pl.program_id(1)
    kc = kc_ref[...]
    vc = vc_ref[...]
    nq = NAT_HEADS * GRID_W
    diag = (jax.lax.broadcasted_iota(jnp.int32, (nq, BRANCH_W), 0) // GRID_W
            == jax.lax.broadcasted_iota(jnp.int32, (nq, BRANCH_W), 1) // HEAD_DIM)
    bufs = (s_a, s_b)

    def window(rr):
        r = i * NAT_RB + rr
        row_start = jnp.clip(r - NAT_ROWS // 2, 0, rows - NAT_ROWS)
        return row_start - r + NAT_ROWS - 1, pl.multiple_of(row_start * GRID_W, GRID_W)

    def scores(rr, buf):
        off, base = window(rr)
        q_r = q_ref[rr * GRID_W:(rr + 1) * GRID_W, :]
        qs = jnp.where(diag, jnp.concatenate([q_r] * NAT_HEADS, axis=0), jnp.zeros((nq, BRANCH_W), BF16))
        bias = jnp.concatenate([bias_ref[off + 2 * jj] for jj in range(NAT_ROWS // 2)], axis=1)
        buf[:, 0:NAT_NLOC] = _dot_t(qs, k_ref[pl.ds(base, NAT_NLOC), :]) + bias
        buf[:, NAT_NLOC:] = _dot_t(qs, kc)

    scores(0, s_a)
    for rr in range(NAT_RB):
        cur = bufs[rr % 2]
        if rr + 1 < NAT_RB:
            scores(rr + 1, bufs[(rr + 1) % 2])
        _, base = window(rr)
        vw = v_ref[pl.ds(base, NAT_NLOC), :]
        m = jnp.max(cur[...], axis=-1, keepdims=True)
        p_loc = jnp.exp2(cur[:, 0:NAT_NLOC] - m)
        p_ctx = jnp.exp2(cur[:, NAT_NLOC:] - m)
        l = jnp.sum(p_loc, axis=-1, keepdims=True) + jnp.sum(p_ctx, axis=-1, keepdims=True)
        o = (_dot(p_loc.astype(BF16), vw) + _dot(p_ctx.astype(BF16), vc)) / l
        o = jnp.where(diag, o, 0.0)
        o_r = o[0:GRID_W]
        for h in range(1, NAT_HEADS):
            o_r = o_r + o[h * GRID_W:(h + 1) * GRID_W]
        o_ref[rr * GRID_W:(rr + 1) * GRID_W, :] = o_r.astype(BF16)


def _nat_lat(nat, ctx_k, ctx_v, layer, bias_tiles):
    B, L, _ = nat.shape
    rows = L // GRID_W
    lc = ctx_k.shape[2]
    tq = NAT_RB * GRID_W
    seq = lambda col: pl.BlockSpec((None, L, BRANCH_W), lambda b, i: (b, 0, col))
    ctx = pl.BlockSpec((None, None, lc, BRANCH_W), lambda b, i: (b, layer, 0, 0))
    return pl.pallas_call(
        functools.partial(_nat_kernel, rows=rows),
        grid=(B, rows // NAT_RB),
        in_specs=[pl.BlockSpec((None, tq, BRANCH_W), lambda b, i: (b, i, 0)), seq(1), seq(2), ctx, ctx,
                  _layer_block(bias_tiles.shape[1:], layer)],
        out_specs=pl.BlockSpec((None, tq, BRANCH_W), lambda b, i: (b, i, 0)),
        out_shape=jax.ShapeDtypeStruct((B, L, BRANCH_W), BF16),
        scratch_shapes=[pltpu.VMEM((NAT_HEADS * GRID_W, NAT_NLOC + lc), F32)] * 2,
        compiler_params=_cparams(2),
        name="nat_lat",
    )(nat, nat, nat, ctx_k, ctx_v, bias_tiles)


GQA_TQ = 256
GQA_NCHUNK = 6


def _gqa_kernel(q_ref, k_lat_ref, vt_lat_ref, k_ctx_ref, vt_ctx_ref, o_ref, k_ref, vt_ref, s_a, s_b):
    n_lat = k_lat_ref.shape[0]
    k_ref[0:n_lat, :] = k_lat_ref[...]
    k_ref[n_lat:, :] = k_ctx_ref[...]
    vt_ref[:, 0:n_lat] = vt_lat_ref[...]
    vt_ref[:, n_lat:] = vt_ctx_ref[...]
    kc_len = k_ref.shape[0] // GQA_NCHUNK
    nqt = q_ref.shape[0] // GQA_TQ
    heads = [(side, kv) for side in range(2) for kv in range(GQA_KV_HEADS)]
    bufs = (s_a, s_b)

    def masked_q(t):
        row = pl.multiple_of(t * GQA_TQ, GQA_TQ)
        out = []
        for side, kv in heads:
            q = q_ref[pl.ds(row, GQA_TQ), side * KV_W:(side + 1) * KV_W]
            out.append(jnp.where(_head_mask(q.shape, kv), q, jnp.zeros_like(q)))
        return out

    qh0 = masked_q(0)
    for h in range(len(heads)):
        s_a[h] = _dot_t(k_ref[0:kc_len, :], qh0[h])

    def qtile(t, carry):
        qh_cur = masked_q(t)
        qh_next = masked_q(jnp.minimum(t + 1, nqt - 1))
        state = [(jnp.full((1, GQA_TQ), NEG_INF, F32), jnp.zeros((VT_ROWS, GQA_TQ), F32)) for _ in heads]
        for c in range(GQA_NCHUNK):
            cur, nxt = bufs[c % 2], bufs[(c + 1) % 2]
            cn = (c + 1) % GQA_NCHUNK
            qn = qh_cur if c + 1 < GQA_NCHUNK else qh_next
            kn = k_ref[cn * kc_len:(cn + 1) * kc_len, :]
            vt = vt_ref[:, c * kc_len:(c + 1) * kc_len]
            for h, (side, kv) in enumerate(heads):
                nxt[h] = _dot_t(kn, qn[h])
                m, acc = state[h]
                s = cur[h]
                m_new = jnp.maximum(m, jnp.max(s, axis=0, keepdims=True))
                p = jnp.exp2(s - m_new).astype(BF16)
                acc = jnp.exp2(m - m_new) * acc + _dot(vt[kv * VT_ROWS:(kv + 1) * VT_ROWS], p)
                state[h] = (m_new, acc)
        outs = [acc[0:HEAD_DIM] / acc[HEAD_DIM:HEAD_DIM + 1] for _, acc in state]
        row = pl.multiple_of(t * GQA_TQ, GQA_TQ)
        for half in range(2):
            o_t = jnp.concatenate([outs[heads.index((side, half))] for side in range(2)], axis=0)
            o_ref[pl.ds(row, GQA_TQ), half * KV_W:(half + 1) * KV_W] = o_t.T.astype(BF16)
        return carry

    jax.lax.fori_loop(0, nqt, qtile, 0)


def _gqa_lat(q, k, vt, ctx_k, ctx_vt, layer):
    B, L, _ = q.shape
    lc = ctx_k.shape[2]
    lk = L + lc
    nvt = GQA_KV_HEADS * VT_ROWS
    assert GQA_NCHUNK % 2 == 0 and lk % (GQA_NCHUNK * LANES) == 0 and L % GQA_TQ == 0 and L % LANES == 0
    return pl.pallas_call(
        _gqa_kernel,
        grid=(B,),
        in_specs=[pl.BlockSpec((None, L, BRANCH_W), lambda b: (b, 0, 0)),
                  pl.BlockSpec((None, L, KV_W), lambda b: (b, 0, 0)),
                  pl.BlockSpec((None, nvt, L), lambda b: (b, 0, 0)),
                  pl.BlockSpec((None, None, lc, KV_W), lambda b: (b, layer, 0, 0)),
                  pl.BlockSpec((None, None, nvt, lc), lambda b: (b, layer, 0, 0))],
        out_specs=pl.BlockSpec((None, L, BRANCH_W), lambda b: (b, 0, 0)),
        out_shape=jax.ShapeDtypeStruct((B, L, BRANCH_W), BF16),
        scratch_shapes=[pltpu.VMEM((lk, KV_W), BF16), pltpu.VMEM((nvt, lk), BF16)]
        + [pltpu.VMEM((GQA_HEADS, lk // GQA_NCHUNK, GQA_TQ), F32)] * 2,
        compiler_params=_cparams(1),
        name="gqa_lat",
    )(q, k, vt, ctx_k, ctx_vt)


def _ctx_prep_kernel(nk_ref, nv_ref, gk_ref, gv_ref, nk_o, nv_o, gk_o, gvt_o):
    nk_o[...] = nk_ref[...].astype(BF16)
    nv_o[...] = nv_ref[...].astype(BF16)
    gk_o[...] = gk_ref[...].astype(BF16)
    gvt = gv_ref[...].T.astype(BF16)
    ones = jnp.ones((VT_ROWS - HEAD_DIM, gvt.shape[1]), BF16)
    for kv in range(GQA_KV_HEADS):
        gvt_o[kv * VT_ROWS:kv * VT_ROWS + HEAD_DIM, :] = gvt[kv * HEAD_DIM:(kv + 1) * HEAD_DIM]
        gvt_o[kv * VT_ROWS + HEAD_DIM:(kv + 1) * VT_ROWS, :] = ones


def _ctx_prep(ck_n, cv_n, ck_g, cv_g):
    B, depth, lc, _ = ck_n.shape
    nvt = GQA_KV_HEADS * VT_ROWS
    blk = lambda r, c: pl.BlockSpec((None, None, r, c), lambda b, l: (b, l, 0, 0))
    return pl.pallas_call(
        _ctx_prep_kernel,
        grid=(B, depth),
        in_specs=[blk(lc, BRANCH_W), blk(lc, BRANCH_W), blk(lc, KV_W), blk(lc, KV_W)],
        out_specs=[blk(lc, BRANCH_W), blk(lc, BRANCH_W), blk(lc, KV_W), blk(nvt, lc)],
        out_shape=[jax.ShapeDtypeStruct((B, depth, lc, BRANCH_W), BF16)] * 2
        + [jax.ShapeDtypeStruct((B, depth, lc, KV_W), BF16), jax.ShapeDtypeStruct((B, depth, nvt, lc), BF16)],
        compiler_params=_cparams(2),
        name="ctx_prep",
    )(ck_n, cv_n, ck_g, cv_g)


OUT_TM = 2048
OUT_RC = 256


def _outproj_kernel(a_ref, n_ref, c_ref, g_ref, gates_ref, x_ref, gm_ref, w_ref, b_ref, lg_ref, lb_ref, o_ref):
    for r0 in range(0, x_ref.shape[0], OUT_RC):
        rows = slice(r0, r0 + OUT_RC)
        mixed = jnp.concatenate([a_ref[rows, :], n_ref[rows, :] * gates_ref[rows, 0:BRANCH_W],
                                 c_ref[rows, :], g_ref[rows, :] * gates_ref[rows, BRANCH_W:]], axis=1)
        out = _dot(mixed, w_ref[...]) + b_ref[...]
        y = DEEPNORM_ALPHA * x_ref[rows, :] + gm_ref[...] * out
        mu = jnp.mean(y, axis=-1, keepdims=True)
        d = y - mu
        var = jnp.mean(d * d, axis=-1, keepdims=True)
        o_ref[rows, :] = d * jax.lax.rsqrt(var + LN_EPS) * lg_ref[...] + lb_ref[...]


def _outproj(a, n, c, g, gates, x, mod, mod_row0, layer, w_out, b_out, ln_g, ln_b):
    B, L, D = x.shape
    tm = min(OUT_TM, L)
    assert L % tm == 0 and tm % OUT_RC == 0
    tok = lambda w: pl.BlockSpec((None, tm, w), lambda i, b: (b, i, 0))
    per_layer = lambda *shape: _layer_block(shape, layer)
    return pl.pallas_call(
        _outproj_kernel,
        grid=(L // tm, B),
        in_specs=[tok(BRANCH_W)] * 4 + [
            tok(2 * BRANCH_W), tok(D),
            pl.BlockSpec((None, None, 1, D), lambda i, b: (layer, mod_row0 + b, 0, 2)),
            per_layer(D, D), per_layer(1, D), per_layer(1, D), per_layer(1, D)],
        out_specs=tok(D),
        out_shape=jax.ShapeDtypeStruct((B, L, D), F32),
        compiler_params=_cparams(2),
        name="outproj",
    )(a, n, c, g, gates, x, mod, w_out, b_out, ln_g, ln_b)


def _rope_tables(seq_len):
    half = HEAD_DIM // 2
    nf = half // 2
    t = np.arange(seq_len)
    inv = np.float32(ROPE_THETA) ** (-np.arange(nf, dtype=np.float32) * np.float32(2.0) / np.float32(half))

    def tabs(pos):
        ang = pos.astype(np.float32)[:, None] * inv[None, :]
        return np.cos(ang), np.sin(ang)

    cr, sr = tabs(t // GRID_W)
    cc, sc = tabs(t % GRID_W)
    zero = np.zeros_like(sr)
    cos = np.concatenate([cr, cr, cc, cc], axis=1)
    sin_a = np.concatenate([-sr, zero, -sc, zero], axis=1)
    sin_b = np.concatenate([zero, sr, zero, sc], axis=1)
    return tuple(jnp.asarray(np.tile(a, (1, 2)), F32) for a in (cos, sin_a, sin_b))


def _nat_bias_tiles(bias_tab):
    col = np.arange(GRID_W)
    col_start = np.clip(col - NAT_COLS // 2, 0, GRID_W - NAT_COLS)
    col_in = (col[None, :] >= col_start[:, None]) & (col[None, :] < col_start[:, None] + NAT_COLS)
    edge = GRID_W - NAT_COLS
    n_diff = 2 * GRID_W - 1
    padded = jnp.concatenate([jnp.repeat(bias_tab[..., :1], edge, axis=-1), bias_tab,
                              jnp.repeat(bias_tab[..., -1:], edge, axis=-1)], axis=-1)
    flat = jnp.tile(padded, (1, 1, 1, GRID_W + 1))
    toeplitz = flat[..., GRID_W - 1:GRID_W - 1 + GRID_W * (n_diff - 1)].reshape(
        bias_tab.shape[:-1] + (GRID_W, n_diff - 1))[..., :GRID_W]
    full = jnp.where(col_in, toeplitz * LOG2E, NEG_INF)
    pair = jnp.concatenate([full[:, :, :-1], full[:, :, 1:]], axis=-1)
    return jnp.transpose(pair, (0, 2, 1, 3, 4)).reshape(-1, 2 * NAT_ROWS - 2, NAT_HEADS * GRID_W, 2 * GRID_W)


def kernel(x_prompt, x_sample, c, cache_nat_k, cache_nat_v, cache_gqa_k, cache_gqa_v, c_ctx, w_mod, b_mod, w_in, b_in,
           pool_w, pool_scale, nat_bias, q_norm, k_norm, conv_w, conv_b, conv_ln_g, conv_ln_b, conv_pw, w_out, b_out,
           ln_g, ln_b):
    nb, seq, D = x_prompt.shape
    db, dseq, _ = x_sample.shape
    lc = cache_nat_k.shape[2]

    cond = jnp.zeros((MOD_ROWS, D), F32).at[0].set(c_ctx).at[1:1 + db].set(c)
    mod = _modulation(cond, w_mod, b_mod).reshape(DEPTH, MOD_ROWS, 1, 3 * D)

    def kernel_columns(a):
        return jnp.concatenate([a[..., c0:c0 + w] for _, c0, w in _IN_SEGMENTS], axis=-1)

    w_in_b = kernel_columns(w_in).astype(BF16)
    b_in_p = kernel_columns(b_in).reshape(DEPTH, 1, IN_WIDTH)
    w_out_b = w_out.astype(BF16)

    seg = jnp.asarray(np.kron(np.eye(GQA_HEADS), np.ones((HEAD_DIM, HEAD_DIM))), BF16)
    rope_tabs = _rope_tables(dseq)
    ck_n, cv_n, ck_g, cvt_g = _ctx_prep(cache_nat_k.reshape(db, DEPTH, lc, BRANCH_W),
                                        cache_nat_v.reshape(db, DEPTH, lc, BRANCH_W),
                                        cache_gqa_k.reshape(db, DEPTH, lc, KV_W),
                                        cache_gqa_v.reshape(db, DEPTH, lc, KV_W))

    row = lambda a: a.reshape(DEPTH, 1, -1)
    n_groups = len(POOL_WINDOWS)
    pool_bd = jnp.einsum('gh,lgcd->lgchd', jnp.eye(n_groups, dtype=F32), pool_w).reshape(DEPTH, BRANCH_W, BRANCH_W)
    in_w = (w_in_b, b_in_p, row(jnp.tile(q_norm, (1, GQA_HEADS))), row(jnp.tile(k_norm, (1, GQA_KV_HEADS))), seg)
    local_w = (pool_bd.astype(BF16), row(pool_scale), conv_w, row(conv_b), row(conv_ln_g), row(conv_ln_b),
               conv_pw.astype(BF16))
    out_w = (w_out_b, row(b_out), row(ln_g), row(ln_b))
    bias_tiles = _nat_bias_tiles(nat_bias)

    y_p = x_prompt.reshape(1, nb * seq, D)
    y_s = x_sample
    new_cache = [jnp.zeros((nb, DEPTH, seq, w), F32) for w in (BRANCH_W, BRANCH_W, KV_W, KV_W)]
    for l in range(DEPTH):
        a_out, c_out, gates, nat, gqa, *new_cache = _inproj(y_p, mod, 0, l, *in_w, local_w, None, new_cache, seq)
        n_out, g_out = _attn_ctx(nat.reshape(nb, seq, -1), gqa.reshape(nb, seq, -1))
        flat = lambda a: a.reshape(1, nb * seq, BRANCH_W)
        y_p = _outproj(a_out, flat(n_out), c_out, flat(g_out), gates, y_p, mod, 0, l, *out_w)

        a_out, c_out, gates, nat, gq, gk, gvt = _inproj(y_s, mod, 1, l, *in_w, local_w, rope_tabs, None, PROJ_TM)
        n_out = _nat_lat(nat, ck_n, cv_n, l, bias_tiles)
        g_out = _gqa_lat(gq, gk, gvt, ck_g, cvt_g, l)
        y_s = _outproj(a_out, n_out, c_out, g_out, gates, y_s, mod, 1, l, *out_w)

    heads = (NAT_HEADS, NAT_HEADS, GQA_KV_HEADS, GQA_KV_HEADS)
    return (y_p.reshape(nb, seq, D), y_s) + tuple(c.reshape(nb, DEPTH, seq, h, HEAD_DIM)
                                                  for c, h in zip(new_cache, heads))
```

```python
import functools

import numpy as np
import jax
import jax.numpy as jnp
from jax.experimental import pallas as pl
from jax.experimental.pallas import tpu as pltpu

F32 = jnp.float32
BF16 = jnp.bfloat16
LANES = 128
SUBLANES = 8

D_MODEL = 1024
DEPTH = 4
GRID_W = 64
HEAD_DIM = 64
BRANCH_W = D_MODEL // 4
POOL_WINDOWS = (2, 4, 8, 16)
POOL_GROUP_W = BRANCH_W // len(POOL_WINDOWS)
NAT_HEADS = BRANCH_W // HEAD_DIM
NAT_ROWS = 8
NAT_COLS = 16
CONV_WIDTH = 31
GQA_HEADS = BRANCH_W // HEAD_DIM
GQA_KV_HEADS = GQA_HEADS // 2
KV_W = GQA_KV_HEADS * HEAD_DIM
ROPE_THETA = 10000.0
LN_EPS = 1e-5
RMS_EPS = 1e-6
NEG_INF = -1e30
DEEPNORM_ALPHA = (2 * DEPTH) ** 0.25
LOG2E = 1.4426950408889634
Q_SCALE = HEAD_DIM ** -0.5 * LOG2E
VT_ROWS = 80

_MODEL_COLUMNS = (("ax", BRANCH_W), ("ag", BRANCH_W), ("nq", BRANCH_W), ("nk", BRANCH_W), ("nv", BRANCH_W),
                  ("ng", BRANCH_W), ("cv", BRANCH_W), ("cglu", BRANCH_W), ("cg", BRANCH_W),
                  ("gq", GQA_HEADS * HEAD_DIM), ("gk", KV_W), ("gv", KV_W), ("gg", BRANCH_W))
_MODEL_OFFSET = dict(zip((n for n, _ in _MODEL_COLUMNS), np.cumsum([0] + [w for _, w in _MODEL_COLUMNS])[:-1].tolist()))
_GQA_HEAD_ORDER = (0, 2, 1, 3)
_IN_SEGMENTS = (
    [("gq%d" % h, _MODEL_OFFSET["gq"] + h * HEAD_DIM, HEAD_DIM) for h in _GQA_HEAD_ORDER]
    + [(n, _MODEL_OFFSET[n], dict(_MODEL_COLUMNS)[n])
       for n in ("gk", "gv", "ax", "cv", "cglu", "ag", "ng", "cg", "gg", "nq", "nk", "nv")])
IN_WIDTH = sum(w for _, _, w in _IN_SEGMENTS)
_IN_OFFSET = dict(zip((n for n, _, _ in _IN_SEGMENTS), np.cumsum([0] + [w for _, _, w in _IN_SEGMENTS])[:-1].tolist()))
C_GQ, C_GK, C_GV = _IN_OFFSET["gq0"], _IN_OFFSET["gk"], _IN_OFFSET["gv"]
C_AX, C_CV, C_CGLU = _IN_OFFSET["ax"], _IN_OFFSET["cv"], _IN_OFFSET["cglu"]
C_AG, C_NG, C_CG, C_GG = _IN_OFFSET["ag"], _IN_OFFSET["ng"], _IN_OFFSET["cg"], _IN_OFFSET["gg"]
C_NQ, C_NK, C_NV = _IN_OFFSET["nq"], _IN_OFFSET["nk"], _IN_OFFSET["nv"]

ROPE_PAIR = HEAD_DIM // 4
PROJ_TM = 512
MOD_TN = 1024
HALO = 16
VMEM_LIMIT = 56 * 1024 * 1024
MOD_ROWS = 16


def _cparams(n_axes):
    return pltpu.CompilerParams(dimension_semantics=("arbitrary",) * n_axes, vmem_limit_bytes=VMEM_LIMIT)


def _layer_block(shape, layer):
    return pl.BlockSpec((None,) + tuple(shape), lambda *_: (layer,) + (0,) * len(shape))


def _silu(x):
    return x * jax.nn.sigmoid(x)


def _dot(a, b):
    return jnp.dot(a, b, preferred_element_type=F32)


def _dot_t(a, b):
    return jax.lax.dot_general(a, b, (((1,), (1,)), ((), ())), preferred_element_type=F32)


def _mod_kernel(cond_ref, w_ref, b_ref, o_ref):
    a = _silu(cond_ref[...]).astype(BF16)
    o_ref[...] = _dot(a, w_ref[...].astype(BF16)) + b_ref[...]


def _modulation(cond, w_mod, b_mod):
    return pl.pallas_call(
        _mod_kernel,
        grid=(DEPTH, 3 * D_MODEL // MOD_TN),
        in_specs=[
            pl.BlockSpec((MOD_ROWS, D_MODEL), lambda l, j: (0, 0)),
            pl.BlockSpec((None, D_MODEL, MOD_TN), lambda l, j: (l, 0, j)),
            pl.BlockSpec((None, 1, MOD_TN), lambda l, j: (l, 0, j)),
        ],
        out_specs=pl.BlockSpec((None, MOD_ROWS, MOD_TN), lambda l, j: (l, 0, j)),
        out_shape=jax.ShapeDtypeStruct((DEPTH, MOD_ROWS, 3 * D_MODEL), F32),
        compiler_params=_cparams(2),
        name="modulation",
    )(cond, w_mod, b_mod.reshape(DEPTH, 1, 3 * D_MODEL))


def _head_rms(x, seg, gain):
    tot = _dot((x * x).astype(BF16), seg)
    return x * jax.lax.rsqrt(tot * (1.0 / HEAD_DIM) + RMS_EPS) * gain


def _rope(x, cos, sin_a, sin_b):
    n = x.shape[-1]
    return x * cos + pltpu.roll(x, n - ROPE_PAIR, 1) * sin_a + pltpu.roll(x, ROPE_PAIR, 1) * sin_b


def _inproj_kernel(*refs, latent, seq_tiles, seq_len):
    (x_ref, xp_ref, xn_ref, sh_ref, sc_ref, w_ref, b_ref, qn_ref, kn_ref, seg_ref,
     pw_ref, ps_ref, cw_ref, cb_ref, lg_ref, lb_ref, cpw_ref) = refs[:17]
    refs = refs[17:]
    if latent:
        cos_ref, sa_ref, sb_ref = refs[:3]
        refs = refs[3:]
    else:
        refs = refs[4:]
    a_out_ref, c_out_ref, gates_ref, nat_ref, gqa_ref = refs[:5]
    refs = refs[5:]
    abuf, zbuf, ybuf, cbuf, ubuf = refs[-5:]
    refs = refs[:-5]
    tm = x_ref.shape[0]
    tile = pl.program_id(0) % seq_tiles

    def put_cache(ref, val):
        seq = ref.shape[1]
        for sidx in range(ref.shape[0]):
            ref[sidx] = val[sidx * seq:(sidx + 1) * seq]

    u_ref = refs[-1]
    refs = refs[:-1]

    def modulate(x):
        return (x * (1.0 + sc_ref[...]) + sh_ref[...]).astype(BF16)

    h = modulate(x_ref[...])
    h_ext = jnp.concatenate([modulate(xp_ref[...]), h, modulate(xn_ref[...])], axis=0)

    def proj(c0, n):
        return u_ref[:, c0:c0 + n] + b_ref[:, c0:c0 + n]

    loc = _dot(h_ext, w_ref[:, C_AX:C_AG]) + b_ref[:, C_AX:C_AG]
    row = jax.lax.broadcasted_iota(jnp.int32, (tm + 2 * HALO, 1), 0)
    inside = ((row >= HALO) | (tile > 0)) & ((row < HALO + tm) | (tile < seq_tiles - 1))
    abuf[...] = jnp.where(inside, loc[:, 0:BRANCH_W], 0.0)
    zbuf[...] = jnp.where(inside, loc[:, BRANCH_W:2 * BRANCH_W] * jax.nn.sigmoid(loc[:, 2 * BRANCH_W:]), 0.0)
    pieces = _local_mixers(abuf, zbuf, ybuf, cbuf, ubuf, pw_ref, ps_ref, cw_ref, cb_ref, lg_ref, lb_ref, cpw_ref,
                           tile * tm, seq_len)
    result = []

    def advance(n):
        for _ in range(n):
            if not result:
                try:
                    next(pieces)
                except StopIteration as done:
                    result.append(done.value)

    u_ref[:, :C_AX] = _dot(h, w_ref[:, :C_AX])
    advance(SUBLANES)
    u_ref[:, C_AG:] = _dot(h, w_ref[:, C_AG:])

    gq = _head_rms(proj(C_GQ, BRANCH_W), seg_ref[...], qn_ref[...])
    gk = _head_rms(proj(C_GK, KV_W), seg_ref[0:KV_W, 0:KV_W], kn_ref[...])
    gv = proj(C_GV, KV_W)
    if not latent:
        nk_ref, nv_ref, gk_ref, gv_ref = refs
        put_cache(gk_ref, gk)
        put_cache(gv_ref, gv)
        gqa_ref[:, BRANCH_W:BRANCH_W + KV_W] = gk.astype(BF16)
        gqa_ref[:, BRANCH_W + KV_W:] = gv.astype(BF16)
    else:
        cos, sa, sb = cos_ref[...], sa_ref[...], sb_ref[...]
        gk = _rope(gk, cos, sa, sb)
        gq = _rope(gq, jnp.concatenate([cos, cos], axis=1), jnp.concatenate([sa, sa], axis=1),
                   jnp.concatenate([sb, sb], axis=1))
        k_ref, vt_ref = refs
        k_ref[...] = gk.astype(BF16)
        gvt = gv.T.astype(BF16)
        ones = jnp.ones((VT_ROWS - HEAD_DIM, gvt.shape[1]), BF16)
        for kv in range(GQA_KV_HEADS):
            vt_ref[kv * VT_ROWS:kv * VT_ROWS + HEAD_DIM, :] = gvt[kv * HEAD_DIM:(kv + 1) * HEAD_DIM]
            vt_ref[kv * VT_ROWS + HEAD_DIM:(kv + 1) * VT_ROWS, :] = ones
    gqa_ref[:, 0:BRANCH_W] = (gq * Q_SCALE).astype(BF16)

    while not result:
        advance(1)
    a, c = result[0]
    a_out_ref[...] = (a * _silu(proj(C_AG, BRANCH_W))).astype(BF16)
    c_out_ref[...] = (c * _silu(proj(C_CG, BRANCH_W))).astype(BF16)
    gates_ref[:, 0:BRANCH_W] = _silu(proj(C_NG, BRANCH_W)).astype(BF16)
    gates_ref[:, BRANCH_W:] = _silu(proj(C_GG, BRANCH_W)).astype(BF16)

    nat_ref[:, 0:BRANCH_W] = (proj(C_NQ, BRANCH_W) * Q_SCALE).astype(BF16)
    nk = proj(C_NK, BRANCH_W)
    nv = proj(C_NV, BRANCH_W)
    nat_ref[:, BRANCH_W:2 * BRANCH_W] = nk.astype(BF16)
    nat_ref[:, 2 * BRANCH_W:] = nv.astype(BF16)
    if not latent:
        put_cache(nk_ref, nk)
        put_cache(nv_ref, nv)


def _inproj(x, mod, mod_row0, layer, w_in, b_in, qn, kn, seg, local_w, rope_tabs, new_cache, tm):
    B, L, D = x.shape
    nt = L // tm
    latent = rope_tabs is not None
    seq_tiles = nt if latent else 1
    hb = tm // HALO
    nhb = L // HALO
    tok = lambda w: pl.BlockSpec((None, tm, w), lambda i, b: (b, i, 0))
    per_layer = lambda *shape: _layer_block(shape, layer)
    C = BRANCH_W
    in_specs = [
        tok(D),
        pl.BlockSpec((None, HALO, D), lambda i, b: (b, jnp.maximum(i * hb - 1, 0), 0)),
        pl.BlockSpec((None, HALO, D), lambda i, b: (b, jnp.minimum((i + 1) * hb, nhb - 1), 0)),
        pl.BlockSpec((None, None, 1, D), lambda i, b: (layer, mod_row0 + b, 0, 0)),
        pl.BlockSpec((None, None, 1, D), lambda i, b: (layer, mod_row0 + b, 0, 1)),
        per_layer(D, IN_WIDTH), per_layer(1, IN_WIDTH), per_layer(1, BRANCH_W), per_layer(1, KV_W),
        pl.BlockSpec((BRANCH_W, BRANCH_W), lambda i, b: (0, 0)),
        per_layer(C, C), per_layer(1, C), per_layer(CONV_WIDTH, C), per_layer(1, C), per_layer(1, C), per_layer(1, C),
        per_layer(C, C),
    ]
    args = [x, x, x, mod, mod, w_in, b_in, qn, kn, seg, *local_w]
    if latent:
        in_specs += [pl.BlockSpec((tm, KV_W), lambda i, b: (i, 0))] * 3
        args += list(rope_tabs)
    gqa_w = BRANCH_W if latent else BRANCH_W + 2 * KV_W
    out_specs = [tok(BRANCH_W), tok(BRANCH_W), tok(2 * BRANCH_W), tok(3 * BRANCH_W), tok(gqa_w)]
    out_shape = [jax.ShapeDtypeStruct((B, L, BRANCH_W), BF16), jax.ShapeDtypeStruct((B, L, BRANCH_W), BF16),
                 jax.ShapeDtypeStruct((B, L, 2 * BRANCH_W), BF16), jax.ShapeDtypeStruct((B, L, 3 * BRANCH_W), BF16),
                 jax.ShapeDtypeStruct((B, L, gqa_w), BF16)]
    if latent:
        out_specs += [tok(KV_W), pl.BlockSpec((None, GQA_KV_HEADS * VT_ROWS, tm), lambda i, b: (b, 0, i))]
        out_shape += [jax.ShapeDtypeStruct((B, L, KV_W), BF16),
                      jax.ShapeDtypeStruct((B, GQA_KV_HEADS * VT_ROWS, L), BF16)]
        aliases = {}
    else:
        seq = new_cache[0].shape[2]
        assert B == 1 and tm == seq
        aliases = {len(args) + j: len(out_specs) + j for j in range(len(new_cache))}
        in_specs += [pl.BlockSpec(memory_space=pl.ANY)] * len(new_cache)
        args += list(new_cache)
        out_specs += [pl.BlockSpec((1, None, seq, c.shape[3]), lambda i, b: (i, layer, 0, 0)) for c in new_cache]
        out_shape += [jax.ShapeDtypeStruct(c.shape, c.dtype) for c in new_cache]
    return pl.pallas_call(
        functools.partial(_inproj_kernel, latent=latent, seq_tiles=seq_tiles, seq_len=L if latent else tm),
        grid=(nt, B),
        in_specs=in_specs,
        out_specs=out_specs,
        out_shape=out_shape,
        input_output_aliases=aliases,
        scratch_shapes=[pltpu.VMEM((tm, IN_WIDTH), F32),
                        pltpu.VMEM((tm + 2 * HALO, C), F32), pltpu.VMEM((tm + 2 * HALO, C), F32),
                        pltpu.VMEM((tm, C), F32), pltpu.VMEM((tm, C), F32),
                        pltpu.VMEM((SUBLANES, tm + SUBLANES, C), F32)],
        compiler_params=_cparams(2),
        name="inproj_lat" if latent else "inproj_ctx",
    )(*args)


LOCAL_RC = 64


CONV_CHUNK_MAX = 104


def _conv_chunk(rows):
    return max(c for c in range(SUBLANES, CONV_CHUNK_MAX + 1, SUBLANES) if rows % c == 0)


def _local_mixers(abuf, zbuf, ybuf, cbuf, ubuf, pw_ref, ps_ref, cw_ref, cb_ref, lg_ref, lb_ref, cpw_ref, t0, seq_len):
    tm = ybuf.shape[0]
    LOCAL_UC = _conv_chunk(tm + SUBLANES)

    half_w = CONV_WIDTH // 2
    for res in range(SUBLANES):
        taps = [d for d in range(res - HALO, HALO, SUBLANES) if -half_w <= d <= half_w]
        for u0 in range(0, tm + SUBLANES, LOCAL_UC):
            part = None
            for d in taps:
                term = zbuf[HALO + u0 + d - res:HALO + u0 + d - res + LOCAL_UC, :] * cw_ref[d + half_w:d + half_w + 1, :]
                part = term if part is None else part + term
            ubuf[res, u0:u0 + LOCAL_UC, :] = part
        yield

    assert 2 * POOL_GROUP_W == LANES and POOL_WINDOWS == (2, 4, 8, 16)
    lane = jax.lax.broadcasted_iota(jnp.int32, (LOCAL_RC, LANES), 1)
    low = lane < POOL_GROUP_W
    for r0 in range(0, tm, LOCAL_RC):
        t = t0 + r0 + jax.lax.broadcasted_iota(jnp.int32, (LOCAL_RC, LANES), 0)

        def sh(off, c0):
            return abuf[HALO + r0 + off:HALO + r0 + off + LOCAL_RC, c0:c0 + LANES]

        def centred(total, half, x):
            cnt = jnp.minimum(t + half, seq_len) - jnp.maximum(t - half, 0)
            return total / cnt.astype(F32) - x

        x0 = sh(0, 0)
        s2 = x0 + sh(-1, 0)
        s4 = s2 + sh(-2, 0) + sh(1, 0)
        ybuf[r0:r0 + LOCAL_RC, 0:LANES] = centred(jnp.where(low, s2, s4), jnp.where(low, 1, 2), x0)
        x1 = sh(0, LANES)
        s8 = x1
        for off in (-4, -3, -2, -1, 1, 2, 3):
            s8 = s8 + sh(off, LANES)
        s16 = s8
        for off in (-8, -7, -6, -5, 4, 5, 6, 7):
            s16 = s16 + sh(off, LANES)
        ybuf[r0:r0 + LOCAL_RC, LANES:] = centred(jnp.where(low, s8, s16), jnp.where(low, 4, 8), x1)

        acc = jnp.zeros((LOCAL_RC, BRANCH_W), F32) + cb_ref[...]
        for res in range(SUBLANES):
            acc = acc + ubuf[res, r0 + res:r0 + res + LOCAL_RC, :]
        cbuf[r0:r0 + LOCAL_RC, :] = acc
        yield

    a = _dot(ybuf[...].astype(BF16), pw_ref[...]) * ps_ref[...]

    cz = cbuf[...]
    mu = jnp.mean(cz, axis=-1, keepdims=True)
    d = cz - mu
    var = jnp.mean(d * d, axis=-1, keepdims=True)
    zn = _silu(d * jax.lax.rsqrt(var + LN_EPS) * lg_ref[...] + lb_ref[...])
    return a, _dot(zn.astype(BF16), cpw_ref[...])


def _head_mask(shape, h):
    lane = jax.lax.broadcasted_iota(jnp.int32, shape, 1)
    return (lane >= h * HEAD_DIM) & (lane < (h + 1) * HEAD_DIM)


def _softmax_pv(s, v):
    m = jnp.max(s, axis=-1, keepdims=True)
    p = jnp.exp2(s - m)
    l = jnp.sum(p, axis=-1, keepdims=True)
    return _dot(p.astype(BF16), v) / l


CTX_SEQS = 4


def _attn_ctx_kernel(nat_ref, gqa_ref, n_out_ref, g_out_ref):
    for s in range(nat_ref.shape[0]):
        _attn_ctx_seq(nat_ref.at[s], gqa_ref.at[s], n_out_ref.at[s], g_out_ref.at[s])


def _attn_ctx_seq(nat_ref, gqa_ref, n_out_ref, g_out_ref):
    q, k, v = (nat_ref[:, j * BRANCH_W:(j + 1) * BRANCH_W] for j in range(3))
    acc = jnp.zeros(q.shape, F32)
    for h in range(NAT_HEADS):
        msk = _head_mask(q.shape, h)
        o = _softmax_pv(_dot_t(jnp.where(msk, q, jnp.zeros_like(q)), k), v)
        acc = acc + jnp.where(msk, o, 0.0)
    n_out_ref[...] = acc.astype(BF16)

    k, v = gqa_ref[:, BRANCH_W:BRANCH_W + KV_W], gqa_ref[:, BRANCH_W + KV_W:]
    sides = []
    for side in range(2):
        q = gqa_ref[:, side * KV_W:(side + 1) * KV_W]
        acc = jnp.zeros(q.shape, F32)
        for kv in range(GQA_KV_HEADS):
            msk = _head_mask(q.shape, kv)
            o = _softmax_pv(_dot_t(jnp.where(msk, q, jnp.zeros_like(q)), k), v)
            acc = acc + jnp.where(msk, o, 0.0)
        sides.append(acc)
    low = _head_mask(sides[0].shape, 0)
    g_out_ref[:, 0:KV_W] = jnp.where(low, sides[0], pltpu.roll(sides[1], HEAD_DIM, 1)).astype(BF16)
    g_out_ref[:, KV_W:] = jnp.where(low, pltpu.roll(sides[0], HEAD_DIM, 1), sides[1]).astype(BF16)


def _attn_ctx(nat, gqa):
    B, L, _ = nat.shape
    ns = CTX_SEQS if B % CTX_SEQS == 0 else 1
    return pl.pallas_call(
        _attn_ctx_kernel,
        grid=(B // ns,),
        in_specs=[pl.BlockSpec((ns, L, 3 * BRANCH_W), lambda b: (b, 0, 0)),
                  pl.BlockSpec((ns, L, BRANCH_W + 2 * KV_W), lambda b: (b, 0, 0))],
        out_specs=[pl.BlockSpec((ns, L, BRANCH_W), lambda b: (b, 0, 0))] * 2,
        out_shape=[jax.ShapeDtypeStruct((B, L, BRANCH_W), BF16)] * 2,
        compiler_params=_cparams(1),
        name="attn_ctx",
    )(nat, gqa)


NAT_RB = 8
NAT_NLOC = NAT_ROWS * GRID_W


def _nat_kernel(q_ref, k_ref, v_ref, kc_ref, vc_ref, bias_ref, o_ref, s_a, s_b, *, rows):
    i = pl.program_id(1)
    kc = kc_ref[...]
    vc = vc_ref[...]
    nq = NAT_HEADS * GRID_W
    diag = (jax.lax.broadcasted_iota(jnp.int32, (nq, BRANCH_W), 0) // GRID_W
            == jax.lax.broadcasted_iota(jnp.int32, (nq, BRANCH_W), 1) // HEAD_DIM)
    bufs = (s_a, s_b)

    def window(rr):
        r = i * NAT_RB + rr
        row_start = jnp.clip(r - NAT_ROWS // 2, 0, rows - NAT_ROWS)
        return row_start - r + NAT_ROWS - 1, pl.multiple_of(row_start * GRID_W, GRID_W)

    def scores(rr, buf):
        off, base = window(rr)
        q_r = q_ref[rr * GRID_W:(rr + 1) * GRID_W, :]
        qs = jnp.where(diag, jnp.concatenate([q_r] * NAT_HEADS, axis=0), jnp.zeros((nq, BRANCH_W), BF16))
        bias = jnp.concatenate([bias_ref[off + 2 * jj] for jj in range(NAT_ROWS // 2)], axis=1)
        buf[:, 0:NAT_NLOC] = _dot_t(qs, k_ref[pl.ds(base, NAT_NLOC), :]) + bias
        buf[:, NAT_NLOC:] = _dot_t(qs, kc)

    scores(0, s_a)
    for rr in range(NAT_RB):
        cur = bufs[rr % 2]
        if rr + 1 < NAT_RB:
            scores(rr + 1, bufs[(rr + 1) % 2])
        _, base = window(rr)
        vw = v_ref[pl.ds(base, NAT_NLOC), :]
        m = jnp.max(cur[...], axis=-1, keepdims=True)
        p_loc = jnp.exp2(cur[:, 0:NAT_NLOC] - m)
        p_ctx = jnp.exp2(cur[:, NAT_NLOC:] - m)
        l = jnp.sum(p_loc, axis=-1, keepdims=True) + jnp.sum(p_ctx, axis=-1, keepdims=True)
        o = (_dot(p_loc.astype(BF16), vw) + _dot(p_ctx.astype(BF16), vc)) / l
        o = jnp.where(diag, o, 0.0)
        o_r = o[0:GRID_W]
        for h in range(1, NAT_HEADS):
            o_r = o_r + o[h * GRID_W:(h + 1) * GRID_W]
        o_ref[rr * GRID_W:(rr + 1) * GRID_W, :] = o_r.astype(BF16)


def _nat_lat(nat, ctx_k, ctx_v, layer, bias_tiles):
    B, L, _ = nat.shape
    rows = L // GRID_W
    lc = ctx_k.shape[2]
    tq = NAT_RB * GRID_W
    seq = lambda col: pl.BlockSpec((None, L, BRANCH_W), lambda b, i: (b, 0, col))
    ctx = pl.BlockSpec((None, None, lc, BRANCH_W), lambda b, i: (b, layer, 0, 0))
    return pl.pallas_call(
        functools.partial(_nat_kernel, rows=rows),
        grid=(B, rows // NAT_RB),
        in_specs=[pl.BlockSpec((None, tq, BRANCH_W), lambda b, i: (b, i, 0)), seq(1), seq(2), ctx, ctx,
                  _layer_block(bias_tiles.shape[1:], layer)],
        out_specs=pl.BlockSpec((None, tq, BRANCH_W), lambda b, i: (b, i, 0)),
        out_shape=jax.ShapeDtypeStruct((B, L, BRANCH_W), BF16),
        scratch_shapes=[pltpu.VMEM((NAT_HEADS * GRID_W, NAT_NLOC + lc), F32)] * 2,
        compiler_params=_cparams(2),
        name="nat_lat",
    )(nat, nat, nat, ctx_k, ctx_v, bias_tiles)


GQA_TQ = 256
GQA_NCHUNK = 6


def _gqa_kernel(q_ref, k_lat_ref, vt_lat_ref, k_ctx_ref, vt_ctx_ref, o_ref, k_ref, vt_ref, s_a, s_b):
    n_lat = k_lat_ref.shape[0]
    k_ref[0:n_lat, :] = k_lat_ref[...]
    k_ref[n_lat:, :] = k_ctx_ref[...]
    vt_ref[:, 0:n_lat] = vt_lat_ref[...]
    vt_ref[:, n_lat:] = vt_ctx_ref[...]
    kc_len = k_ref.shape[0] // GQA_NCHUNK
    nqt = q_ref.shape[0] // GQA_TQ
    heads = [(side, kv) for side in range(2) for kv in range(GQA_KV_HEADS)]
    bufs = (s_a, s_b)

    def masked_q(t):
        row = pl.multiple_of(t * GQA_TQ, GQA_TQ)
        out = []
        for side, kv in heads:
            q = q_ref[pl.ds(row, GQA_TQ), side * KV_W:(side + 1) * KV_W]
            out.append(jnp.where(_head_mask(q.shape, kv), q, jnp.zeros_like(q)))
        return out

    qh0 = masked_q(0)
    for h in range(len(heads)):
        s_a[h] = _dot_t(k_ref[0:kc_len, :], qh0[h])

    def qtile(t, carry):
        qh_cur = masked_q(t)
        qh_next = masked_q(jnp.minimum(t + 1, nqt - 1))
        state = [(jnp.full((1, GQA_TQ), NEG_INF, F32), jnp.zeros((VT_ROWS, GQA_TQ), F32)) for _ in heads]
        for c in range(GQA_NCHUNK):
            cur, nxt = bufs[c % 2], bufs[(c + 1) % 2]
            cn = (c + 1) % GQA_NCHUNK
            qn = qh_cur if c + 1 < GQA_NCHUNK else qh_next
            kn = k_ref[cn * kc_len:(cn + 1) * kc_len, :]
            vt = vt_ref[:, c * kc_len:(c + 1) * kc_len]
            for h, (side, kv) in enumerate(heads):
                nxt[h] = _dot_t(kn, qn[h])
                m, acc = state[h]
                s = cur[h]
                m_new = jnp.maximum(m, jnp.max(s, axis=0, keepdims=True))
                p = jnp.exp2(s - m_new).astype(BF16)
                acc = jnp.exp2(m - m_new) * acc + _dot(vt[kv * VT_ROWS:(kv + 1) * VT_ROWS], p)
                state[h] = (m_new, acc)
        outs = [acc[0:HEAD_DIM] / acc[HEAD_DIM:HEAD_DIM + 1] for _, acc in state]
        row = pl.multiple_of(t * GQA_TQ, GQA_TQ)
        for half in range(2):
            o_t = jnp.concatenate([outs[heads.index((side, half))] for side in range(2)], axis=0)
            o_ref[pl.ds(row, GQA_TQ), half * KV_W:(half + 1) * KV_W] = o_t.T.astype(BF16)
        return carry

    jax.lax.fori_loop(0, nqt, qtile, 0)


def _gqa_lat(q, k, vt, ctx_k, ctx_vt, layer):
    B, L, _ = q.shape
    lc = ctx_k.shape[2]
    lk = L + lc
    nvt = GQA_KV_HEADS * VT_ROWS
    assert GQA_NCHUNK % 2 == 0 and lk % (GQA_NCHUNK * LANES) == 0 and L % GQA_TQ == 0 and L % LANES == 0
    return pl.pallas_call(
        _gqa_kernel,
        grid=(B,),
        in_specs=[pl.BlockSpec((None, L, BRANCH_W), lambda b: (b, 0, 0)),
                  pl.BlockSpec((None, L, KV_W), lambda b: (b, 0, 0)),
                  pl.BlockSpec((None, nvt, L), lambda b: (b, 0, 0)),
                  pl.BlockSpec((None, None, lc, KV_W), lambda b: (b, layer, 0, 0)),
                  pl.BlockSpec((None, None, nvt, lc), lambda b: (b, layer, 0, 0))],
        out_specs=pl.BlockSpec((None, L, BRANCH_W), lambda b: (b, 0, 0)),
        out_shape=jax.ShapeDtypeStruct((B, L, BRANCH_W), BF16),
        scratch_shapes=[pltpu.VMEM((lk, KV_W), BF16), pltpu.VMEM((nvt, lk), BF16)]
        + [pltpu.VMEM((GQA_HEADS, lk // GQA_NCHUNK, GQA_TQ), F32)] * 2,
        compiler_params=_cparams(1),
        name="gqa_lat",
    )(q, k, vt, ctx_k, ctx_vt)


def _ctx_prep_kernel(nk_ref, nv_ref, gk_ref, gv_ref, nk_o, nv_o, gk_o, gvt_o):
    nk_o[...] = nk_ref[...].astype(BF16)
    nv_o[...] = nv_ref[...].astype(BF16)
    gk_o[...] = gk_ref[...].astype(BF16)
    gvt = gv_ref[...].T.astype(BF16)
    ones = jnp.ones((VT_ROWS - HEAD_DIM, gvt.shape[1]), BF16)
    for kv in range(GQA_KV_HEADS):
        gvt_o[kv * VT_ROWS:kv * VT_ROWS + HEAD_DIM, :] = gvt[kv * HEAD_DIM:(kv + 1) * HEAD_DIM]
        gvt_o[kv * VT_ROWS + HEAD_DIM:(kv + 1) * VT_ROWS, :] = ones


def _ctx_prep(ck_n, cv_n, ck_g, cv_g):
    B, depth, lc, _ = ck_n.shape
    nvt = GQA_KV_HEADS * VT_ROWS
    blk = lambda r, c: pl.BlockSpec((None, None, r, c), lambda b, l: (b, l, 0, 0))
    return pl.pallas_call(
        _ctx_prep_kernel,
        grid=(B, depth),
        in_specs=[blk(lc, BRANCH_W), blk(lc, BRANCH_W), blk(lc, KV_W), blk(lc, KV_W)],
        out_specs=[blk(lc, BRANCH_W), blk(lc, BRANCH_W), blk(lc, KV_W), blk(nvt, lc)],
        out_shape=[jax.ShapeDtypeStruct((B, depth, lc, BRANCH_W), BF16)] * 2
        + [jax.ShapeDtypeStruct((B, depth, lc, KV_W), BF16), jax.ShapeDtypeStruct((B, depth, nvt, lc), BF16)],
        compiler_params=_cparams(2),
        name="ctx_prep",
    )(ck_n, cv_n, ck_g, cv_g)


OUT_TM = 2048
OUT_RC = 256


def _outproj_kernel(a_ref, n_ref, c_ref, g_ref, gates_ref, x_ref, gm_ref, w_ref, b_ref, lg_ref, lb_ref, o_ref):
    for r0 in range(0, x_ref.shape[0], OUT_RC):
        rows = slice(r0, r0 + OUT_RC)
        mixed = jnp.concatenate([a_ref[rows, :], n_ref[rows, :] * gates_ref[rows, 0:BRANCH_W],
                                 c_ref[rows, :], g_ref[rows, :] * gates_ref[rows, BRANCH_W:]], axis=1)
        out = _dot(mixed, w_ref[...]) + b_ref[...]
        y = DEEPNORM_ALPHA * x_ref[rows, :] + gm_ref[...] * out
        mu = jnp.mean(y, axis=-1, keepdims=True)
        d = y - mu
        var = jnp.mean(d * d, axis=-1, keepdims=True)
        o_ref[rows, :] = d * jax.lax.rsqrt(var + LN_EPS) * lg_ref[...] + lb_ref[...]


def _outproj(a, n, c, g, gates, x, mod, mod_row0, layer, w_out, b_out, ln_g, ln_b):
    B, L, D = x.shape
    tm = min(OUT_TM, L)
    assert L % tm == 0 and tm % OUT_RC == 0
    tok = lambda w: pl.BlockSpec((None, tm, w), lambda i, b: (b, i, 0))
    per_layer = lambda *shape: _layer_block(shape, layer)
    return pl.pallas_call(
        _outproj_kernel,
        grid=(L // tm, B),
        in_specs=[tok(BRANCH_W)] * 4 + [
            tok(2 * BRANCH_W), tok(D),
            pl.BlockSpec((None, None, 1, D), lambda i, b: (layer, mod_row0 + b, 0, 2)),
            per_layer(D, D), per_layer(1, D), per_layer(1, D), per_layer(1, D)],
        out_specs=tok(D),
        out_shape=jax.ShapeDtypeStruct((B, L, D), F32),
        compiler_params=_cparams(2),
        name="outproj",
    )(a, n, c, g, gates, x, mod, w_out, b_out, ln_g, ln_b)


def _rope_tables(seq_len):
    half = HEAD_DIM // 2
    nf = half // 2
    t = np.arange(seq_len)
    inv = np.float32(ROPE_THETA) ** (-np.arange(nf, dtype=np.float32) * np.float32(2.0) / np.float32(half))

    def tabs(pos):
        ang = pos.astype(np.float32)[:, None] * inv[None, :]
        return np.cos(ang), np.sin(ang)

    cr, sr = tabs(t // GRID_W)
    cc, sc = tabs(t % GRID_W)
    zero = np.zeros_like(sr)
    cos = np.concatenate([cr, cr, cc, cc], axis=1)
    sin_a = np.concatenate([-sr, zero, -sc, zero], axis=1)
    sin_b = np.concatenate([zero, sr, zero, sc], axis=1)
    return tuple(jnp.asarray(np.tile(a, (1, 2)), F32) for a in (cos, sin_a, sin_b))


def _nat_bias_tiles(bias_tab):
    col = np.arange(GRID_W)
    col_start = np.clip(col - NAT_COLS // 2, 0, GRID_W - NAT_COLS)
    col_in = (col[None, :] >= col_start[:, None]) & (col[None, :] < col_start[:, None] + NAT_COLS)
    edge = GRID_W - NAT_COLS
    n_diff = 2 * GRID_W - 1
    padded = jnp.concatenate([jnp.repeat(bias_tab[..., :1], edge, axis=-1), bias_tab,
                              jnp.repeat(bias_tab[..., -1:], edge, axis=-1)], axis=-1)
    flat = jnp.tile(padded, (1, 1, 1, GRID_W + 1))
    toeplitz = flat[..., GRID_W - 1:GRID_W - 1 + GRID_W * (n_diff - 1)].reshape(
        bias_tab.shape[:-1] + (GRID_W, n_diff - 1))[..., :GRID_W]
    full = jnp.where(col_in, toeplitz * LOG2E, NEG_INF)
    pair = jnp.concatenate([full[:, :, :-1], full[:, :, 1:]], axis=-1)
    return jnp.transpose(pair, (0, 2, 1, 3, 4)).reshape(-1, 2 * NAT_ROWS - 2, NAT_HEADS * GRID_W, 2 * GRID_W)


def kernel(x_prompt, x_sample, c, cache_nat_k, cache_nat_v, cache_gqa_k, cache_gqa_v, c_ctx, w_mod, b_mod, w_in, b_in,
           pool_w, pool_scale, nat_bias, q_norm, k_norm, conv_w, conv_b, conv_ln_g, conv_ln_b, conv_pw, w_out, b_out,
           ln_g, ln_b):
    nb, seq, D = x_prompt.shape
    db, dseq, _ = x_sample.shape
    lc = cache_nat_k.shape[2]

    cond = jnp.zeros((MOD_ROWS, D), F32).at[0].set(c_ctx).at[1:1 + db].set(c)
    mod = _modulation(cond, w_mod, b_mod).reshape(DEPTH, MOD_ROWS, 1, 3 * D)

    def kernel_columns(a):
        return jnp.concatenate([a[..., c0:c0 + w] for _, c0, w in _IN_SEGMENTS], axis=-1)

    w_in_b = kernel_columns(w_in).astype(BF16)
    b_in_p = kernel_columns(b_in).reshape(DEPTH, 1, IN_WIDTH)
    w_out_b = w_out.astype(BF16)

    seg = jnp.asarray(np.kron(np.eye(GQA_HEADS), np.ones((HEAD_DIM, HEAD_DIM))), BF16)
    rope_tabs = _rope_tables(dseq)
    ck_n, cv_n, ck_g, cvt_g = _ctx_prep(cache_nat_k.reshape(db, DEPTH, lc, BRANCH_W),
                                        cache_nat_v.reshape(db, DEPTH, lc, BRANCH_W),
                                        cache_gqa_k.reshape(db, DEPTH, lc, KV_W),
                                        cache_gqa_v.reshape(db, DEPTH, lc, KV_W))

    row = lambda a: a.reshape(DEPTH, 1, -1)
    n_groups = len(POOL_WINDOWS)
    pool_bd = jnp.einsum('gh,lgcd->lgchd', jnp.eye(n_groups, dtype=F32), pool_w).reshape(DEPTH, BRANCH_W, BRANCH_W)
    in_w = (w_in_b, b_in_p, row(jnp.tile(q_norm, (1, GQA_HEADS))), row(jnp.tile(k_norm, (1, GQA_KV_HEADS))), seg)
    local_w = (pool_bd.astype(BF16), row(pool_scale), conv_w, row(conv_b), row(conv_ln_g), row(conv_ln_b),
               conv_pw.astype(BF16))
    out_w = (w_out_b, row(b_out), row(ln_g), row(ln_b))
    bias_tiles = _nat_bias_tiles(nat_bias)

    y_p = x_prompt.reshape(1, nb * seq, D)
    y_s = x_sample
    new_cache = [jnp.zeros((nb, DEPTH, seq, w), F32) for w in (BRANCH_W, BRANCH_W, KV_W, KV_W)]
    for l in range(DEPTH):
        a_out, c_out, gates, nat, gqa, *new_cache = _inproj(y_p, mod, 0, l, *in_w, local_w, None, new_cache, seq)
        n_out, g_out = _attn_ctx(nat.reshape(nb, seq, -1), gqa.reshape(nb, seq, -1))
        flat = lambda a: a.reshape(1, nb * seq, BRANCH_W)
        y_p = _outproj(a_out, flat(n_out), c_out, flat(g_out), gates, y_p, mod, 0, l, *out_w)

        a_out, c_out, gates, nat, gq, gk, gvt = _inproj(y_s, mod, 1, l, *in_w, local_w, rope_tabs, None, PROJ_TM)
        n_out = _nat_lat(nat, ck_n, cv_n, l, bias_tiles)
        g_out = _gqa_lat(gq, gk, gvt, ck_g, cvt_g, l)
        y_s = _outproj(a_out, n_out, c_out, g_out, gates, y_s, mod, 1, l, *out_w)

    heads = (NAT_HEADS, NAT_HEADS, GQA_KV_HEADS, GQA_KV_HEADS)
    return (y_p.reshape(nb, seq, D), y_s) + tuple(c.reshape(nb, DEPTH, seq, h, HEAD_DIM)
                                                  for c, h in zip(new_cache, heads))
```

```python
import functools

import numpy as np
import jax
import jax.numpy as jnp
from jax.experimental import pallas as pl
from jax.experimental.pallas import tpu as pltpu

F32 = jnp.float32
BF16 = jnp.bfloat16
LANES = 128
SUBLANES = 8

D_MODEL = 1024
DEPTH = 4
GRID_W = 64
HEAD_DIM = 64
BRANCH_W = D_MODEL // 4
POOL_WINDOWS = (2, 4, 8, 16)
POOL_GROUP_W = BRANCH_W // len(POOL_WINDOWS)
NAT_HEADS = BRANCH_W // HEAD_DIM
NAT_ROWS = 8
NAT_COLS = 16
CONV_WIDTH = 31
GQA_HEADS = BRANCH_W // HEAD_DIM
GQA_KV_HEADS = GQA_HEADS // 2
KV_W = GQA_KV_HEADS * HEAD_DIM
ROPE_THETA = 10000.0
LN_EPS = 1e-5
RMS_EPS = 1e-6
NEG_INF = -1e30
DEEPNORM_ALPHA = (2 * DEPTH) ** 0.25
LOG2E = 1.4426950408889634
Q_SCALE = HEAD_DIM ** -0.5 * LOG2E
VT_ROWS = 80

_MODEL_COLUMNS = (("ax", BRANCH_W), ("ag", BRANCH_W), ("nq", BRANCH_W), ("nk", BRANCH_W), ("nv", BRANCH_W),
                  ("ng", BRANCH_W), ("cv", BRANCH_W), ("cglu", BRANCH_W), ("cg", BRANCH_W),
                  ("gq", GQA_HEADS * HEAD_DIM), ("gk", KV_W), ("gv", KV_W), ("gg", BRANCH_W))
_MODEL_OFFSET = dict(zip((n for n, _ in _MODEL_COLUMNS), np.cumsum([0] + [w for _, w in _MODEL_COLUMNS])[:-1].tolist()))
_GQA_HEAD_ORDER = (0, 2, 1, 3)
_IN_SEGMENTS = (
    [("gq%d" % h, _MODEL_OFFSET["gq"] + h * HEAD_DIM, HEAD_DIM) for h in _GQA_HEAD_ORDER]
    + [(n, _MODEL_OFFSET[n], dict(_MODEL_COLUMNS)[n])
       for n in ("gk", "gv", "ax", "cv", "cglu", "ag", "ng", "cg", "gg", "nq", "nk", "nv")])
IN_WIDTH = sum(w for _, _, w in _IN_SEGMENTS)
_IN_OFFSET = dict(zip((n for n, _, _ in _IN_SEGMENTS), np.cumsum([0] + [w for _, _, w in _IN_SEGMENTS])[:-1].tolist()))
C_GQ, C_GK, C_GV = _IN_OFFSET["gq0"], _IN_OFFSET["gk"], _IN_OFFSET["gv"]
C_AX, C_CV, C_CGLU = _IN_OFFSET["ax"], _IN_OFFSET["cv"], _IN_OFFSET["cglu"]
C_AG, C_NG, C_CG, C_GG = _IN_OFFSET["ag"], _IN_OFFSET["ng"], _IN_OFFSET["cg"], _IN_OFFSET["gg"]
C_NQ, C_NK, C_NV = _IN_OFFSET["nq"], _IN_OFFSET["nk"], _IN_OFFSET["nv"]

ROPE_PAIR = HEAD_DIM // 4
PROJ_TM = 512
MOD_TN = 1024
HALO = 16
VMEM_LIMIT = 56 * 1024 * 1024
MOD_ROWS = 16


def _cparams(n_axes):
    return pltpu.CompilerParams(dimension_semantics=("arbitrary",) * n_axes, vmem_limit_bytes=VMEM_LIMIT)


def _layer_block(shape, layer):
    return pl.BlockSpec((None,) + tuple(shape), lambda *_: (layer,) + (0,) * len(shape),
                        pipeline_mode=pl.Buffered(1))


def _silu(x):
    return x * jax.nn.sigmoid(x)


def _dot(a, b):
    return jnp.dot(a, b, preferred_element_type=F32)


def _dot_t(a, b):
    return jax.lax.dot_general(a, b, (((1,), (1,)), ((), ())), preferred_element_type=F32)


def _mod_kernel(cond_ref, w_ref, b_ref, o_ref):
    a = _silu(cond_ref[...]).astype(BF16)
    o_ref[...] = _dot(a, w_ref[...].astype(BF16)) + b_ref[...]


def _modulation(cond, w_mod, b_mod):
    return pl.pallas_call(
        _mod_kernel,
        grid=(DEPTH, 3 * D_MODEL // MOD_TN),
        in_specs=[
            pl.BlockSpec((MOD_ROWS, D_MODEL), lambda l, j: (0, 0)),
            pl.BlockSpec((None, D_MODEL, MOD_TN), lambda l, j: (l, 0, j)),
            pl.BlockSpec((None, 1, MOD_TN), lambda l, j: (l, 0, j)),
        ],
        out_specs=pl.BlockSpec((None, MOD_ROWS, MOD_TN), lambda l, j: (l, 0, j)),
        out_shape=jax.ShapeDtypeStruct((DEPTH, MOD_ROWS, 3 * D_MODEL), F32),
        compiler_params=_cparams(2),
        name="modulation",
    )(cond, w_mod, b_mod.reshape(DEPTH, 1, 3 * D_MODEL))


def _head_rms(x, seg, gain):
    tot = _dot((x * x).astype(BF16), seg)
    return x * jax.lax.rsqrt(tot * (1.0 / HEAD_DIM) + RMS_EPS) * gain


def _rope(x, cos, sin_a, sin_b):
    n = x.shape[-1]
    return x * cos + pltpu.roll(x, n - ROPE_PAIR, 1) * sin_a + pltpu.roll(x, ROPE_PAIR, 1) * sin_b


def _inproj_kernel(*refs, latent, seq_tiles, seq_len):
    (x_ref, xp_ref, xn_ref, sh_ref, sc_ref, w_ref, b_ref, qn_ref, kn_ref, seg_ref,
     pw_ref, ps_ref, cw_ref, cb_ref, lg_ref, lb_ref, cpw_ref) = refs[:17]
    refs = refs[17:]
    if latent:
        cos_ref, sa_ref, sb_ref = refs[:3]
        refs = refs[3:]
    else:
        refs = refs[4:]
    a_out_ref, c_out_ref, gates_ref, nat_ref, gqa_ref = refs[:5]
    refs = refs[5:]
    abuf, zbuf, ybuf, cbuf, ubuf = refs[-5:]
    refs = refs[:-5]
    tm = x_ref.shape[0]
    tile = pl.program_id(0) % seq_tiles

    def put_cache(ref, val):
        seq = ref.shape[1]
        for sidx in range(ref.shape[0]):
            ref[sidx] = val[sidx * seq:(sidx + 1) * seq]

    u_ref = refs[-1]
    refs = refs[:-1]

    def modulate(x):
        return (x * (1.0 + sc_ref[...]) + sh_ref[...]).astype(BF16)

    h = modulate(x_ref[...])
    h_ext = jnp.concatenate([modulate(xp_ref[...]), h, modulate(xn_ref[...])], axis=0)

    def proj(c0, n):
        return u_ref[:, c0:c0 + n] + b_ref[:, c0:c0 + n]

    loc = _dot(h_ext, w_ref[:, C_AX:C_AG]) + b_ref[:, C_AX:C_AG]
    row = jax.lax.broadcasted_iota(jnp.int32, (tm + 2 * HALO, 1), 0)
    inside = ((row >= HALO) | (tile > 0)) & ((row < HALO + tm) | (tile < seq_tiles - 1))
    abuf[...] = jnp.where(inside, loc[:, 0:BRANCH_W], 0.0)
    zbuf[...] = jnp.where(inside, loc[:, BRANCH_W:2 * BRANCH_W] * jax.nn.sigmoid(loc[:, 2 * BRANCH_W:]), 0.0)
    pieces = _local_mixers(abuf, zbuf, ybuf, cbuf, ubuf, pw_ref, ps_ref, cw_ref, cb_ref, lg_ref, lb_ref, cpw_ref,
                           tile * tm, seq_len)
    result = []

    def advance(n):
        for _ in range(n):
            if not result:
                try:
                    next(pieces)
                except StopIteration as done:
                    result.append(done.value)

    u_ref[:, :C_AX] = _dot(h, w_ref[:, :C_AX])
    advance(SUBLANES)
    u_ref[:, C_AG:] = _dot(h, w_ref[:, C_AG:])

    gq = _head_rms(proj(C_GQ, BRANCH_W), seg_ref[...], qn_ref[...])
    gk = _head_rms(proj(C_GK, KV_W), seg_ref[0:KV_W, 0:KV_W], kn_ref[...])
    gv = proj(C_GV, KV_W)
    if not latent:
        nk_ref, nv_ref, gk_ref, gv_ref = refs
        put_cache(gk_ref, gk)
        put_cache(gv_ref, gv)
        gqa_ref[:, BRANCH_W:BRANCH_W + KV_W] = gk.astype(BF16)
        gqa_ref[:, BRANCH_W + KV_W:] = gv.astype(BF16)
    else:
        cos, sa, sb = cos_ref[...], sa_ref[...], sb_ref[...]
        gk = _rope(gk, cos, sa, sb)
        gq = _rope(gq, jnp.concatenate([cos, cos], axis=1), jnp.concatenate([sa, sa], axis=1),
                   jnp.concatenate([sb, sb], axis=1))
        k_ref, vt_ref = refs
        k_ref[...] = gk.astype(BF16)
        gvt = gv.T.astype(BF16)
        ones = jnp.ones((VT_ROWS - HEAD_DIM, gvt.shape[1]), BF16)
        for kv in range(GQA_KV_HEADS):
            vt_ref[kv * VT_ROWS:kv * VT_ROWS + HEAD_DIM, :] = gvt[kv * HEAD_DIM:(kv + 1) * HEAD_DIM]
            vt_ref[kv * VT_ROWS + HEAD_DIM:(kv + 1) * VT_ROWS, :] = ones
    gqa_ref[:, 0:BRANCH_W] = (gq * Q_SCALE).astype(BF16)

    while not result:
        advance(1)
    a, c = result[0]
    a_out_ref[...] = (a * _silu(proj(C_AG, BRANCH_W))).astype(BF16)
    c_out_ref[...] = (c * _silu(proj(C_CG, BRANCH_W))).astype(BF16)
    gates_ref[:, 0:BRANCH_W] = _silu(proj(C_NG, BRANCH_W)).astype(BF16)
    gates_ref[:, BRANCH_W:] = _silu(proj(C_GG, BRANCH_W)).astype(BF16)

    nat_ref[:, 0:BRANCH_W] = (proj(C_NQ, BRANCH_W) * Q_SCALE).astype(BF16)
    nk = proj(C_NK, BRANCH_W)
    nv = proj(C_NV, BRANCH_W)
    nat_ref[:, BRANCH_W:2 * BRANCH_W] = nk.astype(BF16)
    nat_ref[:, 2 * BRANCH_W:] = nv.astype(BF16)
    if not latent:
        put_cache(nk_ref, nk)
        put_cache(nv_ref, nv)


def _inproj(x, mod, mod_row0, layer, w_in, b_in, qn, kn, seg, local_w, rope_tabs, new_cache, tm):
    B, L, D = x.shape
    nt = L // tm
    latent = rope_tabs is not None
    seq_tiles = nt if latent else 1
    hb = tm // HALO
    nhb = L // HALO
    tok = lambda w: pl.BlockSpec((None, tm, w), lambda i, b: (b, i, 0))
    per_layer = lambda *shape: _layer_block(shape, layer)
    C = BRANCH_W
    in_specs = [
        tok(D),
        pl.BlockSpec((None, HALO, D), lambda i, b: (b, jnp.maximum(i * hb - 1, 0), 0)),
        pl.BlockSpec((None, HALO, D), lambda i, b: (b, jnp.minimum((i + 1) * hb, nhb - 1), 0)),
        pl.BlockSpec((None, None, 1, D), lambda i, b: (layer, mod_row0 + b, 0, 0)),
        pl.BlockSpec((None, None, 1, D), lambda i, b: (layer, mod_row0 + b, 0, 1)),
        per_layer(D, IN_WIDTH), per_layer(1, IN_WIDTH), per_layer(1, BRANCH_W), per_layer(1, KV_W),
        pl.BlockSpec((BRANCH_W, BRANCH_W), lambda i, b: (0, 0)),
        per_layer(C, C), per_layer(1, C), per_layer(CONV_WIDTH, C), per_layer(1, C), per_layer(1, C), per_layer(1, C),
        per_layer(C, C),
    ]
    args = [x, x, x, mod, mod, w_in, b_in, qn, kn, seg, *local_w]
    if latent:
        in_specs += [pl.BlockSpec((tm, KV_W), lambda i, b: (i, 0))] * 3
        args += list(rope_tabs)
    gqa_w = BRANCH_W if latent else BRANCH_W + 2 * KV_W
    out_specs = [tok(BRANCH_W), tok(BRANCH_W), tok(2 * BRANCH_W), tok(3 * BRANCH_W), tok(gqa_w)]
    out_shape = [jax.ShapeDtypeStruct((B, L, BRANCH_W), BF16), jax.ShapeDtypeStruct((B, L, BRANCH_W), BF16),
                 jax.ShapeDtypeStruct((B, L, 2 * BRANCH_W), BF16), jax.ShapeDtypeStruct((B, L, 3 * BRANCH_W), BF16),
                 jax.ShapeDtypeStruct((B, L, gqa_w), BF16)]
    if latent:
        out_specs += [tok(KV_W), pl.BlockSpec((None, GQA_KV_HEADS * VT_ROWS, tm), lambda i, b: (b, 0, i))]
        out_shape += [jax.ShapeDtypeStruct((B, L, KV_W), BF16),
                      jax.ShapeDtypeStruct((B, GQA_KV_HEADS * VT_ROWS, L), BF16)]
        aliases = {}
    else:
        seq = new_cache[0].shape[2]
        assert B == 1 and tm == seq
        aliases = {len(args) + j: len(out_specs) + j for j in range(len(new_cache))}
        in_specs += [pl.BlockSpec(memory_space=pl.ANY)] * len(new_cache)
        args += list(new_cache)
        out_specs += [pl.BlockSpec((1, None, seq, c.shape[3]), lambda i, b: (i, layer, 0, 0)) for c in new_cache]
        out_shape += [jax.ShapeDtypeStruct(c.shape, c.dtype) for c in new_cache]
    return pl.pallas_call(
        functools.partial(_inproj_kernel, latent=latent, seq_tiles=seq_tiles, seq_len=L if latent else tm),
        grid=(nt, B),
        in_specs=in_specs,
        out_specs=out_specs,
        out_shape=out_shape,
        input_output_aliases=aliases,
        scratch_shapes=[pltpu.VMEM((tm, IN_WIDTH), F32),
                        pltpu.VMEM((tm + 2 * HALO, C), F32), pltpu.VMEM((tm + 2 * HALO, C), F32),
                        pltpu.VMEM((tm, C), F32), pltpu.VMEM((tm, C), F32),
                        pltpu.VMEM((SUBLANES, tm + SUBLANES, C), F32)],
        compiler_params=_cparams(2),
        name="inproj_lat" if latent else "inproj_ctx",
    )(*args)


LOCAL_RC = 64


CONV_CHUNK_MAX = 104


def _conv_chunk(rows):
    return max(c for c in range(SUBLANES, CONV_CHUNK_MAX + 1, SUBLANES) if rows % c == 0)


def _local_mixers(abuf, zbuf, ybuf, cbuf, ubuf, pw_ref, ps_ref, cw_ref, cb_ref, lg_ref, lb_ref, cpw_ref, t0, seq_len):
    tm = ybuf.shape[0]
    LOCAL_UC = _conv_chunk(tm + SUBLANES)

    half_w = CONV_WIDTH // 2
    for res in range(SUBLANES):
        taps = [d for d in range(res - HALO, HALO, SUBLANES) if -half_w <= d <= half_w]
        for u0 in range(0, tm + SUBLANES, LOCAL_UC):
            part = None
            for d in taps:
                term = zbuf[HALO + u0 + d - res:HALO + u0 + d - res + LOCAL_UC, :] * cw_ref[d + half_w:d + half_w + 1, :]
                part = term if part is None else part + term
            ubuf[res, u0:u0 + LOCAL_UC, :] = part
        yield

    assert 2 * POOL_GROUP_W == LANES and POOL_WINDOWS == (2, 4, 8, 16)
    lane = jax.lax.broadcasted_iota(jnp.int32, (LOCAL_RC, LANES), 1)
    low = lane < POOL_GROUP_W
    for r0 in range(0, tm, LOCAL_RC):
        t = t0 + r0 + jax.lax.broadcasted_iota(jnp.int32, (LOCAL_RC, LANES), 0)

        def sh(off, c0):
            return abuf[HALO + r0 + off:HALO + r0 + off + LOCAL_RC, c0:c0 + LANES]

        def centred(total, half, x):
            cnt = jnp.minimum(t + half, seq_len) - jnp.maximum(t - half, 0)
            return total / cnt.astype(F32) - x

        x0 = sh(0, 0)
        s2 = x0 + sh(-1, 0)
        s4 = s2 + sh(-2, 0) + sh(1, 0)
        ybuf[r0:r0 + LOCAL_RC, 0:LANES] = centred(jnp.where(low, s2, s4), jnp.where(low, 1, 2), x0)
        x1 = sh(0, LANES)
        s8 = x1
        for off in (-4, -3, -2, -1, 1, 2, 3):
            s8 = s8 + sh(off, LANES)
        s16 = s8
        for off in (-8, -7, -6, -5, 4, 5, 6, 7):
            s16 = s16 + sh(off, LANES)
        ybuf[r0:r0 + LOCAL_RC, LANES:] = centred(jnp.where(low, s8, s16), jnp.where(low, 4, 8), x1)

        acc = jnp.zeros((LOCAL_RC, BRANCH_W), F32) + cb_ref[...]
        for res in range(SUBLANES):
            acc = acc + ubuf[res, r0 + res:r0 + res + LOCAL_RC, :]
        cbuf[r0:r0 + LOCAL_RC, :] = acc
        yield

    a = _dot(ybuf[...].astype(BF16), pw_ref[...]) * ps_ref[...]

    cz = cbuf[...]
    mu = jnp.mean(cz, axis=-1, keepdims=True)
    d = cz - mu
    var = jnp.mean(d * d, axis=-1, keepdims=True)
    zn = _silu(d * jax.lax.rsqrt(var + LN_EPS) * lg_ref[...] + lb_ref[...])
    return a, _dot(zn.astype(BF16), cpw_ref[...])


def _head_mask(shape, h):
    lane = jax.lax.broadcasted_iota(jnp.int32, shape, 1)
    return (lane >= h * HEAD_DIM) & (lane < (h + 1) * HEAD_DIM)


def _softmax_pv(s, v):
    m = jnp.max(s, axis=-1, keepdims=True)
    p = jnp.exp2(s - m)
    l = jnp.sum(p, axis=-1, keepdims=True)
    return _dot(p.astype(BF16), v) / l


CTX_SEQS = 4


def _attn_ctx_kernel(nat_ref, gqa_ref, n_out_ref, g_out_ref):
    for s in range(nat_ref.shape[0]):
        _attn_ctx_seq(nat_ref.at[s], gqa_ref.at[s], n_out_ref.at[s], g_out_ref.at[s])


def _attn_ctx_seq(nat_ref, gqa_ref, n_out_ref, g_out_ref):
    q, k, v = (nat_ref[:, j * BRANCH_W:(j + 1) * BRANCH_W] for j in range(3))
    acc = jnp.zeros(q.shape, F32)
    for h in range(NAT_HEADS):
        msk = _head_mask(q.shape, h)
        o = _softmax_pv(_dot_t(jnp.where(msk, q, jnp.zeros_like(q)), k), v)
        acc = acc + jnp.where(msk, o, 0.0)
    n_out_ref[...] = acc.astype(BF16)

    k, v = gqa_ref[:, BRANCH_W:BRANCH_W + KV_W], gqa_ref[:, BRANCH_W + KV_W:]
    sides = []
    for side in range(2):
        q = gqa_ref[:, side * KV_W:(side + 1) * KV_W]
        acc = jnp.zeros(q.shape, F32)
        for kv in range(GQA_KV_HEADS):
            msk = _head_mask(q.shape, kv)
            o = _softmax_pv(_dot_t(jnp.where(msk, q, jnp.zeros_like(q)), k), v)
            acc = acc + jnp.where(msk, o, 0.0)
        sides.append(acc)
    low = _head_mask(sides[0].shape, 0)
    g_out_ref[:, 0:KV_W] = jnp.where(low, sides[0], pltpu.roll(sides[1], HEAD_DIM, 1)).astype(BF16)
    g_out_ref[:, KV_W:] = jnp.where(low, pltpu.roll(sides[0], HEAD_DIM, 1), sides[1]).astype(BF16)


def _attn_ctx(nat, gqa):
    B, L, _ = nat.shape
    ns = CTX_SEQS if B % CTX_SEQS == 0 else 1
    return pl.pallas_call(
        _attn_ctx_kernel,
        grid=(B // ns,),
        in_specs=[pl.BlockSpec((ns, L, 3 * BRANCH_W), lambda b: (b, 0, 0)),
                  pl.BlockSpec((ns, L, BRANCH_W + 2 * KV_W), lambda b: (b, 0, 0))],
        out_specs=[pl.BlockSpec((ns, L, BRANCH_W), lambda b: (b, 0, 0))] * 2,
        out_shape=[jax.ShapeDtypeStruct((B, L, BRANCH_W), BF16)] * 2,
        compiler_params=_cparams(1),
        name="attn_ctx",
    )(nat, gqa)


NAT_RB = 8
NAT_NLOC = NAT_ROWS * GRID_W


def _nat_kernel(q_ref, k_ref, v_ref, kc_ref, vc_ref, bias_ref, o_ref, s_a, s_b, *, rows):
    i = pl.program_id(1)
    kc = kc_ref[...]
    vc = vc_ref[...]
    nq = NAT_HEADS * GRID_W
    diag = (jax.lax.broadcasted_iota(jnp.int32, (nq, BRANCH_W), 0) // GRID_W
            == jax.lax.broadcasted_iota(jnp.int32, (nq, BRANCH_W), 1) // HEAD_DIM)
    bufs = (s_a, s_b)

    def window(rr):
        r = i * NAT_RB + rr
        row_start = jnp.clip(r - NAT_ROWS // 2, 0, rows - NAT_ROWS)
        return row_start - r + NAT_ROWS - 1, pl.multiple_of(row_start * GRID_W, GRID_W)

    def scores(rr, buf):
        off, base = window(rr)
        q_r = q_ref[rr * GRID_W:(rr + 1) * GRID_W, :]
        qs = jnp.where(diag, jnp.concatenate([q_r] * NAT_HEADS, axis=0), jnp.zeros((nq, BRANCH_W), BF16))
        bias = jnp.concatenate([bias_ref[off + 2 * jj] for jj in range(NAT_ROWS // 2)], axis=1)
        buf[:, 0:NAT_NLOC] = _dot_t(qs, k_ref[pl.ds(base, NAT_NLOC), :]) + bias
        buf[:, NAT_NLOC:] = _dot_t(qs, kc)

    scores(0, s_a)
    for rr in range(NAT_RB):
        cur = bufs[rr % 2]
        if rr + 1 < NAT_RB:
            scores(rr + 1, bufs[(rr + 1) % 2])
        _, base = window(rr)
        vw = v_ref[pl.ds(base, NAT_NLOC), :]
        m = jnp.max(cur[...], axis=-1, keepdims=True)
        p_loc = jnp.exp2(cur[:, 0:NAT_NLOC] - m)
        p_ctx = jnp.exp2(cur[:, NAT_NLOC:] - m)
        l = jnp.sum(p_loc, axis=-1, keepdims=True) + jnp.sum(p_ctx, axis=-1, keepdims=True)
        o = (_dot(p_loc.astype(BF16), vw) + _dot(p_ctx.astype(BF16), vc)) / l
        o = jnp.where(diag, o, 0.0)
        o_r = o[0:GRID_W]
        for h in range(1, NAT_HEADS):
            o_r = o_r + o[h * GRID_W:(h + 1) * GRID_W]
        o_ref[rr * GRID_W:(rr + 1) * GRID_W, :] = o_r.astype(BF16)


def _nat_lat(nat, ctx_k, ctx_v, layer, bias_tiles):
    B, L, _ = nat.shape
    rows = L // GRID_W
    lc = ctx_k.shape[2]
    tq = NAT_RB * GRID_W
    seq = lambda col: pl.BlockSpec((None, L, BRANCH_W), lambda b, i: (b, 0, col))
    ctx = pl.BlockSpec((None, None, lc, BRANCH_W), lambda b, i: (b, layer, 0, 0))
    return pl.pallas_call(
        functools.partial(_nat_kernel, rows=rows),
        grid=(B, rows // NAT_RB),
        in_specs=[pl.BlockSpec((None, tq, BRANCH_W), lambda b, i: (b, i, 0)), seq(1), seq(2), ctx, ctx,
                  _layer_block(bias_tiles.shape[1:], layer)],
        out_specs=pl.BlockSpec((None, tq, BRANCH_W), lambda b, i: (b, i, 0)),
        out_shape=jax.ShapeDtypeStruct((B, L, BRANCH_W), BF16),
        scratch_shapes=[pltpu.VMEM((NAT_HEADS * GRID_W, NAT_NLOC + lc), F32)] * 2,
        compiler_params=_cparams(2),
        name="nat_lat",
    )(nat, nat, nat, ctx_k, ctx_v, bias_tiles)


GQA_TQ = 256
GQA_NCHUNK = 6


def _gqa_kernel(q_ref, k_lat_ref, vt_lat_ref, k_ctx_ref, vt_ctx_ref, o_ref, k_ref, vt_ref, s_a, s_b):
    n_lat = k_lat_ref.shape[0]
    k_ref[0:n_lat, :] = k_lat_ref[...]
    k_ref[n_lat:, :] = k_ctx_ref[...]
    vt_ref[:, 0:n_lat] = vt_lat_ref[...]
    vt_ref[:, n_lat:] = vt_ctx_ref[...]
    kc_len = k_ref.shape[0] // GQA_NCHUNK
    nqt = q_ref.shape[0] // GQA_TQ
    heads = [(side, kv) for side in range(2) for kv in range(GQA_KV_HEADS)]
    bufs = (s_a, s_b)

    def masked_q(t):
        row = pl.multiple_of(t * GQA_TQ, GQA_TQ)
        out = []
        for side, kv in heads:
            q = q_ref[pl.ds(row, GQA_TQ), side * KV_W:(side + 1) * KV_W]
            out.append(jnp.where(_head_mask(q.shape, kv), q, jnp.zeros_like(q)))
        return out

    qh0 = masked_q(0)
    for h in range(len(heads)):
        s_a[h] = _dot_t(k_ref[0:kc_len, :], qh0[h])

    def qtile(t, carry):
        qh_cur = masked_q(t)
        qh_next = masked_q(jnp.minimum(t + 1, nqt - 1))
        state = [(jnp.full((1, GQA_TQ), NEG_INF, F32), jnp.zeros((VT_ROWS, GQA_TQ), F32)) for _ in heads]
        for c in range(GQA_NCHUNK):
            cur, nxt = bufs[c % 2], bufs[(c + 1) % 2]
            cn = (c + 1) % GQA_NCHUNK
            qn = qh_cur if c + 1 < GQA_NCHUNK else qh_next
            kn = k_ref[cn * kc_len:(cn + 1) * kc_len, :]
            vt = vt_ref[:, c * kc_len:(c + 1) * kc_len]
            for h, (side, kv) in enumerate(heads):
                nxt[h] = _dot_t(kn, qn[h])
                m, acc = state[h]
                s = cur[h]
                m_new = jnp.maximum(m, jnp.max(s, axis=0, keepdims=True))
                p = jnp.exp2(s - m_new).astype(BF16)
                acc = jnp.exp2(m - m_new) * acc + _dot(vt[kv * VT_ROWS:(kv + 1) * VT_ROWS], p)
                state[h] = (m_new, acc)
        outs = [acc[0:HEAD_DIM] / acc[HEAD_DIM:HEAD_DIM + 1] for _, acc in state]
        row = pl.multiple_of(t * GQA_TQ, GQA_TQ)
        for half in range(2):
            o_t = jnp.concatenate([outs[heads.index((side, half))] for side in range(2)], axis=0)
            o_ref[pl.ds(row, GQA_TQ), half * KV_W:(half + 1) * KV_W] = o_t.T.astype(BF16)
        return carry

    jax.lax.fori_loop(0, nqt, qtile, 0)


def _gqa_lat(q, k, vt, ctx_k, ctx_vt, layer):
    B, L, _ = q.shape
    lc = ctx_k.shape[2]
    lk = L + lc
    nvt = GQA_KV_HEADS * VT_ROWS
    assert GQA_NCHUNK % 2 == 0 and lk % (GQA_NCHUNK * LANES) == 0 and L % GQA_TQ == 0 and L % LANES == 0
    return pl.pallas_call(
        _gqa_kernel,
        grid=(B,),
        in_specs=[pl.BlockSpec((None, L, BRANCH_W), lambda b: (b, 0, 0)),
                  pl.BlockSpec((None, L, KV_W), lambda b: (b, 0, 0)),
                  pl.BlockSpec((None, nvt, L), lambda b: (b, 0, 0)),
                  pl.BlockSpec((None, None, lc, KV_W), lambda b: (b, layer, 0, 0)),
                  pl.BlockSpec((None, None, nvt, lc), lambda b: (b, layer, 0, 0))],
        out_specs=pl.BlockSpec((None, L, BRANCH_W), lambda b: (b, 0, 0)),
        out_shape=jax.ShapeDtypeStruct((B, L, BRANCH_W), BF16),
        scratch_shapes=[pltpu.VMEM((lk, KV_W), BF16), pltpu.VMEM((nvt, lk), BF16)]
        + [pltpu.VMEM((GQA_HEADS, lk // GQA_NCHUNK, GQA_TQ), F32)] * 2,
        compiler_params=_cparams(1),
        name="gqa_lat",
    )(q, k, vt, ctx_k, ctx_vt)


def _ctx_prep_kernel(nk_ref, nv_ref, gk_ref, gv_ref, nk_o, nv_o, gk_o, gvt_o):
    nk_o[...] = nk_ref[...].astype(BF16)
    nv_o[...] = nv_ref[...].astype(BF16)
    gk_o[...] = gk_ref[...].astype(BF16)
    gvt = gv_ref[...].T.astype(BF16)
    ones = jnp.ones((VT_ROWS - HEAD_DIM, gvt.shape[1]), BF16)
    for kv in range(GQA_KV_HEADS):
        gvt_o[kv * VT_ROWS:kv * VT_ROWS + HEAD_DIM, :] = gvt[kv * HEAD_DIM:(kv + 1) * HEAD_DIM]
        gvt_o[kv * VT_ROWS + HEAD_DIM:(kv + 1) * VT_ROWS, :] = ones


def _ctx_prep(ck_n, cv_n, ck_g, cv_g):
    B, depth, lc, _ = ck_n.shape
    nvt = GQA_KV_HEADS * VT_ROWS
    blk = lambda r, c: pl.BlockSpec((None, None, r, c), lambda b, l: (b, l, 0, 0))
    return pl.pallas_call(
        _ctx_prep_kernel,
        grid=(B, depth),
        in_specs=[blk(lc, BRANCH_W), blk(lc, BRANCH_W), blk(lc, KV_W), blk(lc, KV_W)],
        out_specs=[blk(lc, BRANCH_W), blk(lc, BRANCH_W), blk(lc, KV_W), blk(nvt, lc)],
        out_shape=[jax.ShapeDtypeStruct((B, depth, lc, BRANCH_W), BF16)] * 2
        + [jax.ShapeDtypeStruct((B, depth, lc, KV_W), BF16), jax.ShapeDtypeStruct((B, depth, nvt, lc), BF16)],
        compiler_params=_cparams(2),
        name="ctx_prep",
    )(ck_n, cv_n, ck_g, cv_g)


OUT_TM = 2048
OUT_RC = 256


def _outproj_kernel(a_ref, n_ref, c_ref, g_ref, gates_ref, x_ref, gm_ref, w_ref, b_ref, lg_ref, lb_ref, o_ref):
    for r0 in range(0, x_ref.shape[0], OUT_RC):
        rows = slice(r0, r0 + OUT_RC)
        mixed = jnp.concatenate([a_ref[rows, :], n_ref[rows, :] * gates_ref[rows, 0:BRANCH_W],
                                 c_ref[rows, :], g_ref[rows, :] * gates_ref[rows, BRANCH_W:]], axis=1)
        out = _dot(mixed, w_ref[...]) + b_ref[...]
        y = DEEPNORM_ALPHA * x_ref[rows, :] + gm_ref[...] * out
        mu = jnp.mean(y, axis=-1, keepdims=True)
        d = y - mu
        var = jnp.mean(d * d, axis=-1, keepdims=True)
        o_ref[rows, :] = d * jax.lax.rsqrt(var + LN_EPS) * lg_ref[...] + lb_ref[...]


def _outproj(a, n, c, g, gates, x, mod, mod_row0, layer, w_out, b_out, ln_g, ln_b):
    B, L, D = x.shape
    tm = min(OUT_TM, L)
    assert L % tm == 0 and tm % OUT_RC == 0
    tok = lambda w: pl.BlockSpec((None, tm, w), lambda i, b: (b, i, 0))
    per_layer = lambda *shape: _layer_block(shape, layer)
    return pl.pallas_call(
        _outproj_kernel,
        grid=(L // tm, B),
        in_specs=[tok(BRANCH_W)] * 4 + [
            tok(2 * BRANCH_W), tok(D),
            pl.BlockSpec((None, None, 1, D), lambda i, b: (layer, mod_row0 + b, 0, 2)),
            per_layer(D, D), per_layer(1, D), per_layer(1, D), per_layer(1, D)],
        out_specs=tok(D),
        out_shape=jax.ShapeDtypeStruct((B, L, D), F32),
        compiler_params=_cparams(2),
        name="outproj",
    )(a, n, c, g, gates, x, mod, w_out, b_out, ln_g, ln_b)


def _rope_tables(seq_len):
    half = HEAD_DIM // 2
    nf = half // 2
    t = np.arange(seq_len)
    inv = np.float32(ROPE_THETA) ** (-np.arange(nf, dtype=np.float32) * np.float32(2.0) / np.float32(half))

    def tabs(pos):
        ang = pos.astype(np.float32)[:, None] * inv[None, :]
        return np.cos(ang), np.sin(ang)

    cr, sr = tabs(t // GRID_W)
    cc, sc = tabs(t % GRID_W)
    zero = np.zeros_like(sr)
    cos = np.concatenate([cr, cr, cc, cc], axis=1)
    sin_a = np.concatenate([-sr, zero, -sc, zero], axis=1)
    sin_b = np.concatenate([zero, sr, zero, sc], axis=1)
    return tuple(jnp.asarray(np.tile(a, (1, 2)), F32) for a in (cos, sin_a, sin_b))


def _nat_bias_tiles(bias_tab):
    col = np.arange(GRID_W)
    col_start = np.clip(col - NAT_COLS // 2, 0, GRID_W - NAT_COLS)
    col_in = (col[None, :] >= col_start[:, None]) & (col[None, :] < col_start[:, None] + NAT_COLS)
    edge = GRID_W - NAT_COLS
    n_diff = 2 * GRID_W - 1
    padded = jnp.concatenate([jnp.repeat(bias_tab[..., :1], edge, axis=-1), bias_tab,
                              jnp.repeat(bias_tab[..., -1:], edge, axis=-1)], axis=-1)
    flat = jnp.tile(padded, (1, 1, 1, GRID_W + 1))
    toeplitz = flat[..., GRID_W - 1:GRID_W - 1 + GRID_W * (n_diff - 1)].reshape(
        bias_tab.shape[:-1] + (GRID_W, n_diff - 1))[..., :GRID_W]
    full = jnp.where(col_in, toeplitz * LOG2E, NEG_INF)
    pair = jnp.concatenate([full[:, :, :-1], full[:, :, 1:]], axis=-1)
    return jnp.transpose(pair, (0, 2, 1, 3, 4)).reshape(-1, 2 * NAT_ROWS - 2, NAT_HEADS * GRID_W, 2 * GRID_W)


def kernel(x_prompt, x_sample, c, cache_nat_k, cache_nat_v, cache_gqa_k, cache_gqa_v, c_ctx, w_mod, b_mod, w_in, b_in,
           pool_w, pool_scale, nat_bias, q_norm, k_norm, conv_w, conv_b, conv_ln_g, conv_ln_b, conv_pw, w_out, b_out,
           ln_g, ln_b):
    nb, seq, D = x_prompt.shape
    db, dseq, _ = x_sample.shape
    lc = cache_nat_k.shape[2]

    cond = jnp.zeros((MOD_ROWS, D), F32).at[0].set(c_ctx).at[1:1 + db].set(c)
    mod = _modulation(cond, w_mod, b_mod).reshape(DEPTH, MOD_ROWS, 1, 3 * D)

    def kernel_columns(a):
        return jnp.concatenate([a[..., c0:c0 + w] for _, c0, w in _IN_SEGMENTS], axis=-1)

    w_in_b = kernel_columns(w_in).astype(BF16)
    b_in_p = kernel_columns(b_in).reshape(DEPTH, 1, IN_WIDTH)
    w_out_b = w_out.astype(BF16)

    seg = jnp.asarray(np.kron(np.eye(GQA_HEADS), np.ones((HEAD_DIM, HEAD_DIM))), BF16)
    rope_tabs = _rope_tables(dseq)
    ck_n, cv_n, ck_g, cvt_g = _ctx_prep(cache_nat_k.reshape(db, DEPTH, lc, BRANCH_W),
                                        cache_nat_v.reshape(db, DEPTH, lc, BRANCH_W),
                                        cache_gqa_k.reshape(db, DEPTH, lc, KV_W),
                                        cache_gqa_v.reshape(db, DEPTH, lc, KV_W))

    row = lambda a: a.reshape(DEPTH, 1, -1)
    n_groups = len(POOL_WINDOWS)
    pool_bd = jnp.einsum('gh,lgcd->lgchd', jnp.eye(n_groups, dtype=F32), pool_w).reshape(DEPTH, BRANCH_W, BRANCH_W)
    in_w = (w_in_b, b_in_p, row(jnp.tile(q_norm, (1, GQA_HEADS))), row(jnp.tile(k_norm, (1, GQA_KV_HEADS))), seg)
    local_w = (pool_bd.astype(BF16), row(pool_scale), conv_w, row(conv_b), row(conv_ln_g), row(conv_ln_b),
               conv_pw.astype(BF16))
    out_w = (w_out_b, row(b_out), row(ln_g), row(ln_b))
    bias_tiles = _nat_bias_tiles(nat_bias)

    y_p = x_prompt.reshape(1, nb * seq, D)
    y_s = x_sample
    new_cache = [jnp.zeros((nb, DEPTH, seq, w), F32) for w in (BRANCH_W, BRANCH_W, KV_W, KV_W)]
    for l in range(DEPTH):
        a_out, c_out, gates, nat, gqa, *new_cache = _inproj(y_p, mod, 0, l, *in_w, local_w, None, new_cache, seq)
        n_out, g_out = _attn_ctx(nat.reshape(nb, seq, -1), gqa.reshape(nb, seq, -1))
        flat = lambda a: a.reshape(1, nb * seq, BRANCH_W)
        y_p = _outproj(a_out, flat(n_out), c_out, flat(g_out), gates, y_p, mod, 0, l, *out_w)

        a_out, c_out, gates, nat, gq, gk, gvt = _inproj(y_s, mod, 1, l, *in_w, local_w, rope_tabs, None, PROJ_TM)
        n_out = _nat_lat(nat, ck_n, cv_n, l, bias_tiles)
        g_out = _gqa_lat(gq, gk, gvt, ck_g, cvt_g, l)
        y_s = _outproj(a_out, n_out, c_out, g_out, gates, y_s, mod, 1, l, *out_w)

    heads = (NAT_HEADS, NAT_HEADS, GQA_KV_HEADS, GQA_KV_HEADS)
    return (y_p.reshape(nb, seq, D), y_s) + tuple(c.reshape(nb, DEPTH, seq, h, HEAD_DIM)
                                                  for c, h in zip(new_cache, heads))
```
